```python
import jax, jax.numpy as jnp
from jax import lax
import numpy as np

D_MODEL = 1024
BATCH = 4
SEQ = 4096
DEPTH = 1

PLE_DIM = 256
EPS = 1e-6
A_HEADS = 4
A_DK = 128
A_DV = 128
A_CONV = 4
A_CHUNK = 64
A_WIDTH = A_HEADS * A_DV
A_CONV_CH = 2 * A_HEADS * A_DK + A_WIDTH
B_HEADS = 8
B_KV_HEADS = 2
B_HD = 64
B_WIDTH = B_HEADS * B_HD
IDX_HEADS = 8
IDX_DIM = 128
TOPK_MAX = 256
Q_BLOCK = 128

MIX_WIDTH = A_WIDTH + B_WIDTH
IN_SIZES = (
    A_HEADS * A_DK,
    A_HEADS * A_DK,
    A_WIDTH,
    A_WIDTH,
    A_HEADS,
    A_HEADS,
    B_WIDTH,
    B_KV_HEADS * B_HD,
    B_KV_HEADS * B_HD,
    B_WIDTH,
    IDX_HEADS * IDX_DIM,
    IDX_DIM,
    IDX_HEADS,
)
IN_WIDTH = sum(IN_SIZES)
IN_OFFSETS = tuple(int(s) for s in np.cumsum(IN_SIZES)[:-1])

kernel_name = "hybrid_gdn_dsa_parallel_heads"


def rms_norm(x, gain):
    xf = x.astype(jnp.float32)
    y = xf * lax.rsqrt(jnp.mean(xf * xf, axis=-1, keepdims=True) + EPS)
    return (y * gain.astype(jnp.float32)).astype(x.dtype)


def l2_norm(x):
    return x * lax.rsqrt(jnp.sum(x * x, axis=-1, keepdims=True) + EPS)


def causal_dwconv(x, w):
    c = x.shape[-1]
    return lax.conv_general_dilated(
        x, w[:, None, :].astype(x.dtype), window_strides=(1,), padding=[(A_CONV - 1, 0)],
        dimension_numbers=('NWC', 'WIO', 'NWC'), feature_group_count=c)


def gated_delta_rule_chunked(q, k, v, g, beta):
    bn, seq_len, nh, dk = q.shape
    dv = v.shape[-1]
    c = A_CHUNK
    nc = seq_len // c

    def chunks(t):
        t = jnp.moveaxis(t, 2, 1)
        return t.reshape(bn, nh, nc, c, *t.shape[3:])

    q, k, v, g, beta = chunks(q), chunks(k), chunks(v), chunks(g), chunks(beta)
    g_cum = jnp.cumsum(g, axis=-1)
    tril = jnp.tril(jnp.ones((c, c), dtype=bool))
    strict = jnp.tril(jnp.ones((c, c), dtype=bool), -1)
    diff = g_cum[..., :, None] - g_cum[..., None, :]
    decay = jnp.where(tril, jnp.exp(jnp.where(tril, diff, 0.0)), 0.0)
    k_beta = k * beta[..., None]
    v_beta = v * beta[..., None]
    eye = jnp.eye(c, dtype=jnp.float32)
    kkt = jnp.einsum('bhncd,bhnsd->bhncs', k_beta, k) * decay
    a_mat = eye + jnp.where(strict, kkt, 0.0)
    t_mat = lax.linalg.triangular_solve(a_mat, jnp.broadcast_to(eye, a_mat.shape),
                                        left_side=True, lower=True, unit_diagonal=True)
    value = jnp.einsum('bhncs,bhnse->bhnce', t_mat, v_beta)
    k_cumdecay = jnp.einsum('bhncs,bhnsd->bhncd', t_mat, k_beta * jnp.exp(g_cum)[..., None])
    attn_intra = jnp.einsum('bhncd,bhnsd->bhncs', q, k) * decay
    q_decay = q * jnp.exp(g_cum)[..., None]
    k_decay = k * jnp.exp(g_cum[..., -1:] - g_cum)[..., None]
    g_last = jnp.exp(g_cum[..., -1])

    def step(state, inp):
        qd, kd, val, kcd, att, gl = inp
        v_new = val - jnp.einsum('bhcd,bhde->bhce', kcd, state)
        o = jnp.einsum('bhcd,bhde->bhce', qd, state) + jnp.einsum('bhcs,bhse->bhce', att, v_new)
        state = state * gl[..., None, None] + jnp.einsum('bhcd,bhce->bhde', kd, v_new)
        return state, o

    xs = tuple(jnp.moveaxis(t, 2, 0) for t in (q_decay, k_decay, value, k_cumdecay, attn_intra, g_last))
    s0 = jnp.zeros((bn, nh, dk, dv), jnp.float32)
    _, o = lax.scan(step, s0, xs)
    return o.transpose(1, 0, 3, 2, 4).reshape(bn, seq_len, nh, dv)


def dsa_sparse_attention(q, k, v, iq, ik, iw):
    bn, seq_len = q.shape[:2]
    n_sel = min(TOPK_MAX, seq_len // 4)
    n_blocks = seq_len // Q_BLOCK
    rep = B_HEADS // B_KV_HEADS
    qg = q.reshape(bn, seq_len, B_KV_HEADS, rep, B_HD)
    key_pos = jnp.arange(seq_len)
    scale = B_HD ** -0.5
    gather = jax.vmap(lambda tb, ib: tb[ib])

    def block(i):
        start = i * Q_BLOCK
        q_b = lax.dynamic_slice_in_dim(qg, start, Q_BLOCK, axis=1)
        iq_b = lax.dynamic_slice_in_dim(iq, start, Q_BLOCK, axis=1)
        iw_b = lax.dynamic_slice_in_dim(iw, start, Q_BLOCK, axis=1)
        q_pos = start + jnp.arange(Q_BLOCK)
        logits = jnp.einsum('bqhd,bsd->bqhs', iq_b, ik)
        score = jnp.einsum('bqh,bqhs->bqs', iw_b.astype(jnp.float32),
                           jax.nn.relu(logits.astype(jnp.float32)))
        causal = key_pos[None, :] <= q_pos[:, None]
        score = jnp.where(causal[None], score, -jnp.inf)
        _, idx = lax.top_k(score, n_sel)
        k_sel = gather(k, idx)
        v_sel = gather(v, idx)
        s = jnp.einsum('bqgrd,bqkgd->bqgrk', q_b, k_sel).astype(jnp.float32) * scale
        valid = idx <= q_pos[None, :, None]
        s = jnp.where(valid[:, :, None, None, :], s, -jnp.inf)
        prob = jax.nn.softmax(s, axis=-1).astype(v.dtype)
        o = jnp.einsum('bqgrk,bqkgd->bqgrd', prob, v_sel)
        return o.reshape(bn, Q_BLOCK, B_WIDTH)

    out = lax.map(block, jnp.arange(n_blocks))
    return out.transpose(1, 0, 2, 3).reshape(bn, seq_len, B_WIDTH)


def setup_inputs(seed: int = 0) -> dict:
    key = jax.random.key(seed)
    ks = jax.random.split(key, 16)
    f32 = jnp.float32
    x = jax.random.normal(ks[0], (BATCH, SEQ, D_MODEL), f32)
    p = jax.random.normal(ks[1], (DEPTH, BATCH, SEQ, PLE_DIM), f32)
    attn_norm_w = 1.0 + 0.02 * jax.random.normal(ks[2], (DEPTH, D_MODEL), f32)
    w_in = jax.random.normal(ks[3], (DEPTH, D_MODEL, IN_WIDTH), f32) * D_MODEL ** -0.5
    conv_w = jax.random.normal(ks[4], (DEPTH, A_CONV, A_CONV_CH), f32) * A_CONV ** -0.5
    a_log = jnp.log(jax.random.uniform(ks[5], (DEPTH, A_HEADS), f32, minval=1.0, maxval=16.0))
    dt = jnp.exp(jax.random.uniform(ks[6], (DEPTH, A_HEADS), f32,
                                    minval=float(np.log(1e-3)), maxval=float(np.log(1e-1))))
    dt_bias = dt + jnp.log(-jnp.expm1(-dt))
    a_out_norm_w = 1.0 + 0.02 * jax.random.normal(ks[7], (DEPTH, A_DV), f32)
    b_q_norm_w = 1.0 + 0.02 * jax.random.normal(ks[8], (DEPTH, B_HD), f32)
    b_k_norm_w = 1.0 + 0.02 * jax.random.normal(ks[9], (DEPTH, B_HD), f32)
    w_out = jax.random.normal(ks[10], (DEPTH, MIX_WIDTH, D_MODEL), f32) * MIX_WIDTH ** -0.5
    w_ple = jax.random.normal(ks[11], (DEPTH, PLE_DIM, D_MODEL), f32) * PLE_DIM ** -0.5
    ple_gate_norm_w = 1.0 + 0.02 * jax.random.normal(ks[12], (DEPTH, D_MODEL), f32)
    w_ple_gate = jax.random.normal(ks[13], (DEPTH, D_MODEL, D_MODEL), f32) * D_MODEL ** -0.5
    b_ple_gate = 0.01 * jax.random.normal(ks[14], (DEPTH, D_MODEL), f32)
    return {"x": x, "p": p, "attn_norm_w": attn_norm_w, "w_in": w_in, "conv_w": conv_w,
            "a_log": a_log, "dt_bias": dt_bias, "a_out_norm_w": a_out_norm_w,
            "b_q_norm_w": b_q_norm_w, "b_k_norm_w": b_k_norm_w, "w_out": w_out,
            "w_ple": w_ple, "ple_gate_norm_w": ple_gate_norm_w, "w_ple_gate": w_ple_gate,
            "b_ple_gate": b_ple_gate}


def reference(x, p, attn_norm_w, w_in, conv_w, a_log, dt_bias, a_out_norm_w, b_q_norm_w,
              b_k_norm_w, w_out, w_ple, ple_gate_norm_w, w_ple_gate, b_ple_gate):
    bn, seq_len, _ = x.shape
    f32 = jnp.float32
    for i in range(DEPTH):
        h = rms_norm(x, attn_norm_w[i])
        proj = h @ w_in[i]
        (a_q, a_k, a_v, a_z, a_b, a_a, b_q, b_k, b_v, b_z,
         i_q, i_k, i_w) = jnp.split(proj, IN_OFFSETS, axis=-1)

        qkv = jax.nn.silu(causal_dwconv(jnp.concatenate([a_q, a_k, a_v], axis=-1), conv_w[i]))
        aq, ak, av = jnp.split(qkv, (A_HEADS * A_DK, 2 * A_HEADS * A_DK), axis=-1)
        aq = l2_norm(aq.astype(f32).reshape(bn, seq_len, A_HEADS, A_DK)) * (A_DK ** -0.5)
        ak = l2_norm(ak.astype(f32).reshape(bn, seq_len, A_HEADS, A_DK))
        av = av.astype(f32).reshape(bn, seq_len, A_HEADS, A_DV)
        beta = jax.nn.sigmoid(a_b.astype(f32))
        g = -jnp.exp(a_log[i].astype(f32)) * jax.nn.softplus(a_a.astype(f32) + dt_bias[i].astype(f32))
        o_a = gated_delta_rule_chunked(aq, ak, av, g, beta)
        o_a = rms_norm(o_a, a_out_norm_w[i]).reshape(bn, seq_len, A_WIDTH).astype(x.dtype)
        o_a = o_a * jax.nn.silu(a_z)

        bq = rms_norm(b_q.reshape(bn, seq_len, B_HEADS, B_HD), b_q_norm_w[i])
        bk = rms_norm(b_k.reshape(bn, seq_len, B_KV_HEADS, B_HD), b_k_norm_w[i])
        bv = b_v.reshape(bn, seq_len, B_KV_HEADS, B_HD)
        iq = i_q.reshape(bn, seq_len, IDX_HEADS, IDX_DIM)
        iw = i_w * (IDX_HEADS ** -0.5 * IDX_DIM ** -0.5)
        o_b = dsa_sparse_attention(bq, bk, bv, iq, i_k, iw) * jax.nn.silu(b_z)

        x = x + jnp.concatenate([o_a, o_b], axis=-1) @ w_out[i]

        gate = jax.nn.sigmoid(rms_norm(x, ple_gate_norm_w[i]) @ w_ple_gate[i] + b_ple_gate[i])
        x = x + (p[i] @ w_ple[i]) * gate
    return x
```

```python
import functools

import jax
import jax.numpy as jnp
from jax import lax
from jax.experimental import pallas as pl
from jax.experimental.pallas import tpu as pltpu

F32 = jnp.float32
BF16 = jnp.bfloat16
I32 = jnp.int32
EPS = 1e-6
HI = lax.Precision.HIGHEST

PLE_DIM = 256
A_HEADS = 4
A_DK = 128
A_DV = 128
A_CONV = 4
A_CHUNK = 64
A_WIDTH = A_HEADS * A_DV
A_QKV = 2 * A_HEADS * A_DK + A_WIDTH
B_HEADS = 8
B_KV_HEADS = 2
B_HD = 64
B_WIDTH = B_HEADS * B_HD
B_KV_WIDTH = B_KV_HEADS * B_HD
IDX_HEADS = 8
IDX_DIM = 128
IDX_WIDTH = IDX_HEADS * IDX_DIM
TOPK_MAX = 256
LANES = 128
SMALL_W = LANES
SM_BETA = 0
SM_DECAY = A_HEADS
SM_IW = 2 * A_HEADS

VMEM_LIMIT = 56 * 1024 * 1024
NEG_BIG = -1e30
INT_MIN = -(2 ** 31)


def _dot(a, b, prec=None):
    return jnp.dot(a, b, preferred_element_type=F32, precision=prec)


def _dot_nt(a, b, prec=None):
    return lax.dot_general(a, b, (((1,), (1,)), ((), ())), preferred_element_type=F32, precision=prec)


def _dot_tn(a, b, prec=None):
    return lax.dot_general(a, b, (((0,), (0,)), ((), ())), preferred_element_type=F32, precision=prec)


def _silu(x):
    return x * jax.nn.sigmoid(x)


def _softplus(x):
    return jnp.maximum(x, 0.0) + jnp.log1p(jnp.exp(-jnp.abs(x)))


def _seg_norm64(xb, gain_row):
    lane = lax.broadcasted_iota(I32, xb.shape, 1)
    lo = lane < B_HD
    sq = xb * xb
    s_lo = jnp.sum(jnp.where(lo, sq, 0.0), axis=-1, keepdims=True)
    s_hi = jnp.sum(jnp.where(lo, 0.0, sq), axis=-1, keepdims=True)
    ms = jnp.where(lo, s_lo, s_hi) * (1.0 / B_HD)
    return xb * lax.rsqrt(ms + EPS) * gain_row


_WB_QKV = 0
_WB_AZ = _WB_QKV + A_QKV
_WB_BQ = _WB_AZ + A_WIDTH
_WB_BK = _WB_BQ + B_WIDTH
_WB_BV = _WB_BK + B_KV_WIDTH
_WB_BZ = _WB_BV + B_KV_WIDTH
_WB_END = _WB_BZ + B_WIDTH
_WF_IQ = 0
_WF_IK = _WF_IQ + IDX_WIDTH
_WF_SM = _WF_IK + IDX_DIM
_WF_END = _WF_SM + SMALL_W


def _proj_kernel(x_ref, nw_ref, wb_ref, wf_ref, qg_ref, kg_ref,
                 qkv_ref, az_ref, bq_ref, bk_ref, bv_ref, bz_ref, iq_ref, ik_ref, sm_ref):
    x = x_ref[...]
    h = x * lax.rsqrt(jnp.mean(x * x, axis=-1, keepdims=True) + EPS) * nw_ref[...]
    hb = h.astype(BF16)
    step = 512
    for c0 in range(0, A_QKV, step):
        qkv_ref[:, c0:c0 + step] = _dot(hb, wb_ref[:, _WB_QKV + c0:_WB_QKV + c0 + step])
    az_ref[...] = _dot(hb, wb_ref[:, _WB_AZ:_WB_BQ])
    bz_ref[...] = _dot(hb, wb_ref[:, _WB_BZ:_WB_END])
    bv_ref[...] = _dot(hb, wb_ref[:, _WB_BV:_WB_BZ])
    bk = _dot(hb, wb_ref[:, _WB_BK:_WB_BV])
    bk_ref[...] = _seg_norm64(bk, kg_ref[...])
    bq = _dot(hb, wb_ref[:, _WB_BQ:_WB_BK])
    scale = B_HD ** -0.5
    for c0 in range(0, B_WIDTH, LANES):
        bq_ref[:, c0:c0 + LANES] = _seg_norm64(bq[:, c0:c0 + LANES], qg_ref[...]) * scale
    for c0 in range(0, IDX_WIDTH, step):
        iq_ref[:, c0:c0 + step] = _dot(h, wf_ref[:, _WF_IQ + c0:_WF_IQ + c0 + step], HI)
    ik_ref[...] = _dot(h, wf_ref[:, _WF_IK:_WF_SM], HI)
    sm_ref[...] = _dot(h, wf_ref[:, _WF_SM:_WF_END], HI)


def _project(x2, norm_w, w_in, q_gain, k_gain, tm):
    t, d = x2.shape
    sizes = (A_HEADS * A_DK, A_HEADS * A_DK, A_WIDTH, A_WIDTH, A_HEADS, A_HEADS, B_WIDTH, B_KV_WIDTH,
             B_KV_WIDTH, B_WIDTH, IDX_WIDTH, IDX_DIM, IDX_HEADS)
    offs = [0]
    for s in sizes:
        offs.append(offs[-1] + s)
    (a_q, a_k, a_v, a_z, a_b, a_a, b_q, b_k, b_v, b_z, i_q, i_k, i_w) = [
        w_in[:, offs[n]:offs[n + 1]] for n in range(len(sizes))]
    wb = jnp.concatenate([a_q, a_k, a_v, a_z, b_q, b_k, b_v, b_z], axis=1).astype(BF16)
    pad = jnp.zeros((d, SMALL_W - 2 * A_HEADS - IDX_HEADS), F32)
    wf = jnp.concatenate([i_q, i_k, a_b, a_a, i_w, pad], axis=1)
    qg = jnp.tile(q_gain, LANES // B_HD)[None, :]
    kg = jnp.tile(k_gain, LANES // B_HD)[None, :]
    row = lambda w: pl.BlockSpec((tm, w), lambda i: (i, 0))
    full = lambda a: pl.BlockSpec(a.shape, lambda i: (0, 0))
    widths = (A_QKV, A_WIDTH, B_WIDTH, B_KV_WIDTH, B_KV_WIDTH, B_WIDTH, IDX_WIDTH, IDX_DIM, SMALL_W)
    nw = norm_w[None, :]
    return pl.pallas_call(
        _proj_kernel,
        grid=(t // tm,),
        in_specs=[row(d), full(nw), full(wb), full(wf), full(qg), full(kg)],
        out_specs=[row(w) for w in widths],
        out_shape=[jax.ShapeDtypeStruct((t, w), F32) for w in widths],
        compiler_params=pltpu.CompilerParams(dimension_semantics=("arbitrary",), vmem_limit_bytes=VMEM_LIMIT),
        name="proj",
    )(x2, nw, wb, wf, qg, kg)


def _gdn_kernel(qkv_ref, sm_ref, smt_ref, az_ref, cw_ref, alog_r_ref, dtb_r_ref, alog_c_ref, dtb_c_ref,
                onw_ref, o_ref, s_ref, tail_ref, *, bt):
    c = A_CHUNK

    @pl.when(pl.program_id(1) == 0)
    def _():
        s_ref[...] = jnp.zeros(s_ref.shape, F32)
        tail_ref[...] = jnp.zeros(tail_ref.shape, F32)

    xin = qkv_ref[...]
    tail = tail_ref[...]
    cw = cw_ref[...]
    row8 = lax.broadcasted_iota(I32, tail.shape, 0)
    acc = xin * cw[A_CONV - 1:A_CONV, :]
    for d in range(1, A_CONV):
        xr = pltpu.roll(xin, d, axis=0)
        pr = pltpu.roll(tail, d, axis=0)
        head = jnp.where(row8 < d, pr, xr[0:8])
        xs = jnp.concatenate([head, xr[8:]], axis=0)
        acc = acc + xs * cw[A_CONV - 1 - d:A_CONV - d, :]
    tail_ref[...] = xin[bt - 8:bt]
    qkv = _silu(acc)

    sm = sm_ref[...]
    smt = smt_ref[0]
    g_col = -jnp.exp(alog_r_ref[...]) * _softplus(sm + dtb_r_ref[...])
    g_row = -jnp.exp(alog_c_ref[...]) * _softplus(smt + dtb_c_ref[...])
    beta_all = jax.nn.sigmoid(sm)

    rb = lax.broadcasted_iota(I32, (bt, bt), 0)
    cb = lax.broadcasted_iota(I32, (bt, bt), 1)
    same = (rb // c) == (cb // c)
    ltri = jnp.where(same & (cb <= rb), 1.0, 0.0).astype(F32)
    utri = jnp.where(same & (rb <= cb), 1.0, 0.0).astype(F32)
    gc_col = _dot(ltri, g_col, HI)
    gc_row = _dot(g_row, utri, HI)
    egc_all = jnp.exp(gc_col)

    r = lax.broadcasted_iota(I32, (c, c), 0)
    s = lax.broadcasted_iota(I32, (c, c), 1)
    tril = s <= r
    strict = s < r
    eye = jnp.where(s == r, 1.0, 0.0).astype(F32)
    ones_dk = jnp.ones((A_DK, c), F32)
    onw = onw_ref[...]

    for ci in range(bt // c):
        r0 = ci * c
        gtot = _dot(ones_dk, g_col[r0:r0 + c], HI)
        for h in range(A_HEADS):
            q = qkv[r0:r0 + c, h * A_DK:(h + 1) * A_DK]
            k = qkv[r0:r0 + c, A_HEADS * A_DK + h * A_DK:A_HEADS * A_DK + (h + 1) * A_DK]
            v = qkv[r0:r0 + c, 2 * A_HEADS * A_DK + h * A_DV:2 * A_HEADS * A_DK + (h + 1) * A_DV]
            q = q * lax.rsqrt(jnp.sum(q * q, axis=-1, keepdims=True) + EPS) * (A_DK ** -0.5)
            k = k * lax.rsqrt(jnp.sum(k * k, axis=-1, keepdims=True) + EPS)
            beta = beta_all[r0:r0 + c, SM_BETA + h:SM_BETA + h + 1]
            gcc = gc_col[r0:r0 + c, SM_DECAY + h:SM_DECAY + h + 1]
            gcr = gc_row[SM_DECAY + h:SM_DECAY + h + 1, r0:r0 + c]
            egc = egc_all[r0:r0 + c, SM_DECAY + h:SM_DECAY + h + 1]
            glast = gtot[:, SM_DECAY + h:SM_DECAY + h + 1]
            decay = jnp.where(tril, jnp.exp(jnp.where(tril, gcc - gcr, 0.0)), 0.0)
            kb = k * beta
            vb = v * beta
            a_mat = jnp.where(strict, _dot_nt(kb, k, HI) * decay, 0.0)
            t_mat = eye - a_mat
            a_pow = a_mat
            n = 2
            while n < c:
                a_pow = _dot(a_pow, a_pow, HI)
                t_mat = t_mat + _dot(t_mat, a_pow, HI)
                n *= 2
            value = _dot(t_mat, vb, HI)
            kcd = _dot(t_mat, kb * egc, HI)
            att = _dot_nt(q, k, HI) * decay
            qd = q * egc
            kd = k * jnp.exp(glast[0:c] - gcc)
            state = s_ref[h]
            v_new = value - _dot(kcd, state, HI)
            o = _dot(qd, state, HI) + _dot(att, v_new, HI)
            s_ref[h] = state * jnp.exp(glast) + _dot_tn(kd, v_new, HI)
            on = o * lax.rsqrt(jnp.mean(o * o, axis=-1, keepdims=True) + EPS) * onw
            z = az_ref[r0:r0 + c, h * A_DV:(h + 1) * A_DV]
            o_ref[r0:r0 + c, h * A_DV:(h + 1) * A_DV] = on * _silu(z)


def _gdn(qkv, small, small_t, a_z, conv_w, a_log, dt_bias, out_norm_w, batch, seq, bt):
    nt = seq // bt
    lane_pad = lambda v: jnp.zeros((1, LANES), F32).at[0, SM_DECAY:SM_DECAY + A_HEADS].set(v)
    sub_pad = lambda v: jnp.zeros((8, 1), F32).at[SM_DECAY:SM_DECAY + A_HEADS, 0].set(v)
    cw = jnp.zeros((8, A_QKV), F32).at[:A_CONV].set(conv_w)
    onw = out_norm_w[None, :]
    row = lambda w: pl.BlockSpec((bt, w), lambda b, t: (b * nt + t, 0))
    full = lambda a: pl.BlockSpec(a.shape, lambda b, t: (0, 0))
    consts = (cw, lane_pad(a_log), lane_pad(dt_bias), sub_pad(a_log), sub_pad(dt_bias), onw)
    return pl.pallas_call(
        functools.partial(_gdn_kernel, bt=bt),
        grid=(batch, nt),
        in_specs=[row(A_QKV), row(SMALL_W), pl.BlockSpec((1, 8, bt), lambda b, t: (b, 0, t)), row(A_WIDTH)]
        + [full(a) for a in consts],
        out_specs=row(A_WIDTH),
        out_shape=jax.ShapeDtypeStruct((batch * seq, A_WIDTH), F32),
        scratch_shapes=[pltpu.VMEM((A_HEADS, A_DK, A_DV), F32), pltpu.VMEM((8, A_QKV), F32)],
        compiler_params=pltpu.CompilerParams(dimension_semantics=("arbitrary", "arbitrary"),
                                             vmem_limit_bytes=VMEM_LIMIT),
        name="gdn",
    )(qkv, small, small_t, a_z, *consts)


def _dsa_kernel(q_ref, k_ref, v_ref, iq_ref, ik_ref, sm_ref, bz_ref, o_ref,
                keys_ref, m_ref, l_ref, acc_ref, *, qb, n_sel):
    kt = qb
    rep = B_HEADS // B_KV_HEADS
    i = pl.program_id(1)
    nt = i + 1
    row = lax.broadcasted_iota(I32, (qb, kt), 0)
    col = lax.broadcasted_iota(I32, (qb, kt), 1)
    qpos = i * qb + row

    iq = iq_ref[...]
    iqs = jnp.concatenate([iq[:, h * IDX_DIM:(h + 1) * IDX_DIM] for h in range(IDX_HEADS)], axis=0)
    iw = sm_ref[...] * (IDX_HEADS ** -0.5 * IDX_DIM ** -0.5)

    def score_body(j, carry):
        ikj = ik_ref[pl.ds(pl.multiple_of(j * kt, kt), kt), :]
        lg = _dot_nt(iqs, ikj, HI)
        sc = jnp.zeros((qb, kt), F32)
        for h in range(IDX_HEADS):
            sc = sc + iw[:, SM_IW + h:SM_IW + h + 1] * jnp.maximum(lg[h * qb:(h + 1) * qb], 0.0)
        sc = jnp.where(sc == 0.0, 0.0, sc)
        sc = jnp.where(j * kt + col <= qpos, sc, -jnp.inf)
        bits = pltpu.bitcast(sc, I32)
        keys_ref[j] = bits ^ ((bits >> 31) & 0x7FFFFFFF)
        return carry

    lax.fori_loop(0, nt, score_body, 0)

    def count_ge(cand):
        candb = jnp.broadcast_to(cand, (qb, kt))

        def body(j, acc):
            return acc + jnp.where(keys_ref[j] >= candb, 1.0, 0.0)

        acc = lax.fori_loop(0, nt, body, jnp.zeros((qb, kt), F32))
        return jnp.sum(acc, axis=-1, keepdims=True)

    def bit_body(bi, tau):
        cand = tau ^ lax.shift_left(jnp.int32(1), 31 - bi)
        return jnp.where(count_ge(cand) >= n_sel, cand, tau)

    tau = lax.fori_loop(0, 32, bit_body, jnp.full((qb, 1), INT_MIN, I32))
    taub = jnp.broadcast_to(tau, (qb, kt))

    def gt_body(j, acc):
        return acc + jnp.where(keys_ref[j] > taub, 1.0, 0.0)

    n_gt = jnp.sum(lax.fori_loop(0, nt, gt_body, jnp.zeros((qb, kt), F32)), axis=-1, keepdims=True)
    need = n_sel - n_gt

    qn = q_ref[...]
    lane = lax.broadcasted_iota(I32, (qb, LANES), 1)

    def place(h):
        g = h // rep
        blk = qn[:, (h // 2) * LANES:(h // 2 + 1) * LANES]
        if h % 2 != g:
            blk = pltpu.roll(blk, B_HD, axis=1)
        keep = (lane >= g * B_HD) & (lane < (g + 1) * B_HD)
        return jnp.where(keep, blk, 0.0)

    qs = [jnp.concatenate([place(g * rep + r) for r in range(rep)], axis=0) for g in range(B_KV_HEADS)]
    m_ref[...] = jnp.full(m_ref.shape, NEG_BIG, F32)
    l_ref[...] = jnp.zeros(l_ref.shape, F32)
    acc_ref[...] = jnp.zeros(acc_ref.shape, F32)
    tri = jnp.where(row <= col, 1.0, 0.0).astype(BF16)

    def att_body(j, eq_seen):
        keyt = keys_ref[j]
        eq = keyt == taub
        eqf = jnp.where(eq, 1.0, 0.0)
        pref = _dot(eqf.astype(BF16), tri) + eq_seen
        take = jnp.where(keyt > taub, 1.0, jnp.where(pref <= need, eqf, 0.0))
        self = jnp.where(j * kt + col <= qpos, take, 0.0)
        sel = jnp.concatenate([self] * rep, axis=0) > 0.0
        start = pl.multiple_of(j * kt, kt)
        kj = k_ref[pl.ds(start, kt), :]
        vj = v_ref[pl.ds(start, kt), :]
        for g in range(B_KV_HEADS):
            s = _dot_nt(qs[g], kj, HI)
            m_old = m_ref[g]
            m_new = jnp.maximum(m_old, jnp.max(jnp.where(sel, s, NEG_BIG), axis=-1, keepdims=True))
            alpha = jnp.exp(m_old - m_new)
            p = jnp.where(sel, jnp.exp(s - m_new), 0.0)
            l_ref[g] = alpha * l_ref[g] + jnp.sum(p, axis=-1, keepdims=True)
            acc_ref[g] = alpha * acc_ref[g] + _dot(p, vj, HI)
            m_ref[g] = m_new
        return eq_seen + jnp.sum(eqf, axis=-1, keepdims=True)

    lax.fori_loop(0, nt, att_body, jnp.zeros((qb, 1), F32))

    outs = [acc_ref[g] / l_ref[g] for g in range(B_KV_HEADS)]
    for cblk in range(B_WIDTH // LANES):
        g = (2 * cblk) // rep
        r0 = (2 * cblk) % rep
        a = outs[g][r0 * qb:(r0 + 1) * qb]
        b = outs[g][(r0 + 1) * qb:(r0 + 2) * qb]
        if g == 0:
            b = pltpu.roll(b, B_HD, axis=1)
        else:
            a = pltpu.roll(a, B_HD, axis=1)
        ob = jnp.where(lane < B_HD, a, b)
        z = bz_ref[:, cblk * LANES:(cblk + 1) * LANES]
        o_ref[:, cblk * LANES:(cblk + 1) * LANES] = ob * _silu(z)


def _dsa(bq, bk, bv, iq, ik, small, bz, batch, seq, qb):
    nq = seq // qb
    n_sel = min(TOPK_MAX, seq // 4)
    rep = B_HEADS // B_KV_HEADS
    row = lambda w: pl.BlockSpec((qb, w), lambda b, i: (b * nq + i, 0))
    per_batch = lambda w: pl.BlockSpec((seq, w), lambda b, i: (b, 0))
    return pl.pallas_call(
        functools.partial(_dsa_kernel, qb=qb, n_sel=n_sel),
        grid=(batch, nq),
        in_specs=[row(B_WIDTH), per_batch(B_KV_WIDTH), per_batch(B_KV_WIDTH), row(IDX_WIDTH),
                  per_batch(IDX_DIM), row(SMALL_W), row(B_WIDTH)],
        out_specs=row(B_WIDTH),
        out_shape=jax.ShapeDtypeStruct((batch * seq, B_WIDTH), F32),
        scratch_shapes=[pltpu.VMEM((nq, qb, qb), I32),
                        pltpu.VMEM((B_KV_HEADS, rep * qb, 1), F32),
                        pltpu.VMEM((B_KV_HEADS, rep * qb, 1), F32),
                        pltpu.VMEM((B_KV_HEADS, rep * qb, LANES), F32)],
        compiler_params=pltpu.CompilerParams(dimension_semantics=("arbitrary", "arbitrary"),
                                             vmem_limit_bytes=VMEM_LIMIT),
        name="dsa",
    )(bq, bk, bv, iq, ik, small, bz)


def _out_kernel(x_ref, oa_ref, ob_ref, p_ref, woa_ref, wob_ref, wp_ref, wg_ref, gn_ref, bg_ref, o_ref):
    x1 = (x_ref[...] + _dot(oa_ref[...].astype(BF16), woa_ref[...])
          + _dot(ob_ref[...].astype(BF16), wob_ref[...]))
    hn = x1 * lax.rsqrt(jnp.mean(x1 * x1, axis=-1, keepdims=True) + EPS) * gn_ref[...]
    gate = jax.nn.sigmoid(_dot(hn.astype(BF16), wg_ref[...]) + bg_ref[...])
    o_ref[...] = x1 + _dot(p_ref[...].astype(BF16), wp_ref[...]) * gate


def _output(x2, oa, ob, p2, w_out, w_ple, gate_norm_w, w_gate, b_gate, tm):
    t, d = x2.shape
    woa = w_out[:A_WIDTH].astype(BF16)
    wob = w_out[A_WIDTH:].astype(BF16)
    wp = w_ple.astype(BF16)
    wg = w_gate.astype(BF16)
    gn = gate_norm_w[None, :]
    bg = b_gate[None, :]
    row = lambda w: pl.BlockSpec((tm, w), lambda i: (i, 0))
    full = lambda a: pl.BlockSpec(a.shape, lambda i: (0, 0))
    return pl.pallas_call(
        _out_kernel,
        grid=(t // tm,),
        in_specs=[row(d), row(A_WIDTH), row(B_WIDTH), row(PLE_DIM)] + [full(a) for a in (woa, wob, wp, wg, gn, bg)],
        out_specs=row(d),
        out_shape=jax.ShapeDtypeStruct((t, d), F32),
        compiler_params=pltpu.CompilerParams(dimension_semantics=("arbitrary",), vmem_limit_bytes=VMEM_LIMIT),
        name="out",
    )(x2, oa, ob, p2, woa, wob, wp, wg, gn, bg)


def kernel(x, p, attn_norm_w, w_in, conv_w, a_log, dt_bias, a_out_norm_w, b_q_norm_w, b_k_norm_w, w_out,
           w_ple, ple_gate_norm_w, w_ple_gate, b_ple_gate):
    batch, seq, d = x.shape
    t = batch * seq
    tm = min(512, t)
    bt = 2 * A_CHUNK
    qb = 128
    x2 = x.reshape(t, d)
    for i in range(w_in.shape[0]):
        qkv, az, bq, bk, bv, bz, iq, ik, small = _project(x2, attn_norm_w[i], w_in[i], b_q_norm_w[i],
                                                          b_k_norm_w[i], tm)
        small_t = jnp.swapaxes(small.reshape(batch, seq, SMALL_W)[:, :, :8], 1, 2)
        oa = _gdn(qkv, small, small_t, az, conv_w[i], a_log[i], dt_bias[i], a_out_norm_w[i], batch, seq, bt)
        ob = _dsa(bq, bk, bv, iq, ik, small, bz, batch, seq, qb)
        x2 = _output(x2, oa, ob, p[i].reshape(t, PLE_DIM), w_out[i], w_ple[i], ple_gate_norm_w[i],
                     w_ple_gate[i], b_ple_gate[i], tm)
    return x2.reshape(batch, seq, d)
```

```python
import functools

import jax
import jax.numpy as jnp
from jax import lax
from jax.experimental import pallas as pl
from jax.experimental.pallas import tpu as pltpu

F32 = jnp.float32
BF16 = jnp.bfloat16
I32 = jnp.int32
EPS = 1e-6
HI = lax.Precision.HIGHEST

PLE_DIM = 256
A_HEADS = 4
A_DK = 128
A_DV = 128
A_CONV = 4
A_CHUNK = 64
A_WIDTH = A_HEADS * A_DV
A_QKV = 2 * A_HEADS * A_DK + A_WIDTH
B_HEADS = 8
B_KV_HEADS = 2
B_HD = 64
B_WIDTH = B_HEADS * B_HD
B_KV_WIDTH = B_KV_HEADS * B_HD
IDX_HEADS = 8
IDX_DIM = 128
IDX_WIDTH = IDX_HEADS * IDX_DIM
TOPK_MAX = 256
LANES = 128
SMALL_W = LANES
SM_BETA = 0
SM_DECAY = A_HEADS
SM_IW = 2 * A_HEADS

VMEM_LIMIT = 56 * 1024 * 1024
NEG_BIG = -1e30
INT_MIN = -(2 ** 31)
KEY_MIN_FINITE = INT_MIN + 0x00800000


def _dot(a, b, prec=None):
    return jnp.dot(a, b, preferred_element_type=F32, precision=prec)


def _dot_nt(a, b, prec=None):
    return lax.dot_general(a, b, (((1,), (1,)), ((), ())), preferred_element_type=F32, precision=prec)


def _dot_tn(a, b, prec=None):
    return lax.dot_general(a, b, (((0,), (0,)), ((), ())), preferred_element_type=F32, precision=prec)


def _silu(x):
    return x * jax.nn.sigmoid(x)


def _softplus(x):
    return jnp.maximum(x, 0.0) + jnp.log1p(jnp.exp(-jnp.abs(x)))


def _seg_norm64(xb, gain_row):
    lane = lax.broadcasted_iota(I32, xb.shape, 1)
    lo = lane < B_HD
    sq = xb * xb
    s_lo = jnp.sum(jnp.where(lo, sq, 0.0), axis=-1, keepdims=True)
    s_hi = jnp.sum(jnp.where(lo, 0.0, sq), axis=-1, keepdims=True)
    ms = jnp.where(lo, s_lo, s_hi) * (1.0 / B_HD)
    return xb * lax.rsqrt(ms + EPS) * gain_row


_WB_QKV = 0
_WB_AZ = _WB_QKV + A_QKV
_WB_BQ = _WB_AZ + A_WIDTH
_WB_BK = _WB_BQ + B_WIDTH
_WB_BV = _WB_BK + B_KV_WIDTH
_WB_BZ = _WB_BV + B_KV_WIDTH
_WB_IQ = _WB_BZ + B_WIDTH
_WB_IK = _WB_IQ + IDX_WIDTH
_WB_END = _WB_IK + IDX_DIM


def _proj_kernel(x_ref, nw_ref, wb_ref, wf_ref, qg_ref, kg_ref,
                 qkv_ref, az_ref, bqx_ref, kx_ref, vx_ref, bz_ref, iq_ref, ik_ref, sm_ref):
    x = x_ref[...]
    h = x * lax.rsqrt(jnp.mean(x * x, axis=-1, keepdims=True) + EPS) * nw_ref[...]
    hb = h.astype(BF16)
    step = 512
    for c0 in range(0, A_QKV, step):
        qkv_ref[:, c0:c0 + step] = _dot(hb, wb_ref[:, _WB_QKV + c0:_WB_QKV + c0 + step])
    az_ref[...] = _dot(hb, wb_ref[:, _WB_AZ:_WB_BQ])
    bz_ref[...] = _dot(hb, wb_ref[:, _WB_BZ:_WB_IQ])
    lane = lax.broadcasted_iota(I32, (x.shape[0], LANES), 1)
    lo = lane < B_HD
    bk = _seg_norm64(_dot(hb, wb_ref[:, _WB_BK:_WB_BV]), kg_ref[...])
    bv = _dot(hb, wb_ref[:, _WB_BV:_WB_BZ])
    k_tail = jnp.where(lane == B_HD, 1.0, 0.0)
    for g in range(B_KV_HEADS):
        kg = bk if g == 0 else pltpu.roll(bk, B_HD, axis=1)
        vg = bv if g == 0 else pltpu.roll(bv, B_HD, axis=1)
        kx_ref[g] = jnp.where(lo, kg, k_tail).astype(BF16)
        vx_ref[g] = jnp.where(lo, vg, 1.0).astype(BF16)
    bq = _dot(hb, wb_ref[:, _WB_BQ:_WB_BK])
    scale = B_HD ** -0.5
    for c0 in range(0, B_WIDTH, LANES):
        nb = _seg_norm64(bq[:, c0:c0 + LANES], qg_ref[...]) * scale
        bqx_ref[:, 2 * c0:2 * c0 + LANES] = jnp.where(lo, nb, 0.0).astype(BF16)
        bqx_ref[:, 2 * c0 + LANES:2 * c0 + 2 * LANES] = jnp.where(
            lo, pltpu.roll(nb, B_HD, axis=1), 0.0).astype(BF16)
    for c0 in range(0, IDX_WIDTH, step):
        iq_ref[:, c0:c0 + step] = _dot(hb, wb_ref[:, _WB_IQ + c0:_WB_IQ + c0 + step]).astype(BF16)
    ik_ref[...] = _dot(hb, wb_ref[:, _WB_IK:_WB_END]).astype(BF16)
    sm_ref[...] = _dot(h, wf_ref[...], HI)


def _project(x2, norm_w, w_in, q_gain, k_gain, tm):
    t, d = x2.shape
    sizes = (A_HEADS * A_DK, A_HEADS * A_DK, A_WIDTH, A_WIDTH, A_HEADS, A_HEADS, B_WIDTH, B_KV_WIDTH,
             B_KV_WIDTH, B_WIDTH, IDX_WIDTH, IDX_DIM, IDX_HEADS)
    offs = [0]
    for s in sizes:
        offs.append(offs[-1] + s)
    (a_q, a_k, a_v, a_z, a_b, a_a, b_q, b_k, b_v, b_z, i_q, i_k, i_w) = [
        w_in[:, offs[n]:offs[n + 1]] for n in range(len(sizes))]
    wb = jnp.concatenate([a_q, a_k, a_v, a_z, b_q, b_k, b_v, b_z, i_q, i_k], axis=1).astype(BF16)
    pad = jnp.zeros((d, SMALL_W - 2 * A_HEADS - IDX_HEADS), F32)
    wf = jnp.concatenate([a_b, a_a, i_w, pad], axis=1)
    qg = jnp.tile(q_gain, LANES // B_HD)[None, :]
    kg = jnp.tile(k_gain, LANES // B_HD)[None, :]
    row = lambda w: pl.BlockSpec((tm, w), lambda i: (i, 0))
    grp = pl.BlockSpec((B_KV_HEADS, tm, LANES), lambda i: (0, i, 0))
    full = lambda a: pl.BlockSpec(a.shape, lambda i: (0, 0))
    nw = norm_w[None, :]
    sds = jax.ShapeDtypeStruct
    return pl.pallas_call(
        _proj_kernel,
        grid=(t // tm,),
        in_specs=[row(d), full(nw), full(wb), full(wf), full(qg), full(kg)],
        out_specs=[row(A_QKV), row(A_WIDTH), row(B_HEADS * LANES), grp, grp, row(B_WIDTH), row(IDX_WIDTH),
                   row(IDX_DIM), row(SMALL_W)],
        out_shape=[sds((t, A_QKV), F32), sds((t, A_WIDTH), F32), sds((t, B_HEADS * LANES), BF16),
                   sds((B_KV_HEADS, t, LANES), BF16), sds((B_KV_HEADS, t, LANES), BF16), sds((t, B_WIDTH), F32),
                   sds((t, IDX_WIDTH), BF16), sds((t, IDX_DIM), BF16), sds((t, SMALL_W), F32)],
        compiler_params=pltpu.CompilerParams(dimension_semantics=("arbitrary",), vmem_limit_bytes=VMEM_LIMIT),
        name="proj",
    )(x2, nw, wb, wf, qg, kg)


def _gdn_kernel(qkv_ref, sm_ref, smt_ref, az_ref, cw_ref, alog_r_ref, dtb_r_ref, alog_c_ref, dtb_c_ref,
                onw_ref, o_ref, s_ref, tail_ref, *, bt):
    c = A_CHUNK

    @pl.when(pl.program_id(1) == 0)
    def _():
        s_ref[...] = jnp.zeros(s_ref.shape, F32)
        tail_ref[...] = jnp.zeros(tail_ref.shape, F32)

    xin = qkv_ref[...]
    tail = tail_ref[...]
    cw = cw_ref[...]
    row8 = lax.broadcasted_iota(I32, tail.shape, 0)
    acc = xin * cw[A_CONV - 1:A_CONV, :]
    for d in range(1, A_CONV):
        xr = pltpu.roll(xin, d, axis=0)
        pr = pltpu.roll(tail, d, axis=0)
        head = jnp.where(row8 < d, pr, xr[0:8])
        xs = jnp.concatenate([head, xr[8:]], axis=0)
        acc = acc + xs * cw[A_CONV - 1 - d:A_CONV - d, :]
    tail_ref[...] = xin[bt - 8:bt]
    qkv = _silu(acc)

    sm = sm_ref[...]
    smt = smt_ref[0]
    g_col = -jnp.exp(alog_r_ref[...]) * _softplus(sm + dtb_r_ref[...])
    g_row = -jnp.exp(alog_c_ref[...]) * _softplus(smt + dtb_c_ref[...])
    beta_all = jax.nn.sigmoid(sm)

    rb = lax.broadcasted_iota(I32, (bt, bt), 0)
    cb = lax.broadcasted_iota(I32, (bt, bt), 1)
    same = (rb // c) == (cb // c)
    ltri = jnp.where(same & (cb <= rb), 1.0, 0.0).astype(F32)
    utri = jnp.where(same & (rb <= cb), 1.0, 0.0).astype(F32)
    gc_col = _dot(ltri, g_col, HI)
    gc_row = _dot(g_row, utri, HI)
    egc_all = jnp.exp(gc_col)

    r = lax.broadcasted_iota(I32, (c, c), 0)
    s = lax.broadcasted_iota(I32, (c, c), 1)
    tril = s <= r
    strict = s < r
    eye = jnp.where(s == r, 1.0, 0.0).astype(F32)
    ones_dk = jnp.ones((A_DK, c), F32)
    onw = onw_ref[...]

    for ci in range(bt // c):
        r0 = ci * c
        gtot = _dot(ones_dk, g_col[r0:r0 + c], HI)
        for h in range(A_HEADS):
            q = qkv[r0:r0 + c, h * A_DK:(h + 1) * A_DK]
            k = qkv[r0:r0 + c, A_HEADS * A_DK + h * A_DK:A_HEADS * A_DK + (h + 1) * A_DK]
            v = qkv[r0:r0 + c, 2 * A_HEADS * A_DK + h * A_DV:2 * A_HEADS * A_DK + (h + 1) * A_DV]
            q = q * lax.rsqrt(jnp.sum(q * q, axis=-1, keepdims=True) + EPS) * (A_DK ** -0.5)
            k = k * lax.rsqrt(jnp.sum(k * k, axis=-1, keepdims=True) + EPS)
            beta = beta_all[r0:r0 + c, SM_BETA + h:SM_BETA + h + 1]
            gcc = gc_col[r0:r0 + c, SM_DECAY + h:SM_DECAY + h + 1]
            gcr = gc_row[SM_DECAY + h:SM_DECAY + h + 1, r0:r0 + c]
            egc = egc_all[r0:r0 + c, SM_DECAY + h:SM_DECAY + h + 1]
            glast = gtot[:, SM_DECAY + h:SM_DECAY + h + 1]
            decay = jnp.where(tril, jnp.exp(jnp.where(tril, gcc - gcr, 0.0)), 0.0)
            kb = k * beta
            vb = v * beta
            kbf = k.astype(BF16)
            a_mat = jnp.where(strict, _dot_nt(kb.astype(BF16), kbf) * decay, 0.0)
            t_mat = eye - a_mat
            a_pow = a_mat
            n = 2
            while n < c:
                a_bf = a_pow.astype(BF16)
                a_pow = _dot(a_bf, a_bf)
                t_mat = t_mat + _dot(t_mat.astype(BF16), a_pow.astype(BF16))
                n *= 2
            t_bf = t_mat.astype(BF16)
            value = _dot(t_bf, vb.astype(BF16))
            kcd = _dot(t_bf, (kb * egc).astype(BF16))
            att = _dot_nt(q.astype(BF16), kbf) * decay
            qd = q * egc
            kd = k * jnp.exp(glast[0:c] - gcc)
            state = s_ref[h]
            s_bf = state.astype(BF16)
            v_new = value - _dot(kcd.astype(BF16), s_bf)
            v_bf = v_new.astype(BF16)
            o = _dot(qd.astype(BF16), s_bf) + _dot(att.astype(BF16), v_bf)
            s_ref[h] = state * jnp.exp(glast) + _dot_tn(kd.astype(BF16), v_bf)
            on = o * lax.rsqrt(jnp.mean(o * o, axis=-1, keepdims=True) + EPS) * onw
            z = az_ref[r0:r0 + c, h * A_DV:(h + 1) * A_DV]
            o_ref[r0:r0 + c, h * A_DV:(h + 1) * A_DV] = on * _silu(z)


def _gdn(qkv, small, small_t, a_z, conv_w, a_log, dt_bias, out_norm_w, batch, seq, bt):
    nt = seq // bt
    lane_pad = lambda v: jnp.zeros((1, LANES), F32).at[0, SM_DECAY:SM_DECAY + A_HEADS].set(v)
    sub_pad = lambda v: jnp.zeros((8, 1), F32).at[SM_DECAY:SM_DECAY + A_HEADS, 0].set(v)
    cw = jnp.zeros((8, A_QKV), F32).at[:A_CONV].set(conv_w)
    onw = out_norm_w[None, :]
    row = lambda w: pl.BlockSpec((bt, w), lambda b, t: (b * nt + t, 0))
    full = lambda a: pl.BlockSpec(a.shape, lambda b, t: (0, 0))
    consts = (cw, lane_pad(a_log), lane_pad(dt_bias), sub_pad(a_log), sub_pad(dt_bias), onw)
    return pl.pallas_call(
        functools.partial(_gdn_kernel, bt=bt),
        grid=(batch, nt),
        in_specs=[row(A_QKV), row(SMALL_W), pl.BlockSpec((1, 8, bt), lambda b, t: (b, 0, t)), row(A_WIDTH)]
        + [full(a) for a in consts],
        out_specs=row(A_WIDTH),
        out_shape=jax.ShapeDtypeStruct((batch * seq, A_WIDTH), F32),
        scratch_shapes=[pltpu.VMEM((A_HEADS, A_DK, A_DV), F32), pltpu.VMEM((8, A_QKV), F32)],
        compiler_params=pltpu.CompilerParams(dimension_semantics=("arbitrary", "arbitrary"),
                                             vmem_limit_bytes=VMEM_LIMIT),
        name="gdn",
    )(qkv, small, small_t, a_z, *consts)


def _dsa_kernel(q_ref, kx_ref, vx_ref, iq_ref, ik_ref, sm_ref, bz_ref, tri_ref, o_ref,
                sc_ref, bias_ref, mx_ref, *, qb, ks, n_sel):
    rep = B_HEADS // B_KV_HEADS
    i = pl.program_id(1)
    ns = (i * qb) // ks + 1
    row = lax.broadcasted_iota(I32, (qb, ks), 0)
    col = lax.broadcasted_iota(I32, (qb, ks), 1)
    qpos = i * qb + row

    iq = iq_ref[...]
    iqs = jnp.concatenate([iq[:, h * IDX_DIM:(h + 1) * IDX_DIM] for h in range(IDX_HEADS)], axis=0)
    iw = sm_ref[...] * (IDX_HEADS ** -0.5 * IDX_DIM ** -0.5)

    def score_body(j, carry):
        start = pl.multiple_of(j * ks, ks)
        lg = _dot_nt(iqs, ik_ref[pl.ds(start, ks), :])
        sc = jnp.zeros((qb, ks), F32)
        for h in range(IDX_HEADS):
            sc = sc + iw[:, SM_IW + h:SM_IW + h + 1] * jnp.maximum(lg[h * qb:(h + 1) * qb], 0.0)
        sc_ref[j] = jnp.where(j * ks + col <= qpos, sc, -jnp.inf)
        return carry

    lax.fori_loop(0, ns, score_body, 0)

    def count(pred):
        def body(j, acc):
            hit = jnp.where(pred(sc_ref[j]), 1.0, 0.0)
            for u in range(ks // LANES):
                acc = acc + hit[:, u * LANES:(u + 1) * LANES]
            return acc

        acc = lax.fori_loop(0, ns, body, jnp.zeros((qb, LANES), F32))
        return jnp.sum(acc, axis=-1, keepdims=True)

    def key_value(key):
        val = pltpu.bitcast(key ^ ((key >> 31) & 0x7FFFFFFF), F32)
        return jnp.where(key < KEY_MIN_FINITE, -jnp.inf, val)

    def bit_body(bi, tau):
        cand = tau ^ lax.shift_left(jnp.int32(1), 31 - bi)
        candb = jnp.broadcast_to(key_value(cand), (qb, ks))
        return jnp.where(count(lambda sc: sc >= candb) >= n_sel, cand, tau)

    tau = lax.fori_loop(0, 32, bit_body, jnp.full((qb, 1), INT_MIN, I32))
    taub = jnp.broadcast_to(key_value(tau), (qb, ks))
    need = n_sel - count(lambda sc: sc > taub)

    q = q_ref[...]
    qs = [jnp.concatenate([q[:, (g * rep + r) * LANES:(g * rep + r + 1) * LANES] for r in range(rep)], axis=0)
          for g in range(B_KV_HEADS)]
    mx_ref[...] = jnp.full(mx_ref.shape, NEG_BIG, F32)
    tri = tri_ref[...]

    def max_body(j, eq_seen):
        start = pl.multiple_of(j * ks, ks)
        keyt = sc_ref[j]
        eqf = jnp.where(keyt == taub, 1.0, 0.0)
        pref = _dot(eqf.astype(BF16), tri) + eq_seen
        take = jnp.where(keyt > taub, 1.0, jnp.where(pref <= need, eqf, 0.0))
        bias = jnp.where((take > 0.0) & (j * ks + col <= qpos), 0.0, NEG_BIG)
        bias_ref[j] = bias
        bias4 = jnp.concatenate([bias] * rep, axis=0)
        for g in range(B_KV_HEADS):
            s = _dot_nt(qs[g], kx_ref[g, pl.ds(start, ks), :]) + bias4
            m = mx_ref[g]
            for u in range(ks // LANES):
                m = jnp.maximum(m, s[:, u * LANES:(u + 1) * LANES])
            mx_ref[g] = m
        return eq_seen + jnp.sum(eqf, axis=-1, keepdims=True)

    lax.fori_loop(0, ns, max_body, jnp.zeros((qb, 1), F32))

    lane4 = lax.broadcasted_iota(I32, (rep * qb, LANES), 1)
    qe = []
    for g in range(B_KV_HEADS):
        m = jnp.max(mx_ref[g], axis=-1, keepdims=True)
        qe.append(jnp.where(lane4 == B_HD, -m, qs[g].astype(F32)).astype(BF16))

    def pv_body(j, accs):
        start = pl.multiple_of(j * ks, ks)
        bias4 = jnp.concatenate([bias_ref[j]] * rep, axis=0)
        out = []
        for g in range(B_KV_HEADS):
            s = _dot_nt(qe[g], kx_ref[g, pl.ds(start, ks), :]) + bias4
            p = jnp.exp(s).astype(BF16)
            out.append(accs[g] + _dot(p, vx_ref[g, pl.ds(start, ks), :]))
        return tuple(out)

    accs = lax.fori_loop(0, ns, pv_body, tuple(jnp.zeros((rep * qb, LANES), F32) for _ in range(B_KV_HEADS)))

    lane = lax.broadcasted_iota(I32, (qb, LANES), 1)
    outs = [acc / acc[:, B_HD:B_HD + 1] for acc in accs]
    for cblk in range(B_WIDTH // LANES):
        g = (2 * cblk) // rep
        r0 = (2 * cblk) % rep
        a = outs[g][r0 * qb:(r0 + 1) * qb]
        b = pltpu.roll(outs[g][(r0 + 1) * qb:(r0 + 2) * qb], B_HD, axis=1)
        z = bz_ref[:, cblk * LANES:(cblk + 1) * LANES]
        o_ref[:, cblk * LANES:(cblk + 1) * LANES] = jnp.where(lane < B_HD, a, b) * _silu(z)


def _dsa(bqx, kx, vx, iq, ik, small, bz, batch, seq, qb, ks):
    nq = seq // qb
    n_sel = min(TOPK_MAX, seq // 4)
    rep = B_HEADS // B_KV_HEADS
    tri = (jnp.arange(ks)[:, None] <= jnp.arange(ks)[None, :]).astype(BF16)
    row = lambda w: pl.BlockSpec((qb, w), lambda b, i: (b * nq + i, 0))
    per_batch = lambda w: pl.BlockSpec((seq, w), lambda b, i: (b, 0))
    per_batch_grp = pl.BlockSpec((B_KV_HEADS, seq, LANES), lambda b, i: (0, b, 0))
    return pl.pallas_call(
        functools.partial(_dsa_kernel, qb=qb, ks=ks, n_sel=n_sel),
        grid=(batch, nq),
        in_specs=[row(B_HEADS * LANES), per_batch_grp, per_batch_grp, row(IDX_WIDTH), per_batch(IDX_DIM),
                  row(SMALL_W), row(B_WIDTH), pl.BlockSpec((ks, ks), lambda b, i: (0, 0))],
        out_specs=row(B_WIDTH),
        out_shape=jax.ShapeDtypeStruct((batch * seq, B_WIDTH), F32),
        scratch_shapes=[pltpu.VMEM((seq // ks, qb, ks), F32),
                        pltpu.VMEM((seq // ks, qb, ks), F32),
                        pltpu.VMEM((B_KV_HEADS, rep * qb, LANES), F32)],
        compiler_params=pltpu.CompilerParams(dimension_semantics=("arbitrary", "arbitrary"),
                                             vmem_limit_bytes=VMEM_LIMIT),
        name="dsa",
    )(bqx, kx, vx, iq, ik, small, bz, tri)


def _out_kernel(x_ref, oa_ref, ob_ref, p_ref, woa_ref, wob_ref, wp_ref, wg_ref, gn_ref, bg_ref, o_ref):
    x1 = (x_ref[...] + _dot(oa_ref[...].astype(BF16), woa_ref[...])
          + _dot(ob_ref[...].astype(BF16), wob_ref[...]))
    hn = x1 * lax.rsqrt(jnp.mean(x1 * x1, axis=-1, keepdims=True) + EPS) * gn_ref[...]
    gate = jax.nn.sigmoid(_dot(hn.astype(BF16), wg_ref[...]) + bg_ref[...])
    o_ref[...] = x1 + _dot(p_ref[...].astype(BF16), wp_ref[...]) * gate


def _output(x2, oa, ob, p2, w_out, w_ple, gate_norm_w, w_gate, b_gate, tm):
    t, d = x2.shape
    woa = w_out[:A_WIDTH].astype(BF16)
    wob = w_out[A_WIDTH:].astype(BF16)
    wp = w_ple.astype(BF16)
    wg = w_gate.astype(BF16)
    gn = gate_norm_w[None, :]
    bg = b_gate[None, :]
    row = lambda w: pl.BlockSpec((tm, w), lambda i: (i, 0))
    full = lambda a: pl.BlockSpec(a.shape, lambda i: (0, 0))
    return pl.pallas_call(
        _out_kernel,
        grid=(t // tm,),
        in_specs=[row(d), row(A_WIDTH), row(B_WIDTH), row(PLE_DIM)] + [full(a) for a in (woa, wob, wp, wg, gn, bg)],
        out_specs=row(d),
        out_shape=jax.ShapeDtypeStruct((t, d), F32),
        compiler_params=pltpu.CompilerParams(dimension_semantics=("arbitrary",), vmem_limit_bytes=VMEM_LIMIT),
        name="out",
    )(x2, oa, ob, p2, woa, wob, wp, wg, gn, bg)


def kernel(x, p, attn_norm_w, w_in, conv_w, a_log, dt_bias, a_out_norm_w, b_q_norm_w, b_k_norm_w, w_out,
           w_ple, ple_gate_norm_w, w_ple_gate, b_ple_gate):
    batch, seq, d = x.shape
    t = batch * seq
    tm = min(512, t)
    bt = 2 * A_CHUNK
    qb = 128
    ks = min(512, seq)
    x2 = x.reshape(t, d)
    for i in range(w_in.shape[0]):
        qkv, az, bqx, kx, vx, bz, iq, ik, small = _project(x2, attn_norm_w[i], w_in[i], b_q_norm_w[i],
                                                           b_k_norm_w[i], tm)
        small_t = jnp.swapaxes(small.reshape(batch, seq, SMALL_W)[:, :, :8], 1, 2)
        oa = _gdn(qkv, small, small_t, az, conv_w[i], a_log[i], dt_bias[i], a_out_norm_w[i], batch, seq, bt)
        ob = _dsa(bqx, kx, vx, iq, ik, small, bz, batch, seq, qb, ks)
        x2 = _output(x2, oa, ob, p[i].reshape(t, PLE_DIM), w_out[i], w_ple[i], ple_gate_norm_w[i],
                     w_ple_gate[i], b_ple_gate[i], tm)
    return x2.reshape(batch, seq, d)
```

```python
import functools

import jax
import jax.numpy as jnp
from jax import lax
from jax.experimental import pallas as pl
from jax.experimental.pallas import tpu as pltpu

F32 = jnp.float32
BF16 = jnp.bfloat16
I32 = jnp.int32
EPS = 1e-6
HI = lax.Precision.HIGHEST

PLE_DIM = 256
A_HEADS = 4
A_DK = 128
A_DV = 128
A_CONV = 4
A_CHUNK = 64
A_WIDTH = A_HEADS * A_DV
A_QKV = 2 * A_HEADS * A_DK + A_WIDTH
B_HEADS = 8
B_KV_HEADS = 2
B_HD = 64
B_WIDTH = B_HEADS * B_HD
B_KV_WIDTH = B_KV_HEADS * B_HD
IDX_HEADS = 8
IDX_DIM = 128
IDX_WIDTH = IDX_HEADS * IDX_DIM
TOPK_MAX = 256
LANES = 128
SMALL_W = LANES
SM_BETA = 0
SM_DECAY = A_HEADS
SM_IW = 2 * A_HEADS

VMEM_LIMIT = 56 * 1024 * 1024
NEG_BIG = -1e30
INT_MIN = -(2 ** 31)
KEY_MIN_FINITE = INT_MIN + 0x00800000


def _dot(a, b, prec=None):
    return jnp.dot(a, b, preferred_element_type=F32, precision=prec)


def _dot_nt(a, b, prec=None):
    return lax.dot_general(a, b, (((1,), (1,)), ((), ())), preferred_element_type=F32, precision=prec)


def _dot_tn(a, b, prec=None):
    return lax.dot_general(a, b, (((0,), (0,)), ((), ())), preferred_element_type=F32, precision=prec)


def _silu(x):
    return x * jax.nn.sigmoid(x)


def _softplus(x):
    return jnp.maximum(x, 0.0) + jnp.log1p(jnp.exp(-jnp.abs(x)))


def _seg_norm64(xb, gain_row):
    lane = lax.broadcasted_iota(I32, xb.shape, 1)
    lo = lane < B_HD
    sq = xb * xb
    s_lo = jnp.sum(jnp.where(lo, sq, 0.0), axis=-1, keepdims=True)
    s_hi = jnp.sum(jnp.where(lo, 0.0, sq), axis=-1, keepdims=True)
    ms = jnp.where(lo, s_lo, s_hi) * (1.0 / B_HD)
    return xb * lax.rsqrt(ms + EPS) * gain_row


_WB_QKV = 0
_WB_AZ = _WB_QKV + A_QKV
_WB_BQ = _WB_AZ + A_WIDTH
_WB_BK = _WB_BQ + B_WIDTH
_WB_BV = _WB_BK + B_KV_WIDTH
_WB_BZ = _WB_BV + B_KV_WIDTH
_WB_IQ = _WB_BZ + B_WIDTH
_WB_IK = _WB_IQ + IDX_WIDTH
_WB_END = _WB_IK + IDX_DIM


def _proj_kernel(x_ref, nw_ref, wb_ref, wf_ref, qg_ref, kg_ref,
                 qkv_ref, az_ref, bqx_ref, kx_ref, vx_ref, bz_ref, iq_ref, ik_ref, sm_ref):
    x = x_ref[...]
    h = x * lax.rsqrt(jnp.mean(x * x, axis=-1, keepdims=True) + EPS) * nw_ref[...]
    hb = h.astype(BF16)
    step = 512
    for c0 in range(0, A_QKV, step):
        qkv_ref[:, c0:c0 + step] = _dot(hb, wb_ref[:, _WB_QKV + c0:_WB_QKV + c0 + step])
    az_ref[...] = _dot(hb, wb_ref[:, _WB_AZ:_WB_BQ])
    bz_ref[...] = _dot(hb, wb_ref[:, _WB_BZ:_WB_IQ])
    lane = lax.broadcasted_iota(I32, (x.shape[0], LANES), 1)
    lo = lane < B_HD
    bk = _seg_norm64(_dot(hb, wb_ref[:, _WB_BK:_WB_BV]), kg_ref[...])
    bv = _dot(hb, wb_ref[:, _WB_BV:_WB_BZ])
    k_tail = jnp.where(lane == B_HD, 1.0, 0.0)
    for g in range(B_KV_HEADS):
        kg = bk if g == 0 else pltpu.roll(bk, B_HD, axis=1)
        vg = bv if g == 0 else pltpu.roll(bv, B_HD, axis=1)
        kx_ref[g] = jnp.where(lo, kg, k_tail).astype(BF16)
        vx_ref[g] = jnp.where(lo, vg, 1.0).astype(BF16)
    bq = _dot(hb, wb_ref[:, _WB_BQ:_WB_BK])
    scale = B_HD ** -0.5
    for c0 in range(0, B_WIDTH, LANES):
        nb = _seg_norm64(bq[:, c0:c0 + LANES], qg_ref[...]) * scale
        bqx_ref[:, 2 * c0:2 * c0 + LANES] = jnp.where(lo, nb, 0.0).astype(BF16)
        bqx_ref[:, 2 * c0 + LANES:2 * c0 + 2 * LANES] = jnp.where(
            lo, pltpu.roll(nb, B_HD, axis=1), 0.0).astype(BF16)
    for c0 in range(0, IDX_WIDTH, step):
        iq_ref[:, c0:c0 + step] = _dot(hb, wb_ref[:, _WB_IQ + c0:_WB_IQ + c0 + step]).astype(BF16)
    ik_ref[...] = _dot(hb, wb_ref[:, _WB_IK:_WB_END]).astype(BF16)
    sm_ref[...] = _dot(h, wf_ref[...], HI)


def _project(x2, norm_w, w_in, q_gain, k_gain, tm):
    t, d = x2.shape
    sizes = (A_HEADS * A_DK, A_HEADS * A_DK, A_WIDTH, A_WIDTH, A_HEADS, A_HEADS, B_WIDTH, B_KV_WIDTH,
             B_KV_WIDTH, B_WIDTH, IDX_WIDTH, IDX_DIM, IDX_HEADS)
    offs = [0]
    for s in sizes:
        offs.append(offs[-1] + s)
    (a_q, a_k, a_v, a_z, a_b, a_a, b_q, b_k, b_v, b_z, i_q, i_k, i_w) = [
        w_in[:, offs[n]:offs[n + 1]] for n in range(len(sizes))]
    wb = jnp.concatenate([a_q, a_k, a_v, a_z, b_q, b_k, b_v, b_z, i_q, i_k], axis=1).astype(BF16)
    pad = jnp.zeros((d, SMALL_W - 2 * A_HEADS - IDX_HEADS), F32)
    wf = jnp.concatenate([a_b, a_a, i_w, pad], axis=1)
    qg = jnp.tile(q_gain, LANES // B_HD)[None, :]
    kg = jnp.tile(k_gain, LANES // B_HD)[None, :]
    row = lambda w: pl.BlockSpec((tm, w), lambda i: (i, 0))
    grp = pl.BlockSpec((B_KV_HEADS, tm, LANES), lambda i: (0, i, 0))
    full = lambda a: pl.BlockSpec(a.shape, lambda i: (0, 0))
    nw = norm_w[None, :]
    sds = jax.ShapeDtypeStruct
    return pl.pallas_call(
        _proj_kernel,
        grid=(t // tm,),
        in_specs=[row(d), full(nw), full(wb), full(wf), full(qg), full(kg)],
        out_specs=[row(A_QKV), row(A_WIDTH), row(B_HEADS * LANES), grp, grp, row(B_WIDTH), row(IDX_WIDTH),
                   row(IDX_DIM), row(SMALL_W)],
        out_shape=[sds((t, A_QKV), F32), sds((t, A_WIDTH), F32), sds((t, B_HEADS * LANES), BF16),
                   sds((B_KV_HEADS, t, LANES), BF16), sds((B_KV_HEADS, t, LANES), BF16), sds((t, B_WIDTH), F32),
                   sds((t, IDX_WIDTH), BF16), sds((t, IDX_DIM), BF16), sds((t, SMALL_W), F32)],
        compiler_params=pltpu.CompilerParams(dimension_semantics=("arbitrary",), vmem_limit_bytes=VMEM_LIMIT),
        name="proj",
    )(x2, nw, wb, wf, qg, kg)


def _gdn_kernel(qkv_ref, sm_ref, smt_ref, az_ref, cw_ref, alog_r_ref, dtb_r_ref, alog_c_ref, dtb_c_ref,
                onw_ref, o_ref, s_ref, tail_ref, *, bt):
    c = A_CHUNK

    @pl.when(pl.program_id(1) == 0)
    def _():
        s_ref[...] = jnp.zeros(s_ref.shape, F32)
        tail_ref[...] = jnp.zeros(tail_ref.shape, F32)

    xin = qkv_ref[...]
    tail = tail_ref[...]
    cw = cw_ref[...]
    row8 = lax.broadcasted_iota(I32, tail.shape, 0)
    acc = xin * cw[A_CONV - 1:A_CONV, :]
    for d in range(1, A_CONV):
        xr = pltpu.roll(xin, d, axis=0)
        pr = pltpu.roll(tail, d, axis=0)
        head = jnp.where(row8 < d, pr, xr[0:8])
        xs = jnp.concatenate([head, xr[8:]], axis=0)
        acc = acc + xs * cw[A_CONV - 1 - d:A_CONV - d, :]
    tail_ref[...] = xin[bt - 8:bt]
    qkv = _silu(acc)

    sm = sm_ref[...]
    smt = smt_ref[0]
    g_col = -jnp.exp(alog_r_ref[...]) * _softplus(sm + dtb_r_ref[...])
    g_row = -jnp.exp(alog_c_ref[...]) * _softplus(smt + dtb_c_ref[...])
    beta_all = jax.nn.sigmoid(sm)

    r = lax.broadcasted_iota(I32, (bt, bt), 0)
    s = lax.broadcasted_iota(I32, (bt, bt), 1)
    same = (r // c) == (s // c)
    tril = same & (s <= r)
    strict = same & (s < r)
    eye = jnp.where(s == r, 1.0, 0.0).astype(F32)
    gc_col = _dot(jnp.where(tril, 1.0, 0.0).astype(F32), g_col, HI)
    gc_row = _dot(g_row, jnp.where(same & (r <= s), 1.0, 0.0).astype(F32), HI)
    gtot = _dot(jnp.where(same, 1.0, 0.0).astype(F32), g_col, HI)
    egc_all = jnp.exp(gc_col)
    ekd_all = jnp.exp(gtot - gc_col)
    egl_all = jnp.exp(gtot)
    onw = onw_ref[...]

    for h in range(A_HEADS):
        q = qkv[:, h * A_DK:(h + 1) * A_DK]
        k = qkv[:, A_HEADS * A_DK + h * A_DK:A_HEADS * A_DK + (h + 1) * A_DK]
        v = qkv[:, 2 * A_HEADS * A_DK + h * A_DV:2 * A_HEADS * A_DK + (h + 1) * A_DV]
        q = q * lax.rsqrt(jnp.sum(q * q, axis=-1, keepdims=True) + EPS) * (A_DK ** -0.5)
        k = k * lax.rsqrt(jnp.sum(k * k, axis=-1, keepdims=True) + EPS)
        beta = beta_all[:, SM_BETA + h:SM_BETA + h + 1]
        lane = SM_DECAY + h
        gcc = gc_col[:, lane:lane + 1]
        gcr = gc_row[lane:lane + 1, :]
        egc = egc_all[:, lane:lane + 1]
        egl = egl_all[:, lane:lane + 1]
        decay = jnp.where(tril, jnp.exp(jnp.where(tril, gcc - gcr, 0.0)), 0.0)
        kb = k * beta
        kbf = k.astype(BF16)
        a_mat = jnp.where(strict, _dot_nt(kb.astype(BF16), kbf) * decay, 0.0)
        t_mat = eye - a_mat
        a_pow = a_mat
        n = 2
        while n < c:
            a_bf = a_pow.astype(BF16)
            a_pow = _dot(a_bf, a_bf)
            t_mat = t_mat + _dot(t_mat.astype(BF16), a_pow.astype(BF16))
            n *= 2
        t_bf = t_mat.astype(BF16)
        value = _dot(t_bf, (v * beta).astype(BF16))
        kcd = _dot(t_bf, (kb * egc).astype(BF16)).astype(BF16)
        att = (_dot_nt(q.astype(BF16), kbf) * decay).astype(BF16)
        qd = (q * egc).astype(BF16)
        kd = (k * ekd_all[:, lane:lane + 1]).astype(BF16)
        state = s_ref[h]
        o_inter = []
        v_new = []
        for ci in range(bt // c):
            rows = slice(ci * c, (ci + 1) * c)
            s_bf = state.astype(BF16)
            vn = (value[rows] - _dot(kcd[rows], s_bf)).astype(BF16)
            o_inter.append(_dot(qd[rows], s_bf))
            v_new.append(vn)
            gl = egl[ci * c:ci * c + 8]
            state = state * jnp.concatenate([gl] * (A_DK // 8), axis=0) + _dot_tn(kd[rows], vn)
        s_ref[h] = state
        o = jnp.concatenate(o_inter, axis=0) + _dot(att, jnp.concatenate(v_new, axis=0))
        on = o * lax.rsqrt(jnp.mean(o * o, axis=-1, keepdims=True) + EPS) * onw
        z = az_ref[:, h * A_DV:(h + 1) * A_DV]
        o_ref[:, h * A_DV:(h + 1) * A_DV] = on * _silu(z)


def _gdn(qkv, small, small_t, a_z, conv_w, a_log, dt_bias, out_norm_w, batch, seq, bt):
    nt = seq // bt
    lane_pad = lambda v: jnp.zeros((1, LANES), F32).at[0, SM_DECAY:SM_DECAY + A_HEADS].set(v)
    sub_pad = lambda v: jnp.zeros((8, 1), F32).at[SM_DECAY:SM_DECAY + A_HEADS, 0].set(v)
    cw = jnp.zeros((8, A_QKV), F32).at[:A_CONV].set(conv_w)
    onw = out_norm_w[None, :]
    row = lambda w: pl.BlockSpec((bt, w), lambda b, t: (b * nt + t, 0))
    full = lambda a: pl.BlockSpec(a.shape, lambda b, t: (0, 0))
    consts = (cw, lane_pad(a_log), lane_pad(dt_bias), sub_pad(a_log), sub_pad(dt_bias), onw)
    return pl.pallas_call(
        functools.partial(_gdn_kernel, bt=bt),
        grid=(batch, nt),
        in_specs=[row(A_QKV), row(SMALL_W), pl.BlockSpec((1, 8, bt), lambda b, t: (b, 0, t)), row(A_WIDTH)]
        + [full(a) for a in consts],
        out_specs=row(A_WIDTH),
        out_shape=jax.ShapeDtypeStruct((batch * seq, A_WIDTH), F32),
        scratch_shapes=[pltpu.VMEM((A_HEADS, A_DK, A_DV), F32), pltpu.VMEM((8, A_QKV), F32)],
        compiler_params=pltpu.CompilerParams(dimension_semantics=("arbitrary", "arbitrary"),
                                             vmem_limit_bytes=VMEM_LIMIT),
        name="gdn",
    )(qkv, small, small_t, a_z, *consts)


def _dsa_kernel(q_ref, kx_ref, vx_ref, iq_ref, ik_ref, sm_ref, bz_ref, tri_ref, o_ref,
                sc_ref, bias_ref, mx_ref, *, qb, ks, n_sel):
    rep = B_HEADS // B_KV_HEADS
    i = pl.program_id(1)
    ns = (i * qb) // ks + 1
    row = lax.broadcasted_iota(I32, (qb, ks), 0)
    col = lax.broadcasted_iota(I32, (qb, ks), 1)
    qpos = i * qb + row

    iq = iq_ref[...]
    iqs = jnp.concatenate([iq[:, h * IDX_DIM:(h + 1) * IDX_DIM] for h in range(IDX_HEADS)], axis=0)
    iw = sm_ref[...] * (IDX_HEADS ** -0.5 * IDX_DIM ** -0.5)

    def score_body(j, carry):
        start = pl.multiple_of(j * ks, ks)
        lg = _dot_nt(iqs, ik_ref[pl.ds(start, ks), :])
        sc = jnp.zeros((qb, ks), F32)
        for h in range(IDX_HEADS):
            sc = sc + iw[:, SM_IW + h:SM_IW + h + 1] * jnp.maximum(lg[h * qb:(h + 1) * qb], 0.0)
        sc_ref[j] = jnp.where(j * ks + col <= qpos, sc, -jnp.inf)
        return carry

    lax.fori_loop(0, ns, score_body, 0)

    def count(pred):
        def body(j, acc):
            hit = jnp.where(pred(sc_ref[j]), 1.0, 0.0)
            for u in range(ks // LANES):
                acc = acc + hit[:, u * LANES:(u + 1) * LANES]
            return acc

        acc = lax.fori_loop(0, ns, body, jnp.zeros((qb, LANES), F32))
        return jnp.sum(acc, axis=-1, keepdims=True)

    def key_value(key):
        val = pltpu.bitcast(key ^ ((key >> 31) & 0x7FFFFFFF), F32)
        return jnp.where(key < KEY_MIN_FINITE, -jnp.inf, val)

    def bit_body(bi, tau):
        cand = tau ^ lax.shift_left(jnp.int32(1), 31 - bi)
        candb = jnp.broadcast_to(key_value(cand), (qb, ks))
        return jnp.where(count(lambda sc: sc >= candb) >= n_sel, cand, tau)

    tau = lax.fori_loop(0, 32, bit_body, jnp.full((qb, 1), INT_MIN, I32))
    taub = jnp.broadcast_to(key_value(tau), (qb, ks))
    need = n_sel - count(lambda sc: sc > taub)

    q = q_ref[...]
    qs = [jnp.concatenate([q[:, (g * rep + r) * LANES:(g * rep + r + 1) * LANES] for r in range(rep)], axis=0)
          for g in range(B_KV_HEADS)]
    mx_ref[...] = jnp.full(mx_ref.shape, NEG_BIG, F32)
    tri = tri_ref[...]

    def max_body(j, eq_seen):
        start = pl.multiple_of(j * ks, ks)
        keyt = sc_ref[j]
        eqf = jnp.where(keyt == taub, 1.0, 0.0)
        pref = _dot(eqf.astype(BF16), tri) + eq_seen
        take = jnp.where(keyt > taub, 1.0, jnp.where(pref <= need, eqf, 0.0))
        bias = jnp.where((take > 0.0) & (j * ks + col <= qpos), 0.0, NEG_BIG)
        bias_ref[j] = bias
        bias4 = jnp.concatenate([bias] * rep, axis=0)
        for g in range(B_KV_HEADS):
            s = _dot_nt(qs[g], kx_ref[g, pl.ds(start, ks), :]) + bias4
            m = mx_ref[g]
            for u in range(ks // LANES):
                m = jnp.maximum(m, s[:, u * LANES:(u + 1) * LANES])
            mx_ref[g] = m
        return eq_seen + jnp.sum(eqf, axis=-1, keepdims=True)

    lax.fori_loop(0, ns, max_body, jnp.zeros((qb, 1), F32))

    lane4 = lax.broadcasted_iota(I32, (rep * qb, LANES), 1)
    qe = []
    for g in range(B_KV_HEADS):
        m = jnp.max(mx_ref[g], axis=-1, keepdims=True)
        qe.append(jnp.where(lane4 == B_HD, -m, qs[g].astype(F32)).astype(BF16))

    def pv_body(j, accs):
        start = pl.multiple_of(j * ks, ks)
        bias4 = jnp.concatenate([bias_ref[j]] * rep, axis=0)
        out = []
        for g in range(B_KV_HEADS):
            s = _dot_nt(qe[g], kx_ref[g, pl.ds(start, ks), :]) + bias4
            p = jnp.exp(s).astype(BF16)
            out.append(accs[g] + _dot(p, vx_ref[g, pl.ds(start, ks), :]))
        return tuple(out)

    accs = lax.fori_loop(0, ns, pv_body, tuple(jnp.zeros((rep * qb, LANES), F32) for _ in range(B_KV_HEADS)))

    lane = lax.broadcasted_iota(I32, (qb, LANES), 1)
    outs = [acc / acc[:, B_HD:B_HD + 1] for acc in accs]
    for cblk in range(B_WIDTH // LANES):
        g = (2 * cblk) // rep
        r0 = (2 * cblk) % rep
        a = outs[g][r0 * qb:(r0 + 1) * qb]
        b = pltpu.roll(outs[g][(r0 + 1) * qb:(r0 + 2) * qb], B_HD, axis=1)
        z = bz_ref[:, cblk * LANES:(cblk + 1) * LANES]
        o_ref[:, cblk * LANES:(cblk + 1) * LANES] = jnp.where(lane < B_HD, a, b) * _silu(z)


def _dsa(bqx, kx, vx, iq, ik, small, bz, batch, seq, qb, ks):
    nq = seq // qb
    n_sel = min(TOPK_MAX, seq // 4)
    rep = B_HEADS // B_KV_HEADS
    tri = (jnp.arange(ks)[:, None] <= jnp.arange(ks)[None, :]).astype(BF16)
    row = lambda w: pl.BlockSpec((qb, w), lambda b, i: (b * nq + i, 0))
    per_batch = lambda w: pl.BlockSpec((seq, w), lambda b, i: (b, 0))
    per_batch_grp = pl.BlockSpec((B_KV_HEADS, seq, LANES), lambda b, i: (0, b, 0))
    return pl.pallas_call(
        functools.partial(_dsa_kernel, qb=qb, ks=ks, n_sel=n_sel),
        grid=(batch, nq),
        in_specs=[row(B_HEADS * LANES), per_batch_grp, per_batch_grp, row(IDX_WIDTH), per_batch(IDX_DIM),
                  row(SMALL_W), row(B_WIDTH), pl.BlockSpec((ks, ks), lambda b, i: (0, 0))],
        out_specs=row(B_WIDTH),
        out_shape=jax.ShapeDtypeStruct((batch * seq, B_WIDTH), F32),
        scratch_shapes=[pltpu.VMEM((seq // ks, qb, ks), F32),
                        pltpu.VMEM((seq // ks, qb, ks), F32),
                        pltpu.VMEM((B_KV_HEADS, rep * qb, LANES), F32)],
        compiler_params=pltpu.CompilerParams(dimension_semantics=("arbitrary", "arbitrary"),
                                             vmem_limit_bytes=VMEM_LIMIT),
        name="dsa",
    )(bqx, kx, vx, iq, ik, small, bz, tri)


def _out_kernel(x_ref, oa_ref, ob_ref, p_ref, woa_ref, wob_ref, wp_ref, wg_ref, gn_ref, bg_ref, o_ref):
    x1 = (x_ref[...] + _dot(oa_ref[...].astype(BF16), woa_ref[...])
          + _dot(ob_ref[...].astype(BF16), wob_ref[...]))
    hn = x1 * lax.rsqrt(jnp.mean(x1 * x1, axis=-1, keepdims=True) + EPS) * gn_ref[...]
    gate = jax.nn.sigmoid(_dot(hn.astype(BF16), wg_ref[...]) + bg_ref[...])
    o_ref[...] = x1 + _dot(p_ref[...].astype(BF16), wp_ref[...]) * gate


def _output(x2, oa, ob, p2, w_out, w_ple, gate_norm_w, w_gate, b_gate, tm):
    t, d = x2.shape
    woa = w_out[:A_WIDTH].astype(BF16)
    wob = w_out[A_WIDTH:].astype(BF16)
    wp = w_ple.astype(BF16)
    wg = w_gate.astype(BF16)
    gn = gate_norm_w[None, :]
    bg = b_gate[None, :]
    row = lambda w: pl.BlockSpec((tm, w), lambda i: (i, 0))
    full = lambda a: pl.BlockSpec(a.shape, lambda i: (0, 0))
    return pl.pallas_call(
        _out_kernel,
        grid=(t // tm,),
        in_specs=[row(d), row(A_WIDTH), row(B_WIDTH), row(PLE_DIM)] + [full(a) for a in (woa, wob, wp, wg, gn, bg)],
        out_specs=row(d),
        out_shape=jax.ShapeDtypeStruct((t, d), F32),
        compiler_params=pltpu.CompilerParams(dimension_semantics=("arbitrary",), vmem_limit_bytes=VMEM_LIMIT),
        name="out",
    )(x2, oa, ob, p2, woa, wob, wp, wg, gn, bg)


def kernel(x, p, attn_norm_w, w_in, conv_w, a_log, dt_bias, a_out_norm_w, b_q_norm_w, b_k_norm_w, w_out,
           w_ple, ple_gate_norm_w, w_ple_gate, b_ple_gate):
    batch, seq, d = x.shape
    t = batch * seq
    tm = min(512, t)
    bt = min(256, seq)
    qb = 128
    ks = min(512, seq)
    x2 = x.reshape(t, d)
    for i in range(w_in.shape[0]):
        qkv, az, bqx, kx, vx, bz, iq, ik, small = _project(x2, attn_norm_w[i], w_in[i], b_q_norm_w[i],
                                                           b_k_norm_w[i], tm)
        small_t = jnp.swapaxes(small.reshape(batch, seq, SMALL_W)[:, :, :8], 1, 2)
        oa = _gdn(qkv, small, small_t, az, conv_w[i], a_log[i], dt_bias[i], a_out_norm_w[i], batch, seq, bt)
        ob = _dsa(bqx, kx, vx, iq, ik, small, bz, batch, seq, qb, ks)
        x2 = _output(x2, oa, ob, p[i].reshape(t, PLE_DIM), w_out[i], w_ple[i], ple_gate_norm_w[i],
                     w_ple_gate[i], b_ple_gate[i], tm)
    return x2.reshape(batch, seq, d)
```

```python
import functools

import jax
import jax.numpy as jnp
from jax import lax
from jax.experimental import pallas as pl
from jax.experimental.pallas import tpu as pltpu

F32 = jnp.float32
BF16 = jnp.bfloat16
I32 = jnp.int32
EPS = 1e-6
HI = lax.Precision.HIGHEST

PLE_DIM = 256
A_HEADS = 4
A_DK = 128
A_DV = 128
A_CONV = 4
A_CHUNK = 64
A_WIDTH = A_HEADS * A_DV
A_QKV = 2 * A_HEADS * A_DK + A_WIDTH
B_HEADS = 8
B_KV_HEADS = 2
B_HD = 64
B_WIDTH = B_HEADS * B_HD
B_KV_WIDTH = B_KV_HEADS * B_HD
IDX_HEADS = 8
IDX_DIM = 128
IDX_WIDTH = IDX_HEADS * IDX_DIM
TOPK_MAX = 256
LANES = 128
SMALL_W = LANES
SM_BETA = 0
SM_DECAY = A_HEADS
SM_IW = 2 * A_HEADS

VMEM_LIMIT = 56 * 1024 * 1024
NEG_BIG = -1e30
INT_MIN = -(2 ** 31)
KEY_MIN_FINITE = INT_MIN + 0x00800000


def _dot(a, b, prec=None):
    return jnp.dot(a, b, preferred_element_type=F32, precision=prec)


def _dot_nt(a, b, prec=None):
    return lax.dot_general(a, b, (((1,), (1,)), ((), ())), preferred_element_type=F32, precision=prec)


def _dot_tn(a, b, prec=None):
    return lax.dot_general(a, b, (((0,), (0,)), ((), ())), preferred_element_type=F32, precision=prec)


def _silu(x):
    return x * jax.nn.sigmoid(x)


def _softplus(x):
    return jnp.maximum(x, 0.0) + jnp.log1p(jnp.exp(-jnp.abs(x)))


def _seg_norm64(xb, gain_row):
    lane = lax.broadcasted_iota(I32, xb.shape, 1)
    lo = lane < B_HD
    sq = xb * xb
    s_lo = jnp.sum(jnp.where(lo, sq, 0.0), axis=-1, keepdims=True)
    s_hi = jnp.sum(jnp.where(lo, 0.0, sq), axis=-1, keepdims=True)
    ms = jnp.where(lo, s_lo, s_hi) * (1.0 / B_HD)
    return xb * lax.rsqrt(ms + EPS) * gain_row


_WB_QKV = 0
_WB_AZ = _WB_QKV + A_QKV
_WB_BQ = _WB_AZ + A_WIDTH
_WB_BK = _WB_BQ + B_WIDTH
_WB_BV = _WB_BK + B_KV_WIDTH
_WB_BZ = _WB_BV + B_KV_WIDTH
_WB_IQ = _WB_BZ + B_WIDTH
_WB_IK = _WB_IQ + IDX_WIDTH
_WB_END = _WB_IK + IDX_DIM


def _proj_kernel(x_ref, nw_ref, wb_ref, wf_ref, qg_ref, kg_ref,
                 qkv_ref, az_ref, bqx_ref, kx_ref, vx_ref, bz_ref, iq_ref, ik_ref, sm_ref):
    x = x_ref[...]
    h = x * lax.rsqrt(jnp.mean(x * x, axis=-1, keepdims=True) + EPS) * nw_ref[...]
    hb = h.astype(BF16)
    step = 512
    for c0 in range(0, A_QKV, step):
        qkv_ref[:, c0:c0 + step] = _dot(hb, wb_ref[:, _WB_QKV + c0:_WB_QKV + c0 + step])
    az_ref[...] = _dot(hb, wb_ref[:, _WB_AZ:_WB_BQ])
    bz_ref[...] = _dot(hb, wb_ref[:, _WB_BZ:_WB_IQ])
    lane = lax.broadcasted_iota(I32, (x.shape[0], LANES), 1)
    lo = lane < B_HD
    bk = _seg_norm64(_dot(hb, wb_ref[:, _WB_BK:_WB_BV]), kg_ref[...])
    bv = _dot(hb, wb_ref[:, _WB_BV:_WB_BZ])
    k_tail = jnp.where(lane == B_HD, 1.0, 0.0)
    for g in range(B_KV_HEADS):
        kg = bk if g == 0 else pltpu.roll(bk, B_HD, axis=1)
        vg = bv if g == 0 else pltpu.roll(bv, B_HD, axis=1)
        kx_ref[g] = jnp.where(lo, kg, k_tail).astype(BF16)
        vx_ref[g] = jnp.where(lo, vg, 1.0).astype(BF16)
    bq = _dot(hb, wb_ref[:, _WB_BQ:_WB_BK])
    scale = B_HD ** -0.5
    for c0 in range(0, B_WIDTH, LANES):
        nb = _seg_norm64(bq[:, c0:c0 + LANES], qg_ref[...]) * scale
        bqx_ref[:, 2 * c0:2 * c0 + LANES] = jnp.where(lo, nb, 0.0).astype(BF16)
        bqx_ref[:, 2 * c0 + LANES:2 * c0 + 2 * LANES] = jnp.where(
            lo, pltpu.roll(nb, B_HD, axis=1), 0.0).astype(BF16)
    for c0 in range(0, IDX_WIDTH, step):
        iq_ref[:, c0:c0 + step] = _dot(hb, wb_ref[:, _WB_IQ + c0:_WB_IQ + c0 + step]).astype(BF16)
    ik_ref[...] = _dot(hb, wb_ref[:, _WB_IK:_WB_END]).astype(BF16)
    sm_ref[...] = _dot(h, wf_ref[...], HI)


def _project(x2, norm_w, w_in, q_gain, k_gain, tm):
    t, d = x2.shape
    sizes = (A_HEADS * A_DK, A_HEADS * A_DK, A_WIDTH, A_WIDTH, A_HEADS, A_HEADS, B_WIDTH, B_KV_WIDTH,
             B_KV_WIDTH, B_WIDTH, IDX_WIDTH, IDX_DIM, IDX_HEADS)
    offs = [0]
    for s in sizes:
        offs.append(offs[-1] + s)
    (a_q, a_k, a_v, a_z, a_b, a_a, b_q, b_k, b_v, b_z, i_q, i_k, i_w) = [
        w_in[:, offs[n]:offs[n + 1]] for n in range(len(sizes))]
    wb = jnp.concatenate([a_q, a_k, a_v, a_z, b_q, b_k, b_v, b_z, i_q, i_k], axis=1).astype(BF16)
    pad = jnp.zeros((d, SMALL_W - 2 * A_HEADS - IDX_HEADS), F32)
    wf = jnp.concatenate([a_b, a_a, i_w, pad], axis=1)
    qg = jnp.tile(q_gain, LANES // B_HD)[None, :]
    kg = jnp.tile(k_gain, LANES // B_HD)[None, :]
    row = lambda w: pl.BlockSpec((tm, w), lambda i: (i, 0))
    grp = pl.BlockSpec((B_KV_HEADS, tm, LANES), lambda i: (0, i, 0))
    full = lambda a: pl.BlockSpec(a.shape, lambda i: (0, 0))
    nw = norm_w[None, :]
    sds = jax.ShapeDtypeStruct
    return pl.pallas_call(
        _proj_kernel,
        grid=(t // tm,),
        in_specs=[row(d), full(nw), full(wb), full(wf), full(qg), full(kg)],
        out_specs=[row(A_QKV), row(A_WIDTH), row(B_HEADS * LANES), grp, grp, row(B_WIDTH), row(IDX_WIDTH),
                   row(IDX_DIM), row(SMALL_W)],
        out_shape=[sds((t, A_QKV), F32), sds((t, A_WIDTH), F32), sds((t, B_HEADS * LANES), BF16),
                   sds((B_KV_HEADS, t, LANES), BF16), sds((B_KV_HEADS, t, LANES), BF16), sds((t, B_WIDTH), F32),
                   sds((t, IDX_WIDTH), BF16), sds((t, IDX_DIM), BF16), sds((t, SMALL_W), F32)],
        compiler_params=pltpu.CompilerParams(dimension_semantics=("arbitrary",), vmem_limit_bytes=VMEM_LIMIT),
        name="proj",
    )(x2, nw, wb, wf, qg, kg)


def _gdn_kernel(qkv_ref, sm_ref, smt_ref, az_ref, cw_ref, alog_r_ref, dtb_r_ref, alog_c_ref, dtb_c_ref,
                onw_ref, o_ref, s_ref, tail_ref, *, bt):
    c = A_CHUNK

    @pl.when(pl.program_id(1) == 0)
    def _():
        s_ref[...] = jnp.zeros(s_ref.shape, F32)
        tail_ref[...] = jnp.zeros(tail_ref.shape, F32)

    xin = qkv_ref[...]
    tail = tail_ref[...]
    cw = cw_ref[...]
    row8 = lax.broadcasted_iota(I32, tail.shape, 0)
    acc = xin * cw[A_CONV - 1:A_CONV, :]
    for d in range(1, A_CONV):
        xr = pltpu.roll(xin, d, axis=0)
        pr = pltpu.roll(tail, d, axis=0)
        head = jnp.where(row8 < d, pr, xr[0:8])
        xs = jnp.concatenate([head, xr[8:]], axis=0)
        acc = acc + xs * cw[A_CONV - 1 - d:A_CONV - d, :]
    tail_ref[...] = xin[bt - 8:bt]
    qkv = _silu(acc)

    sm = sm_ref[...]
    smt = smt_ref[0]
    g_col = -jnp.exp(alog_r_ref[...]) * _softplus(sm + dtb_r_ref[...])
    g_row = -jnp.exp(alog_c_ref[...]) * _softplus(smt + dtb_c_ref[...])
    beta_all = jax.nn.sigmoid(sm)

    r = lax.broadcasted_iota(I32, (bt, bt), 0)
    s = lax.broadcasted_iota(I32, (bt, bt), 1)
    same = (r // c) == (s // c)
    tril = same & (s <= r)
    strict = same & (s < r)
    eye = jnp.where(s == r, 1.0, 0.0).astype(F32)
    gc_col = _dot(jnp.where(tril, 1.0, 0.0).astype(F32), g_col, HI)
    gc_row = _dot(g_row, jnp.where(same & (r <= s), 1.0, 0.0).astype(F32), HI)
    gtot = _dot(jnp.where(same, 1.0, 0.0).astype(F32), g_col, HI)
    egc_all = jnp.exp(gc_col)
    ekd_all = jnp.exp(gtot - gc_col)
    egl_all = jnp.exp(gtot)
    onw = onw_ref[...]

    for h in range(A_HEADS):
        q = qkv[:, h * A_DK:(h + 1) * A_DK]
        k = qkv[:, A_HEADS * A_DK + h * A_DK:A_HEADS * A_DK + (h + 1) * A_DK]
        v = qkv[:, 2 * A_HEADS * A_DK + h * A_DV:2 * A_HEADS * A_DK + (h + 1) * A_DV]
        q = q * lax.rsqrt(jnp.sum(q * q, axis=-1, keepdims=True) + EPS) * (A_DK ** -0.5)
        k = k * lax.rsqrt(jnp.sum(k * k, axis=-1, keepdims=True) + EPS)
        beta = beta_all[:, SM_BETA + h:SM_BETA + h + 1]
        lane = SM_DECAY + h
        gcc = gc_col[:, lane:lane + 1]
        gcr = gc_row[lane:lane + 1, :]
        egc = egc_all[:, lane:lane + 1]
        egl = egl_all[:, lane:lane + 1]
        decay = jnp.where(tril, jnp.exp(jnp.where(tril, gcc - gcr, 0.0)), 0.0)
        kb = k * beta
        kbf = k.astype(BF16)
        a_mat = jnp.where(strict, _dot_nt(kb.astype(BF16), kbf) * decay, 0.0)
        t_mat = eye - a_mat
        a_pow = a_mat
        n = 2
        while n < c:
            a_bf = a_pow.astype(BF16)
            a_pow = _dot(a_bf, a_bf)
            t_mat = t_mat + _dot(t_mat.astype(BF16), a_pow.astype(BF16))
            n *= 2
        t_bf = t_mat.astype(BF16)
        value = _dot(t_bf, (v * beta).astype(BF16))
        kcd = _dot(t_bf, (kb * egc).astype(BF16)).astype(BF16)
        att = (_dot_nt(q.astype(BF16), kbf) * decay).astype(BF16)
        qd = (q * egc).astype(BF16)
        kd = (k * ekd_all[:, lane:lane + 1]).astype(BF16)
        state = s_ref[h]
        o_inter = []
        v_new = []
        for ci in range(bt // c):
            rows = slice(ci * c, (ci + 1) * c)
            s_bf = state.astype(BF16)
            vn = (value[rows] - _dot(kcd[rows], s_bf)).astype(BF16)
            o_inter.append(_dot(qd[rows], s_bf))
            v_new.append(vn)
            gl = egl[ci * c:ci * c + 8]
            state = state * jnp.concatenate([gl] * (A_DK // 8), axis=0) + _dot_tn(kd[rows], vn)
        s_ref[h] = state
        o = jnp.concatenate(o_inter, axis=0) + _dot(att, jnp.concatenate(v_new, axis=0))
        on = o * lax.rsqrt(jnp.mean(o * o, axis=-1, keepdims=True) + EPS) * onw
        z = az_ref[:, h * A_DV:(h + 1) * A_DV]
        o_ref[:, h * A_DV:(h + 1) * A_DV] = on * _silu(z)


def _gdn(qkv, small, small_t, a_z, conv_w, a_log, dt_bias, out_norm_w, batch, seq, bt):
    nt = seq // bt
    lane_pad = lambda v: jnp.zeros((1, LANES), F32).at[0, SM_DECAY:SM_DECAY + A_HEADS].set(v)
    sub_pad = lambda v: jnp.zeros((8, 1), F32).at[SM_DECAY:SM_DECAY + A_HEADS, 0].set(v)
    cw = jnp.zeros((8, A_QKV), F32).at[:A_CONV].set(conv_w)
    onw = out_norm_w[None, :]
    row = lambda w: pl.BlockSpec((bt, w), lambda b, t: (b * nt + t, 0))
    full = lambda a: pl.BlockSpec(a.shape, lambda b, t: (0, 0))
    consts = (cw, lane_pad(a_log), lane_pad(dt_bias), sub_pad(a_log), sub_pad(dt_bias), onw)
    return pl.pallas_call(
        functools.partial(_gdn_kernel, bt=bt),
        grid=(batch, nt),
        in_specs=[row(A_QKV), row(SMALL_W), pl.BlockSpec((1, 8, bt), lambda b, t: (b, 0, t)), row(A_WIDTH)]
        + [full(a) for a in consts],
        out_specs=row(A_WIDTH),
        out_shape=jax.ShapeDtypeStruct((batch * seq, A_WIDTH), F32),
        scratch_shapes=[pltpu.VMEM((A_HEADS, A_DK, A_DV), F32), pltpu.VMEM((8, A_QKV), F32)],
        compiler_params=pltpu.CompilerParams(dimension_semantics=("arbitrary", "arbitrary"),
                                             vmem_limit_bytes=VMEM_LIMIT),
        name="gdn",
    )(qkv, small, small_t, a_z, *consts)


def _dsa_kernel(q_ref, kx_ref, vx_ref, iq_ref, ik_ref, sm_ref, bz_ref, tri_ref, o_ref,
                sc_ref, hb_ref, mx_ref, acc_ref, *, qb, ks, n_sel):
    rep = B_HEADS // B_KV_HEADS
    i = pl.program_id(1)
    ns = (i * qb) // ks + 1
    row = lax.broadcasted_iota(I32, (qb, ks), 0)
    col = lax.broadcasted_iota(I32, (qb, ks), 1)
    qpos = i * qb + row

    iq = iq_ref[...]
    iqs = jnp.concatenate([iq[:, h * IDX_DIM:(h + 1) * IDX_DIM] for h in range(IDX_HEADS)], axis=0)
    iw = sm_ref[...] * (IDX_HEADS ** -0.5 * IDX_DIM ** -0.5)

    def score_body(j, carry):
        start = pl.multiple_of(j * ks, ks)
        lg = _dot_nt(iqs, ik_ref[pl.ds(start, ks), :])
        sc = jnp.zeros((qb, ks), F32)
        for h in range(IDX_HEADS):
            sc = sc + iw[:, SM_IW + h:SM_IW + h + 1] * jnp.maximum(lg[h * qb:(h + 1) * qb], 0.0)
        sc = jnp.where(j * ks + col <= qpos, sc, -jnp.inf)
        sc_ref[j] = sc
        hb_ref[j] = sc.astype(BF16)
        return carry

    lax.fori_loop(0, ns, score_body, 0)

    def count(src_ref, hit_of, dtype):
        def body(j, acc):
            for u in range(ks // LANES):
                acc = acc + hit_of(src_ref[j, :, u * LANES:(u + 1) * LANES])
            return acc

        acc = lax.fori_loop(0, ns, body, jnp.zeros((qb, LANES), dtype))
        return jnp.sum(acc.astype(F32), axis=-1, keepdims=True)

    def key_value(key):
        val = pltpu.bitcast(key ^ ((key >> 31) & 0x7FFFFFFF), F32)
        return jnp.where(key < KEY_MIN_FINITE, -jnp.inf, val)

    def count_ge(key):
        cand = jnp.broadcast_to(key_value(key), (qb, LANES))
        return count(sc_ref, lambda sc: jnp.where(sc >= cand, 1.0, 0.0), F32)

    one_b = jnp.ones((qb, LANES), BF16)
    zero_b = jnp.zeros((qb, LANES), BF16)

    def coarse_body(bi, tau):
        key = tau ^ lax.shift_left(jnp.int32(1), 31 - bi)
        cand = jnp.broadcast_to(key_value(key), (qb, LANES)).astype(BF16)
        cnt = count(hb_ref, lambda hb: jnp.where(hb >= cand, one_b, zero_b), BF16)
        return jnp.where(cnt >= n_sel, key, tau)

    coarse = lax.fori_loop(0, 16, coarse_body, jnp.full((qb, 1), INT_MIN, I32))
    step = 1 << 16
    lo0 = jnp.maximum(coarse, INT_MIN + 2 * step) - 2 * step
    hi0 = coarse + step

    def fine_cond(state):
        it, _, _, _, open_rows = state
        return jnp.logical_and(it < 20, open_rows > 0.0)

    def fine_body(state):
        it, lo, hi, c_lo, _ = state
        settled = (c_lo == n_sel) | (hi - lo <= 1)
        open_rows = jnp.max(jnp.where(settled, 0.0, 1.0))
        mid = lo + ((hi - lo) >> 1)
        cnt = count_ge(mid)
        ok = cnt >= n_sel
        return (it + 1, jnp.where(ok, mid, lo), jnp.where(ok, hi, mid), jnp.where(ok, cnt, c_lo), open_rows)

    _, tau, _, _, _ = lax.while_loop(
        fine_cond, fine_body, (jnp.int32(0), lo0, hi0, jnp.full((qb, 1), -1.0, F32), jnp.float32(1.0)))
    tau128 = jnp.broadcast_to(key_value(tau), (qb, LANES))
    taub = jnp.concatenate([tau128] * (ks // LANES), axis=1)
    need = n_sel - count(sc_ref, lambda sc: jnp.where(sc > tau128, 1.0, 0.0), F32)

    q = q_ref[...]
    qs = [jnp.concatenate([q[:, (g * rep + r) * LANES:(g * rep + r + 1) * LANES] for r in range(rep)], axis=0)
          for g in range(B_KV_HEADS)]
    mx_ref[...] = jnp.full(mx_ref.shape, NEG_BIG, F32)
    acc_ref[...] = jnp.zeros(acc_ref.shape, F32)
    tri = tri_ref[...]

    def att_body(j, eq_seen):
        start = pl.multiple_of(j * ks, ks)
        keyt = sc_ref[j]
        eqf = jnp.where(keyt == taub, 1.0, 0.0)
        pref = _dot(eqf.astype(BF16), tri) + eq_seen
        take = jnp.where(keyt > taub, 1.0, jnp.where(pref <= need, eqf, 0.0))
        bias = jnp.where((take > 0.0) & (j * ks + col <= qpos), 0.0, 2.0 * NEG_BIG)
        bias4 = jnp.concatenate([bias] * rep, axis=0)
        for g in range(B_KV_HEADS):
            s = _dot_nt(qs[g], kx_ref[g, pl.ds(start, ks), :]) + bias4
            t = s[:, 0:LANES]
            for u in range(1, ks // LANES):
                t = jnp.maximum(t, s[:, u * LANES:(u + 1) * LANES])
            m_old = mx_ref[g]
            m_new = jnp.maximum(m_old, jnp.max(t, axis=-1, keepdims=True))
            p = jnp.exp(s - jnp.concatenate([m_new] * (ks // LANES), axis=1)).astype(BF16)
            acc_ref[g] = jnp.exp(m_old - m_new) * acc_ref[g] + _dot(p, vx_ref[g, pl.ds(start, ks), :])
            mx_ref[g] = m_new
        return eq_seen + jnp.sum(eqf, axis=-1, keepdims=True)

    lax.fori_loop(0, ns, att_body, jnp.zeros((qb, 1), F32))

    lane = lax.broadcasted_iota(I32, (qb, LANES), 1)
    outs = [acc_ref[g] / acc_ref[g][:, B_HD:B_HD + 1] for g in range(B_KV_HEADS)]
    for cblk in range(B_WIDTH // LANES):
        g = (2 * cblk) // rep
        r0 = (2 * cblk) % rep
        a = outs[g][r0 * qb:(r0 + 1) * qb]
        b = pltpu.roll(outs[g][(r0 + 1) * qb:(r0 + 2) * qb], B_HD, axis=1)
        z = bz_ref[:, cblk * LANES:(cblk + 1) * LANES]
        o_ref[:, cblk * LANES:(cblk + 1) * LANES] = jnp.where(lane < B_HD, a, b) * _silu(z)


def _dsa(bqx, kx, vx, iq, ik, small, bz, batch, seq, qb, ks):
    nq = seq // qb
    n_sel = min(TOPK_MAX, seq // 4)
    rep = B_HEADS // B_KV_HEADS
    tri = (jnp.arange(ks)[:, None] <= jnp.arange(ks)[None, :]).astype(BF16)
    row = lambda w: pl.BlockSpec((qb, w), lambda b, i: (b * nq + i, 0))
    per_batch = lambda w: pl.BlockSpec((seq, w), lambda b, i: (b, 0))
    per_batch_grp = pl.BlockSpec((B_KV_HEADS, seq, LANES), lambda b, i: (0, b, 0))
    return pl.pallas_call(
        functools.partial(_dsa_kernel, qb=qb, ks=ks, n_sel=n_sel),
        grid=(batch, nq),
        in_specs=[row(B_HEADS * LANES), per_batch_grp, per_batch_grp, row(IDX_WIDTH), per_batch(IDX_DIM),
                  row(SMALL_W), row(B_WIDTH), pl.BlockSpec((ks, ks), lambda b, i: (0, 0))],
        out_specs=row(B_WIDTH),
        out_shape=jax.ShapeDtypeStruct((batch * seq, B_WIDTH), F32),
        scratch_shapes=[pltpu.VMEM((seq // ks, qb, ks), F32),
                        pltpu.VMEM((seq // ks, qb, ks), BF16),
                        pltpu.VMEM((B_KV_HEADS, rep * qb, LANES), F32),
                        pltpu.VMEM((B_KV_HEADS, rep * qb, LANES), F32)],
        compiler_params=pltpu.CompilerParams(dimension_semantics=("arbitrary", "arbitrary"),
                                             vmem_limit_bytes=VMEM_LIMIT),
        name="dsa",
    )(bqx, kx, vx, iq, ik, small, bz, tri)


def _out_kernel(x_ref, oa_ref, ob_ref, p_ref, woa_ref, wob_ref, wp_ref, wg_ref, gn_ref, bg_ref, o_ref):
    x1 = (x_ref[...] + _dot(oa_ref[...].astype(BF16), woa_ref[...])
          + _dot(ob_ref[...].astype(BF16), wob_ref[...]))
    hn = x1 * lax.rsqrt(jnp.mean(x1 * x1, axis=-1, keepdims=True) + EPS) * gn_ref[...]
    gate = jax.nn.sigmoid(_dot(hn.astype(BF16), wg_ref[...]) + bg_ref[...])
    o_ref[...] = x1 + _dot(p_ref[...].astype(BF16), wp_ref[...]) * gate


def _output(x2, oa, ob, p2, w_out, w_ple, gate_norm_w, w_gate, b_gate, tm):
    t, d = x2.shape
    woa = w_out[:A_WIDTH].astype(BF16)
    wob = w_out[A_WIDTH:].astype(BF16)
    wp = w_ple.astype(BF16)
    wg = w_gate.astype(BF16)
    gn = gate_norm_w[None, :]
    bg = b_gate[None, :]
    row = lambda w: pl.BlockSpec((tm, w), lambda i: (i, 0))
    full = lambda a: pl.BlockSpec(a.shape, lambda i: (0, 0))
    return pl.pallas_call(
        _out_kernel,
        grid=(t // tm,),
        in_specs=[row(d), row(A_WIDTH), row(B_WIDTH), row(PLE_DIM)] + [full(a) for a in (woa, wob, wp, wg, gn, bg)],
        out_specs=row(d),
        out_shape=jax.ShapeDtypeStruct((t, d), F32),
        compiler_params=pltpu.CompilerParams(dimension_semantics=("arbitrary",), vmem_limit_bytes=VMEM_LIMIT),
        name="out",
    )(x2, oa, ob, p2, woa, wob, wp, wg, gn, bg)


def kernel(x, p, attn_norm_w, w_in, conv_w, a_log, dt_bias, a_out_norm_w, b_q_norm_w, b_k_norm_w, w_out,
           w_ple, ple_gate_norm_w, w_ple_gate, b_ple_gate):
    batch, seq, d = x.shape
    t = batch * seq
    tm = min(512, t)
    bt = min(256, seq)
    qb = min(256, seq)
    ks = min(512, seq)
    x2 = x.reshape(t, d)
    for i in range(w_in.shape[0]):
        qkv, az, bqx, kx, vx, bz, iq, ik, small = _project(x2, attn_norm_w[i], w_in[i], b_q_norm_w[i],
                                                           b_k_norm_w[i], tm)
        small_t = jnp.swapaxes(small.reshape(batch, seq, SMALL_W)[:, :, :8], 1, 2)
        oa = _gdn(qkv, small, small_t, az, conv_w[i], a_log[i], dt_bias[i], a_out_norm_w[i], batch, seq, bt)
        ob = _dsa(bqx, kx, vx, iq, ik, small, bz, batch, seq, qb, ks)
        x2 = _output(x2, oa, ob, p[i].reshape(t, PLE_DIM), w_out[i], w_ple[i], ple_gate_norm_w[i],
                     w_ple_gate[i], b_ple_gate[i], tm)
    return x2.reshape(batch, seq, d)
```

```python
import functools

import jax
import jax.numpy as jnp
from jax import lax
from jax.experimental import pallas as pl
from jax.experimental.pallas import tpu as pltpu

F32 = jnp.float32
BF16 = jnp.bfloat16
I32 = jnp.int32
EPS = 1e-6
HI = lax.Precision.HIGHEST

PLE_DIM = 256
A_HEADS = 4
A_DK = 128
A_DV = 128
A_CONV = 4
A_CHUNK = 64
A_WIDTH = A_HEADS * A_DV
A_QKV = 2 * A_HEADS * A_DK + A_WIDTH
B_HEADS = 8
B_KV_HEADS = 2
B_HD = 64
B_WIDTH = B_HEADS * B_HD
B_KV_WIDTH = B_KV_HEADS * B_HD
IDX_HEADS = 8
IDX_DIM = 128
IDX_WIDTH = IDX_HEADS * IDX_DIM
TOPK_MAX = 256
LANES = 128
SMALL_W = LANES
SM_BETA = 0
SM_DECAY = A_HEADS
SM_IW = 2 * A_HEADS

VMEM_LIMIT = 56 * 1024 * 1024
NEG_BIG = -1e30
INT_MIN = -(2 ** 31)
KEY_MIN_FINITE = INT_MIN + 0x00800000


def _dot(a, b, prec=None):
    return jnp.dot(a, b, preferred_element_type=F32, precision=prec)


def _dot_nt(a, b, prec=None):
    return lax.dot_general(a, b, (((1,), (1,)), ((), ())), preferred_element_type=F32, precision=prec)


def _dot_tn(a, b, prec=None):
    return lax.dot_general(a, b, (((0,), (0,)), ((), ())), preferred_element_type=F32, precision=prec)


def _silu(x):
    return x * jax.nn.sigmoid(x)


def _softplus(x):
    return jnp.maximum(x, 0.0) + jnp.log1p(jnp.exp(-jnp.abs(x)))


def _seg_norm64(xb, gain_row):
    lane = lax.broadcasted_iota(I32, xb.shape, 1)
    lo = lane < B_HD
    sq = xb * xb
    s_lo = jnp.sum(jnp.where(lo, sq, 0.0), axis=-1, keepdims=True)
    s_hi = jnp.sum(jnp.where(lo, 0.0, sq), axis=-1, keepdims=True)
    ms = jnp.where(lo, s_lo, s_hi) * (1.0 / B_HD)
    return xb * lax.rsqrt(ms + EPS) * gain_row


_WB_QKV = 0
_WB_AZ = _WB_QKV + A_QKV
_WB_BQ = _WB_AZ + A_WIDTH
_WB_BK = _WB_BQ + B_WIDTH
_WB_BV = _WB_BK + B_KV_WIDTH
_WB_BZ = _WB_BV + B_KV_WIDTH
_WB_IQ = _WB_BZ + B_WIDTH
_WB_IK = _WB_IQ + IDX_WIDTH
_WB_END = _WB_IK + IDX_DIM


def _proj_kernel(x_ref, nw_ref, wb_ref, wf_ref, qg_ref, kg_ref,
                 qkv_ref, az_ref, bqx_ref, kx_ref, vx_ref, bz_ref, iq_ref, ik_ref, sm_ref):
    x = x_ref[...]
    h = x * lax.rsqrt(jnp.mean(x * x, axis=-1, keepdims=True) + EPS) * nw_ref[...]
    hb = h.astype(BF16)
    step = 512
    for c0 in range(0, A_QKV, step):
        qkv_ref[:, c0:c0 + step] = _dot(hb, wb_ref[:, _WB_QKV + c0:_WB_QKV + c0 + step])
    az_ref[...] = _dot(hb, wb_ref[:, _WB_AZ:_WB_BQ])
    bz_ref[...] = _dot(hb, wb_ref[:, _WB_BZ:_WB_IQ])
    lane = lax.broadcasted_iota(I32, (x.shape[0], LANES), 1)
    lo = lane < B_HD
    bk = _seg_norm64(_dot(hb, wb_ref[:, _WB_BK:_WB_BV]), kg_ref[...])
    bv = _dot(hb, wb_ref[:, _WB_BV:_WB_BZ])
    k_tail = jnp.where(lane == B_HD, 1.0, 0.0)
    for g in range(B_KV_HEADS):
        kg = bk if g == 0 else pltpu.roll(bk, B_HD, axis=1)
        vg = bv if g == 0 else pltpu.roll(bv, B_HD, axis=1)
        kx_ref[g] = jnp.where(lo, kg, k_tail).astype(BF16)
        vx_ref[g] = jnp.where(lo, vg, 1.0).astype(BF16)
    bq = _dot(hb, wb_ref[:, _WB_BQ:_WB_BK])
    scale = B_HD ** -0.5
    for c0 in range(0, B_WIDTH, LANES):
        nb = _seg_norm64(bq[:, c0:c0 + LANES], qg_ref[...]) * scale
        bqx_ref[:, 2 * c0:2 * c0 + LANES] = jnp.where(lo, nb, 0.0).astype(BF16)
        bqx_ref[:, 2 * c0 + LANES:2 * c0 + 2 * LANES] = jnp.where(
            lo, pltpu.roll(nb, B_HD, axis=1), 0.0).astype(BF16)
    for c0 in range(0, IDX_WIDTH, step):
        iq_ref[:, c0:c0 + step] = _dot(hb, wb_ref[:, _WB_IQ + c0:_WB_IQ + c0 + step]).astype(BF16)
    ik_ref[...] = _dot(hb, wb_ref[:, _WB_IK:_WB_END]).astype(BF16)
    sm_ref[...] = _dot(h, wf_ref[...], HI)


def _project(x2, norm_w, w_in, q_gain, k_gain, tm):
    t, d = x2.shape
    sizes = (A_HEADS * A_DK, A_HEADS * A_DK, A_WIDTH, A_WIDTH, A_HEADS, A_HEADS, B_WIDTH, B_KV_WIDTH,
             B_KV_WIDTH, B_WIDTH, IDX_WIDTH, IDX_DIM, IDX_HEADS)
    offs = [0]
    for s in sizes:
        offs.append(offs[-1] + s)
    (a_q, a_k, a_v, a_z, a_b, a_a, b_q, b_k, b_v, b_z, i_q, i_k, i_w) = [
        w_in[:, offs[n]:offs[n + 1]] for n in range(len(sizes))]
    wb = jnp.concatenate([a_q, a_k, a_v, a_z, b_q, b_k, b_v, b_z, i_q, i_k], axis=1).astype(BF16)
    pad = jnp.zeros((d, SMALL_W - 2 * A_HEADS - IDX_HEADS), F32)
    wf = jnp.concatenate([a_b, a_a, i_w, pad], axis=1)
    qg = jnp.tile(q_gain, LANES // B_HD)[None, :]
    kg = jnp.tile(k_gain, LANES // B_HD)[None, :]
    row = lambda w: pl.BlockSpec((tm, w), lambda i: (i, 0))
    grp = pl.BlockSpec((B_KV_HEADS, tm, LANES), lambda i: (0, i, 0))
    full = lambda a: pl.BlockSpec(a.shape, lambda i: (0, 0))
    nw = norm_w[None, :]
    sds = jax.ShapeDtypeStruct
    return pl.pallas_call(
        _proj_kernel,
        grid=(t // tm,),
        in_specs=[row(d), full(nw), full(wb), full(wf), full(qg), full(kg)],
        out_specs=[row(A_QKV), row(A_WIDTH), row(B_HEADS * LANES), grp, grp, row(B_WIDTH), row(IDX_WIDTH),
                   row(IDX_DIM), row(SMALL_W)],
        out_shape=[sds((t, A_QKV), F32), sds((t, A_WIDTH), F32), sds((t, B_HEADS * LANES), BF16),
                   sds((B_KV_HEADS, t, LANES), BF16), sds((B_KV_HEADS, t, LANES), BF16), sds((t, B_WIDTH), F32),
                   sds((t, IDX_WIDTH), BF16), sds((t, IDX_DIM), BF16), sds((t, SMALL_W), F32)],
        compiler_params=pltpu.CompilerParams(dimension_semantics=("arbitrary",), vmem_limit_bytes=VMEM_LIMIT),
        name="proj",
    )(x2, nw, wb, wf, qg, kg)


def _gdn_kernel(qkv_ref, sm_ref, smt_ref, az_ref, cw_ref, alog_r_ref, dtb_r_ref, alog_c_ref, dtb_c_ref,
                onw_ref, o_ref, s_ref, tail_ref, *, bt):
    c = A_CHUNK

    @pl.when(pl.program_id(1) == 0)
    def _():
        s_ref[...] = jnp.zeros(s_ref.shape, F32)
        tail_ref[...] = jnp.zeros(tail_ref.shape, F32)

    xin = qkv_ref[...]
    tail = tail_ref[...]
    cw = cw_ref[...]
    row8 = lax.broadcasted_iota(I32, tail.shape, 0)
    acc = xin * cw[A_CONV - 1:A_CONV, :]
    for d in range(1, A_CONV):
        xr = pltpu.roll(xin, d, axis=0)
        pr = pltpu.roll(tail, d, axis=0)
        head = jnp.where(row8 < d, pr, xr[0:8])
        xs = jnp.concatenate([head, xr[8:]], axis=0)
        acc = acc + xs * cw[A_CONV - 1 - d:A_CONV - d, :]
    tail_ref[...] = xin[bt - 8:bt]
    qkv = _silu(acc)

    sm = sm_ref[...]
    smt = smt_ref[0]
    g_col = -jnp.exp(alog_r_ref[...]) * _softplus(sm + dtb_r_ref[...])
    g_row = -jnp.exp(alog_c_ref[...]) * _softplus(smt + dtb_c_ref[...])
    beta_all = jax.nn.sigmoid(sm)

    r = lax.broadcasted_iota(I32, (bt, bt), 0)
    s = lax.broadcasted_iota(I32, (bt, bt), 1)
    same = (r // c) == (s // c)
    tril = same & (s <= r)
    strict = same & (s < r)
    eye = jnp.where(s == r, 1.0, 0.0).astype(F32)
    gc_col = _dot(jnp.where(tril, 1.0, 0.0).astype(F32), g_col, HI)
    gc_row = _dot(g_row, jnp.where(same & (r <= s), 1.0, 0.0).astype(F32), HI)
    gtot = _dot(jnp.where(same, 1.0, 0.0).astype(F32), g_col, HI)
    egc_all = jnp.exp(gc_col)
    ekd_all = jnp.exp(gtot - gc_col)
    egl_all = jnp.exp(gtot)
    onw = onw_ref[...]

    for h in range(A_HEADS):
        q = qkv[:, h * A_DK:(h + 1) * A_DK]
        k = qkv[:, A_HEADS * A_DK + h * A_DK:A_HEADS * A_DK + (h + 1) * A_DK]
        v = qkv[:, 2 * A_HEADS * A_DK + h * A_DV:2 * A_HEADS * A_DK + (h + 1) * A_DV]
        q = q * lax.rsqrt(jnp.sum(q * q, axis=-1, keepdims=True) + EPS) * (A_DK ** -0.5)
        k = k * lax.rsqrt(jnp.sum(k * k, axis=-1, keepdims=True) + EPS)
        beta = beta_all[:, SM_BETA + h:SM_BETA + h + 1]
        lane = SM_DECAY + h
        gcc = gc_col[:, lane:lane + 1]
        gcr = gc_row[lane:lane + 1, :]
        egc = egc_all[:, lane:lane + 1]
        egl = egl_all[:, lane:lane + 1]
        decay = jnp.where(tril, jnp.exp(jnp.where(tril, gcc - gcr, 0.0)), 0.0)
        kb = k * beta
        kbf = k.astype(BF16)
        a_mat = jnp.where(strict, _dot_nt(kb.astype(BF16), kbf) * decay, 0.0)
        t_mat = eye - a_mat
        a_pow = a_mat
        n = 2
        while n < c:
            a_bf = a_pow.astype(BF16)
            a_pow = _dot(a_bf, a_bf)
            t_mat = t_mat + _dot(t_mat.astype(BF16), a_pow.astype(BF16))
            n *= 2
        t_bf = t_mat.astype(BF16)
        value = _dot(t_bf, (v * beta).astype(BF16))
        kcd = _dot(t_bf, (kb * egc).astype(BF16)).astype(BF16)
        att = (_dot_nt(q.astype(BF16), kbf) * decay).astype(BF16)
        qd = (q * egc).astype(BF16)
        kd = (k * ekd_all[:, lane:lane + 1]).astype(BF16)
        state = s_ref[h]
        o_inter = []
        v_new = []
        for ci in range(bt // c):
            rows = slice(ci * c, (ci + 1) * c)
            s_bf = state.astype(BF16)
            vn = (value[rows] - _dot(kcd[rows], s_bf)).astype(BF16)
            o_inter.append(_dot(qd[rows], s_bf))
            v_new.append(vn)
            gl = egl[ci * c:ci * c + 8]
            state = state * jnp.concatenate([gl] * (A_DK // 8), axis=0) + _dot_tn(kd[rows], vn)
        s_ref[h] = state
        o = jnp.concatenate(o_inter, axis=0) + _dot(att, jnp.concatenate(v_new, axis=0))
        on = o * lax.rsqrt(jnp.mean(o * o, axis=-1, keepdims=True) + EPS) * onw
        z = az_ref[:, h * A_DV:(h + 1) * A_DV]
        o_ref[:, h * A_DV:(h + 1) * A_DV] = on * _silu(z)


def _gdn(qkv, small, small_t, a_z, conv_w, a_log, dt_bias, out_norm_w, batch, seq, bt):
    nt = seq // bt
    lane_pad = lambda v: jnp.zeros((1, LANES), F32).at[0, SM_DECAY:SM_DECAY + A_HEADS].set(v)
    sub_pad = lambda v: jnp.zeros((8, 1), F32).at[SM_DECAY:SM_DECAY + A_HEADS, 0].set(v)
    cw = jnp.zeros((8, A_QKV), F32).at[:A_CONV].set(conv_w)
    onw = out_norm_w[None, :]
    row = lambda w: pl.BlockSpec((bt, w), lambda b, t: (b * nt + t, 0))
    full = lambda a: pl.BlockSpec(a.shape, lambda b, t: (0, 0))
    consts = (cw, lane_pad(a_log), lane_pad(dt_bias), sub_pad(a_log), sub_pad(dt_bias), onw)
    return pl.pallas_call(
        functools.partial(_gdn_kernel, bt=bt),
        grid=(batch, nt),
        in_specs=[row(A_QKV), row(SMALL_W), pl.BlockSpec((1, 8, bt), lambda b, t: (b, 0, t)), row(A_WIDTH)]
        + [full(a) for a in consts],
        out_specs=row(A_WIDTH),
        out_shape=jax.ShapeDtypeStruct((batch * seq, A_WIDTH), F32),
        scratch_shapes=[pltpu.VMEM((A_HEADS, A_DK, A_DV), F32), pltpu.VMEM((8, A_QKV), F32)],
        compiler_params=pltpu.CompilerParams(dimension_semantics=("arbitrary", "arbitrary"),
                                             vmem_limit_bytes=VMEM_LIMIT),
        name="gdn",
    )(qkv, small, small_t, a_z, *consts)


def _dsa_kernel(q_ref, kx_ref, vx_ref, iq_ref, ik_ref, sm_ref, bz_ref, tri_ref, o_ref,
                sc_ref, hb_ref, mx_ref, acc_ref, *, qb, ks, n_sel):
    rep = B_HEADS // B_KV_HEADS
    i = pl.program_id(1)
    ns = (i * qb) // ks + 1
    row = lax.broadcasted_iota(I32, (qb, ks), 0)
    col = lax.broadcasted_iota(I32, (qb, ks), 1)
    qpos = i * qb + row

    iq = iq_ref[...]
    iqs = jnp.concatenate([iq[:, h * IDX_DIM:(h + 1) * IDX_DIM] for h in range(IDX_HEADS)], axis=0)
    iw = sm_ref[...] * (IDX_HEADS ** -0.5 * IDX_DIM ** -0.5)

    def score_body(j, carry):
        start = pl.multiple_of(j * ks, ks)
        lg = _dot_nt(iqs, ik_ref[pl.ds(start, ks), :])
        sc = jnp.zeros((qb, ks), F32)
        for h in range(IDX_HEADS):
            sc = sc + iw[:, SM_IW + h:SM_IW + h + 1] * jnp.maximum(lg[h * qb:(h + 1) * qb], 0.0)
        sc = jnp.where(j * ks + col <= qpos, sc, -jnp.inf)
        sc_ref[j] = sc
        hb_ref[j] = sc.astype(BF16)
        return carry

    lax.fori_loop(0, ns, score_body, 0)

    def count(src_ref, hit_of, dtype):
        def body(j, acc):
            for u in range(ks // LANES):
                acc = acc + hit_of(src_ref[j, :, u * LANES:(u + 1) * LANES])
            return acc

        acc = lax.fori_loop(0, ns, body, jnp.zeros((qb, LANES), dtype))
        return jnp.sum(acc.astype(F32), axis=-1, keepdims=True)

    def key_value(key):
        val = pltpu.bitcast(key ^ ((key >> 31) & 0x7FFFFFFF), F32)
        return jnp.where(key < KEY_MIN_FINITE, -jnp.inf, val)

    def count_ge(key):
        cand = jnp.broadcast_to(key_value(key), (qb, LANES))
        return count(sc_ref, lambda sc: jnp.where(sc >= cand, 1.0, 0.0), F32)

    one_b = jnp.ones((qb, LANES), BF16)
    zero_b = jnp.zeros((qb, LANES), BF16)

    def coarse_value(key):
        bits = (key ^ ((key >> 31) & 0x7FFFFFFF)) & (-(1 << 16))
        return jnp.where(key < KEY_MIN_FINITE, -jnp.inf, pltpu.bitcast(bits, F32))

    def count_coarse(hit_of):
        return count(hb_ref, lambda hb: jnp.where(hit_of(hb), one_b, zero_b), BF16)

    def coarse_body(bi, tau):
        key = tau ^ lax.shift_left(jnp.int32(1), 31 - bi)
        cand = jnp.broadcast_to(coarse_value(key), (qb, LANES)).astype(BF16)
        return jnp.where(count_coarse(lambda hb: hb >= cand) >= n_sel, key, tau)

    coarse = lax.fori_loop(0, 16, coarse_body, jnp.full((qb, 1), INT_MIN, I32))
    step = 1 << 16
    centre = jnp.where(coarse < 0, coarse | (step - 1), coarse)
    lo0 = jnp.maximum(centre, INT_MIN + 2 * step) - 2 * step
    hi0 = centre + step
    at_zero = ((count_coarse(lambda hb: hb > zero_b) < n_sel) & (count_coarse(lambda hb: hb >= zero_b) >= n_sel))
    lo0 = jnp.where(at_zero, 0, lo0)
    hi0 = jnp.where(at_zero, 1, hi0)

    def fine_cond(state):
        it, _, _, _, open_rows = state
        return jnp.logical_and(it < 20, open_rows > 0.0)

    def fine_body(state):
        it, lo, hi, c_lo, _ = state
        settled = (c_lo == n_sel) | (hi - lo <= 1)
        open_rows = jnp.max(jnp.where(settled, 0.0, 1.0))
        mid = lo + ((hi - lo) >> 1)
        cnt = count_ge(mid)
        ok = cnt >= n_sel
        return (it + 1, jnp.where(ok, mid, lo), jnp.where(ok, hi, mid), jnp.where(ok, cnt, c_lo), open_rows)

    _, tau, _, _, _ = lax.while_loop(
        fine_cond, fine_body, (jnp.int32(0), lo0, hi0, jnp.full((qb, 1), -1.0, F32), jnp.float32(1.0)))
    tau128 = jnp.broadcast_to(key_value(tau), (qb, LANES))
    taub = jnp.concatenate([tau128] * (ks // LANES), axis=1)
    need = n_sel - count(sc_ref, lambda sc: jnp.where(sc > tau128, 1.0, 0.0), F32)

    q = q_ref[...]
    qs = [jnp.concatenate([q[:, (g * rep + r) * LANES:(g * rep + r + 1) * LANES] for r in range(rep)], axis=0)
          for g in range(B_KV_HEADS)]
    mx_ref[...] = jnp.full(mx_ref.shape, NEG_BIG, F32)
    acc_ref[...] = jnp.zeros(acc_ref.shape, F32)
    tri = tri_ref[...]

    def att_body(j, eq_seen):
        start = pl.multiple_of(j * ks, ks)
        keyt = sc_ref[j]
        eqf = jnp.where(keyt == taub, 1.0, 0.0)
        pref = _dot(eqf.astype(BF16), tri) + eq_seen
        take = jnp.where(keyt > taub, 1.0, jnp.where(pref <= need, eqf, 0.0))
        bias = jnp.where((take > 0.0) & (j * ks + col <= qpos), 0.0, 2.0 * NEG_BIG)
        bias4 = jnp.concatenate([bias] * rep, axis=0)
        for g in range(B_KV_HEADS):
            s = _dot_nt(qs[g], kx_ref[g, pl.ds(start, ks), :]) + bias4
            t = s[:, 0:LANES]
            for u in range(1, ks // LANES):
                t = jnp.maximum(t, s[:, u * LANES:(u + 1) * LANES])
            m_old = mx_ref[g]
            m_new = jnp.maximum(m_old, jnp.max(t, axis=-1, keepdims=True))
            p = jnp.exp(s - jnp.concatenate([m_new] * (ks // LANES), axis=1)).astype(BF16)
            acc_ref[g] = jnp.exp(m_old - m_new) * acc_ref[g] + _dot(p, vx_ref[g, pl.ds(start, ks), :])
            mx_ref[g] = m_new
        return eq_seen + jnp.sum(eqf, axis=-1, keepdims=True)

    lax.fori_loop(0, ns, att_body, jnp.zeros((qb, 1), F32))

    lane = lax.broadcasted_iota(I32, (qb, LANES), 1)
    outs = [acc_ref[g] / acc_ref[g][:, B_HD:B_HD + 1] for g in range(B_KV_HEADS)]
    for cblk in range(B_WIDTH // LANES):
        g = (2 * cblk) // rep
        r0 = (2 * cblk) % rep
        a = outs[g][r0 * qb:(r0 + 1) * qb]
        b = pltpu.roll(outs[g][(r0 + 1) * qb:(r0 + 2) * qb], B_HD, axis=1)
        z = bz_ref[:, cblk * LANES:(cblk + 1) * LANES]
        o_ref[:, cblk * LANES:(cblk + 1) * LANES] = jnp.where(lane < B_HD, a, b) * _silu(z)


def _dsa(bqx, kx, vx, iq, ik, small, bz, batch, seq, qb, ks):
    nq = seq // qb
    n_sel = min(TOPK_MAX, seq // 4)
    rep = B_HEADS // B_KV_HEADS
    tri = (jnp.arange(ks)[:, None] <= jnp.arange(ks)[None, :]).astype(BF16)
    row = lambda w: pl.BlockSpec((qb, w), lambda b, i: (b * nq + i, 0))
    per_batch = lambda w: pl.BlockSpec((seq, w), lambda b, i: (b, 0))
    per_batch_grp = pl.BlockSpec((B_KV_HEADS, seq, LANES), lambda b, i: (0, b, 0))
    return pl.pallas_call(
        functools.partial(_dsa_kernel, qb=qb, ks=ks, n_sel=n_sel),
        grid=(batch, nq),
        in_specs=[row(B_HEADS * LANES), per_batch_grp, per_batch_grp, row(IDX_WIDTH), per_batch(IDX_DIM),
                  row(SMALL_W), row(B_WIDTH), pl.BlockSpec((ks, ks), lambda b, i: (0, 0))],
        out_specs=row(B_WIDTH),
        out_shape=jax.ShapeDtypeStruct((batch * seq, B_WIDTH), F32),
        scratch_shapes=[pltpu.VMEM((seq // ks, qb, ks), F32),
                        pltpu.VMEM((seq // ks, qb, ks), BF16),
                        pltpu.VMEM((B_KV_HEADS, rep * qb, LANES), F32),
                        pltpu.VMEM((B_KV_HEADS, rep * qb, LANES), F32)],
        compiler_params=pltpu.CompilerParams(dimension_semantics=("arbitrary", "arbitrary"),
                                             vmem_limit_bytes=VMEM_LIMIT),
        name="dsa",
    )(bqx, kx, vx, iq, ik, small, bz, tri)


def _out_kernel(x_ref, oa_ref, ob_ref, p_ref, woa_ref, wob_ref, wp_ref, wg_ref, gn_ref, bg_ref, o_ref):
    x1 = (x_ref[...] + _dot(oa_ref[...].astype(BF16), woa_ref[...])
          + _dot(ob_ref[...].astype(BF16), wob_ref[...]))
    hn = x1 * lax.rsqrt(jnp.mean(x1 * x1, axis=-1, keepdims=True) + EPS) * gn_ref[...]
    gate = jax.nn.sigmoid(_dot(hn.astype(BF16), wg_ref[...]) + bg_ref[...])
    o_ref[...] = x1 + _dot(p_ref[...].astype(BF16), wp_ref[...]) * gate


def _output(x2, oa, ob, p2, w_out, w_ple, gate_norm_w, w_gate, b_gate, tm):
    t, d = x2.shape
    woa = w_out[:A_WIDTH].astype(BF16)
    wob = w_out[A_WIDTH:].astype(BF16)
    wp = w_ple.astype(BF16)
    wg = w_gate.astype(BF16)
    gn = gate_norm_w[None, :]
    bg = b_gate[None, :]
    row = lambda w: pl.BlockSpec((tm, w), lambda i: (i, 0))
    full = lambda a: pl.BlockSpec(a.shape, lambda i: (0, 0))
    return pl.pallas_call(
        _out_kernel,
        grid=(t // tm,),
        in_specs=[row(d), row(A_WIDTH), row(B_WIDTH), row(PLE_DIM)] + [full(a) for a in (woa, wob, wp, wg, gn, bg)],
        out_specs=row(d),
        out_shape=jax.ShapeDtypeStruct((t, d), F32),
        compiler_params=pltpu.CompilerParams(dimension_semantics=("arbitrary",), vmem_limit_bytes=VMEM_LIMIT),
        name="out",
    )(x2, oa, ob, p2, woa, wob, wp, wg, gn, bg)


def kernel(x, p, attn_norm_w, w_in, conv_w, a_log, dt_bias, a_out_norm_w, b_q_norm_w, b_k_norm_w, w_out,
           w_ple, ple_gate_norm_w, w_ple_gate, b_ple_gate):
    batch, seq, d = x.shape
    t = batch * seq
    tm = min(512, t)
    bt = min(256, seq)
    qb = min(256, seq)
    ks = min(512, seq)
    x2 = x.reshape(t, d)
    for i in range(w_in.shape[0]):
        qkv, az, bqx, kx, vx, bz, iq, ik, small = _project(x2, attn_norm_w[i], w_in[i], b_q_norm_w[i],
                                                           b_k_norm_w[i], tm)
        small_t = jnp.swapaxes(small.reshape(batch, seq, SMALL_W)[:, :, :8], 1, 2)
        oa = _gdn(qkv, small, small_t, az, conv_w[i], a_log[i], dt_bias[i], a_out_norm_w[i], batch, seq, bt)
        ob = _dsa(bqx, kx, vx, iq, ik, small, bz, batch, seq, qb, ks)
        x2 = _output(x2, oa, ob, p[i].reshape(t, PLE_DIM), w_out[i], w_ple[i], ple_gate_norm_w[i],
                     w_ple_gate[i], b_ple_gate[i], tm)
    return x2.reshape(batch, seq, d)
```

```python
import functools

import jax
import jax.numpy as jnp
from jax import lax
from jax.experimental import pallas as pl
from jax.experimental.pallas import tpu as pltpu

F32 = jnp.float32
BF16 = jnp.bfloat16
I32 = jnp.int32
EPS = 1e-6
HI = lax.Precision.HIGHEST

PLE_DIM = 256
A_HEADS = 4
A_DK = 128
A_DV = 128
A_CONV = 4
A_CHUNK = 64
A_WIDTH = A_HEADS * A_DV
A_QKV = 2 * A_HEADS * A_DK + A_WIDTH
B_HEADS = 8
B_KV_HEADS = 2
B_HD = 64
B_WIDTH = B_HEADS * B_HD
B_KV_WIDTH = B_KV_HEADS * B_HD
IDX_HEADS = 8
IDX_DIM = 128
IDX_WIDTH = IDX_HEADS * IDX_DIM
TOPK_MAX = 256
LANES = 128
SMALL_W = LANES
SM_BETA = 0
SM_DECAY = A_HEADS
SM_IW = 2 * A_HEADS

VMEM_LIMIT = 56 * 1024 * 1024
NEG_BIG = -1e30
INT_MIN = -(2 ** 31)
KEY_MIN_FINITE = INT_MIN + 0x00800000


def _dot(a, b, prec=None):
    return jnp.dot(a, b, preferred_element_type=F32, precision=prec)


def _dot_nt(a, b, prec=None):
    return lax.dot_general(a, b, (((1,), (1,)), ((), ())), preferred_element_type=F32, precision=prec)


def _dot_tn(a, b, prec=None):
    return lax.dot_general(a, b, (((0,), (0,)), ((), ())), preferred_element_type=F32, precision=prec)


def _silu(x):
    return x * jax.nn.sigmoid(x)


def _softplus(x):
    return jnp.maximum(x, 0.0) + jnp.log1p(jnp.exp(-jnp.abs(x)))


def _seg_norm64(xb, gain_row):
    lane = lax.broadcasted_iota(I32, xb.shape, 1)
    lo = lane < B_HD
    sq = xb * xb
    s_lo = jnp.sum(jnp.where(lo, sq, 0.0), axis=-1, keepdims=True)
    s_hi = jnp.sum(jnp.where(lo, 0.0, sq), axis=-1, keepdims=True)
    ms = jnp.where(lo, s_lo, s_hi) * (1.0 / B_HD)
    return xb * lax.rsqrt(ms + EPS) * gain_row


_WB_QKV = 0
_WB_AZ = _WB_QKV + A_QKV
_WB_BQ = _WB_AZ + A_WIDTH
_WB_BK = _WB_BQ + B_WIDTH
_WB_BV = _WB_BK + B_KV_WIDTH
_WB_BZ = _WB_BV + B_KV_WIDTH
_WB_IQ = _WB_BZ + B_WIDTH
_WB_IK = _WB_IQ + IDX_WIDTH
_WB_END = _WB_IK + IDX_DIM


def _proj_kernel(x_ref, nw_ref, wb_ref, wf_ref, qg_ref, kg_ref,
                 qkv_ref, az_ref, bqx_ref, kx_ref, vx_ref, bz_ref, iq_ref, ik_ref, sm_ref):
    x = x_ref[...]
    h = x * lax.rsqrt(jnp.mean(x * x, axis=-1, keepdims=True) + EPS) * nw_ref[...]
    hb = h.astype(BF16)
    step = 512
    for c0 in range(0, A_QKV, step):
        qkv_ref[:, c0:c0 + step] = _dot(hb, wb_ref[:, _WB_QKV + c0:_WB_QKV + c0 + step])
    az_ref[...] = _dot(hb, wb_ref[:, _WB_AZ:_WB_BQ])
    bz_ref[...] = _dot(hb, wb_ref[:, _WB_BZ:_WB_IQ])
    lane = lax.broadcasted_iota(I32, (x.shape[0], LANES), 1)
    lo = lane < B_HD
    bk = _seg_norm64(_dot(hb, wb_ref[:, _WB_BK:_WB_BV]), kg_ref[...])
    bv = _dot(hb, wb_ref[:, _WB_BV:_WB_BZ])
    k_tail = jnp.where(lane == B_HD, 1.0, 0.0)
    for g in range(B_KV_HEADS):
        kg = bk if g == 0 else pltpu.roll(bk, B_HD, axis=1)
        vg = bv if g == 0 else pltpu.roll(bv, B_HD, axis=1)
        kx_ref[g] = jnp.where(lo, kg, k_tail).astype(BF16)
        vx_ref[g] = jnp.where(lo, vg, 1.0).astype(BF16)
    bq = _dot(hb, wb_ref[:, _WB_BQ:_WB_BK])
    scale = B_HD ** -0.5
    for c0 in range(0, B_WIDTH, LANES):
        nb = _seg_norm64(bq[:, c0:c0 + LANES], qg_ref[...]) * scale
        bqx_ref[:, 2 * c0:2 * c0 + LANES] = jnp.where(lo, nb, 0.0).astype(BF16)
        bqx_ref[:, 2 * c0 + LANES:2 * c0 + 2 * LANES] = jnp.where(
            lo, pltpu.roll(nb, B_HD, axis=1), 0.0).astype(BF16)
    for c0 in range(0, IDX_WIDTH, step):
        iq_ref[:, c0:c0 + step] = _dot(hb, wb_ref[:, _WB_IQ + c0:_WB_IQ + c0 + step]).astype(BF16)
    ik_ref[...] = _dot(hb, wb_ref[:, _WB_IK:_WB_END]).astype(BF16)
    sm_ref[...] = _dot(h, wf_ref[...], HI)


def _project(x2, norm_w, w_in, q_gain, k_gain, tm):
    t, d = x2.shape
    sizes = (A_HEADS * A_DK, A_HEADS * A_DK, A_WIDTH, A_WIDTH, A_HEADS, A_HEADS, B_WIDTH, B_KV_WIDTH,
             B_KV_WIDTH, B_WIDTH, IDX_WIDTH, IDX_DIM, IDX_HEADS)
    offs = [0]
    for s in sizes:
        offs.append(offs[-1] + s)
    (a_q, a_k, a_v, a_z, a_b, a_a, b_q, b_k, b_v, b_z, i_q, i_k, i_w) = [
        w_in[:, offs[n]:offs[n + 1]] for n in range(len(sizes))]
    wb = jnp.concatenate([a_q, a_k, a_v, a_z, b_q, b_k, b_v, b_z, i_q, i_k], axis=1).astype(BF16)
    pad = jnp.zeros((d, SMALL_W - 2 * A_HEADS - IDX_HEADS), F32)
    wf = jnp.concatenate([a_b, a_a, i_w, pad], axis=1)
    qg = jnp.tile(q_gain, LANES // B_HD)[None, :]
    kg = jnp.tile(k_gain, LANES // B_HD)[None, :]
    row = lambda w: pl.BlockSpec((tm, w), lambda i: (i, 0))
    grp = pl.BlockSpec((B_KV_HEADS, tm, LANES), lambda i: (0, i, 0))
    full = lambda a: pl.BlockSpec(a.shape, lambda i: (0, 0))
    nw = norm_w[None, :]
    sds = jax.ShapeDtypeStruct
    return pl.pallas_call(
        _proj_kernel,
        grid=(t // tm,),
        in_specs=[row(d), full(nw), full(wb), full(wf), full(qg), full(kg)],
        out_specs=[row(A_QKV), row(A_WIDTH), row(B_HEADS * LANES), grp, grp, row(B_WIDTH), row(IDX_WIDTH),
                   row(IDX_DIM), row(SMALL_W)],
        out_shape=[sds((t, A_QKV), F32), sds((t, A_WIDTH), F32), sds((t, B_HEADS * LANES), BF16),
                   sds((B_KV_HEADS, t, LANES), BF16), sds((B_KV_HEADS, t, LANES), BF16), sds((t, B_WIDTH), F32),
                   sds((t, IDX_WIDTH), BF16), sds((t, IDX_DIM), BF16), sds((t, SMALL_W), F32)],
        compiler_params=pltpu.CompilerParams(dimension_semantics=("arbitrary",), vmem_limit_bytes=VMEM_LIMIT),
        name="proj",
    )(x2, nw, wb, wf, qg, kg)


def _gdn_kernel(qkv_ref, sm_ref, smt_ref, az_ref, cw_ref, alog_r_ref, dtb_r_ref, alog_c_ref, dtb_c_ref,
                onw_ref, o_ref, s_ref, tail_ref, *, bt):
    c = A_CHUNK

    @pl.when(pl.program_id(1) == 0)
    def _():
        s_ref[...] = jnp.zeros(s_ref.shape, F32)
        tail_ref[...] = jnp.zeros(tail_ref.shape, F32)

    xin = qkv_ref[...]
    tail = tail_ref[...]
    cw = cw_ref[...]
    row8 = lax.broadcasted_iota(I32, tail.shape, 0)
    acc = xin * cw[A_CONV - 1:A_CONV, :]
    for d in range(1, A_CONV):
        xr = pltpu.roll(xin, d, axis=0)
        pr = pltpu.roll(tail, d, axis=0)
        head = jnp.where(row8 < d, pr, xr[0:8])
        xs = jnp.concatenate([head, xr[8:]], axis=0)
        acc = acc + xs * cw[A_CONV - 1 - d:A_CONV - d, :]
    tail_ref[...] = xin[bt - 8:bt]
    qkv = _silu(acc)

    sm = sm_ref[...]
    smt = smt_ref[0]
    g_col = -jnp.exp(alog_r_ref[...]) * _softplus(sm + dtb_r_ref[...])
    g_row = -jnp.exp(alog_c_ref[...]) * _softplus(smt + dtb_c_ref[...])
    beta_all = jax.nn.sigmoid(sm)

    r = lax.broadcasted_iota(I32, (bt, bt), 0)
    s = lax.broadcasted_iota(I32, (bt, bt), 1)
    same = (r // c) == (s // c)
    tril = same & (s <= r)
    strict = same & (s < r)
    eye = jnp.where(s == r, 1.0, 0.0).astype(F32)
    gc_col = _dot(jnp.where(tril, 1.0, 0.0).astype(F32), g_col, HI)
    gc_row = _dot(g_row, jnp.where(same & (r <= s), 1.0, 0.0).astype(F32), HI)
    gtot = _dot(jnp.where(same, 1.0, 0.0).astype(F32), g_col, HI)
    egc_all = jnp.exp(gc_col)
    ekd_all = jnp.exp(gtot - gc_col)
    egl_all = jnp.exp(gtot)
    onw = onw_ref[...]

    for h in range(A_HEADS):
        q = qkv[:, h * A_DK:(h + 1) * A_DK]
        k = qkv[:, A_HEADS * A_DK + h * A_DK:A_HEADS * A_DK + (h + 1) * A_DK]
        v = qkv[:, 2 * A_HEADS * A_DK + h * A_DV:2 * A_HEADS * A_DK + (h + 1) * A_DV]
        q = q * lax.rsqrt(jnp.sum(q * q, axis=-1, keepdims=True) + EPS) * (A_DK ** -0.5)
        k = k * lax.rsqrt(jnp.sum(k * k, axis=-1, keepdims=True) + EPS)
        beta = beta_all[:, SM_BETA + h:SM_BETA + h + 1]
        lane = SM_DECAY + h
        gcc = gc_col[:, lane:lane + 1]
        gcr = gc_row[lane:lane + 1, :]
        egc = egc_all[:, lane:lane + 1]
        egl = egl_all[:, lane:lane + 1]
        decay = jnp.where(tril, jnp.exp(jnp.where(tril, gcc - gcr, 0.0)), 0.0)
        kb = k * beta
        kbf = k.astype(BF16)
        a_mat = jnp.where(strict, _dot_nt(kb.astype(BF16), kbf) * decay, 0.0)
        t_mat = eye - a_mat
        a_pow = a_mat
        n = 2
        while n < c:
            a_bf = a_pow.astype(BF16)
            a_pow = _dot(a_bf, a_bf)
            t_mat = t_mat + _dot(t_mat.astype(BF16), a_pow.astype(BF16))
            n *= 2
        t_bf = t_mat.astype(BF16)
        value = _dot(t_bf, (v * beta).astype(BF16))
        kcd = _dot(t_bf, (kb * egc).astype(BF16)).astype(BF16)
        att = (_dot_nt(q.astype(BF16), kbf) * decay).astype(BF16)
        qd = (q * egc).astype(BF16)
        kd = (k * ekd_all[:, lane:lane + 1]).astype(BF16)
        state = s_ref[h]
        o_inter = []
        v_new = []
        for ci in range(bt // c):
            rows = slice(ci * c, (ci + 1) * c)
            s_bf = state.astype(BF16)
            vn = (value[rows] - _dot(kcd[rows], s_bf)).astype(BF16)
            o_inter.append(_dot(qd[rows], s_bf))
            v_new.append(vn)
            gl = egl[ci * c:ci * c + 8]
            state = state * jnp.concatenate([gl] * (A_DK // 8), axis=0) + _dot_tn(kd[rows], vn)
        s_ref[h] = state
        o = jnp.concatenate(o_inter, axis=0) + _dot(att, jnp.concatenate(v_new, axis=0))
        on = o * lax.rsqrt(jnp.mean(o * o, axis=-1, keepdims=True) + EPS) * onw
        z = az_ref[:, h * A_DV:(h + 1) * A_DV]
        o_ref[:, h * A_DV:(h + 1) * A_DV] = on * _silu(z)


def _gdn(qkv, small, small_t, a_z, conv_w, a_log, dt_bias, out_norm_w, batch, seq, bt):
    nt = seq // bt
    lane_pad = lambda v: jnp.zeros((1, LANES), F32).at[0, SM_DECAY:SM_DECAY + A_HEADS].set(v)
    sub_pad = lambda v: jnp.zeros((8, 1), F32).at[SM_DECAY:SM_DECAY + A_HEADS, 0].set(v)
    cw = jnp.zeros((8, A_QKV), F32).at[:A_CONV].set(conv_w)
    onw = out_norm_w[None, :]
    row = lambda w: pl.BlockSpec((bt, w), lambda b, t: (b * nt + t, 0))
    full = lambda a: pl.BlockSpec(a.shape, lambda b, t: (0, 0))
    consts = (cw, lane_pad(a_log), lane_pad(dt_bias), sub_pad(a_log), sub_pad(dt_bias), onw)
    return pl.pallas_call(
        functools.partial(_gdn_kernel, bt=bt),
        grid=(batch, nt),
        in_specs=[row(A_QKV), row(SMALL_W), pl.BlockSpec((1, 8, bt), lambda b, t: (b, 0, t)), row(A_WIDTH)]
        + [full(a) for a in consts],
        out_specs=row(A_WIDTH),
        out_shape=jax.ShapeDtypeStruct((batch * seq, A_WIDTH), F32),
        scratch_shapes=[pltpu.VMEM((A_HEADS, A_DK, A_DV), F32), pltpu.VMEM((8, A_QKV), F32)],
        compiler_params=pltpu.CompilerParams(dimension_semantics=("arbitrary", "arbitrary"),
                                             vmem_limit_bytes=VMEM_LIMIT),
        name="gdn",
    )(qkv, small, small_t, a_z, *consts)


def _dsa_kernel(q_ref, kx_ref, vxt_ref, iq_ref, ik_ref, smt_ref, bz_ref, ltri_ref, o_ref,
                sc_ref, hb_ref, mx_ref, acc_ref, *, qb, ks, n_sel):
    rep = B_HEADS // B_KV_HEADS
    i = pl.program_id(1)
    ns = (i * qb) // ks + 1
    kidx = lax.broadcasted_iota(I32, (ks, qb), 0)
    qpos = i * qb + lax.broadcasted_iota(I32, (ks, qb), 1)
    kidx_t = lax.broadcasted_iota(I32, (LANES, qb), 0)
    qpos_t = i * qb + lax.broadcasted_iota(I32, (LANES, qb), 1)

    iq = iq_ref[...]
    iqs = jnp.concatenate([iq[:, h * IDX_DIM:(h + 1) * IDX_DIM] for h in range(IDX_HEADS)], axis=0)
    iwt = smt_ref[0][SM_IW:SM_IW + IDX_HEADS, :] * (IDX_HEADS ** -0.5 * IDX_DIM ** -0.5)

    def score_body(j, carry):
        for u in range(ks // LANES):
            start = pl.multiple_of(j * ks + u * LANES, LANES)
            lg = _dot_nt(ik_ref[pl.ds(start, LANES), :], iqs)
            sc = jnp.zeros((LANES, qb), F32)
            for h in range(IDX_HEADS):
                sc = sc + iwt[h:h + 1, :] * jnp.maximum(lg[:, h * qb:(h + 1) * qb], 0.0)
            rows = slice(u * LANES, (u + 1) * LANES)
            sc = jnp.where(start + kidx_t <= qpos_t, sc, -jnp.inf)
            sc_ref[j, rows, :] = sc
            hb_ref[j, rows, :] = sc.astype(BF16)
        return carry

    lax.fori_loop(0, ns, score_body, 0)

    def count(src_ref, chunk, hit_of, dtype):
        def body(j, acc):
            for u in range(ks // chunk):
                acc = acc + hit_of(src_ref[j, u * chunk:(u + 1) * chunk, :])
            return acc

        acc = lax.fori_loop(0, ns, body, jnp.zeros((chunk, qb), dtype))
        return jnp.sum(acc.astype(F32), axis=0, keepdims=True)

    def key_value(key):
        val = pltpu.bitcast(key ^ ((key >> 31) & 0x7FFFFFFF), F32)
        return jnp.where(key < KEY_MIN_FINITE, -jnp.inf, val)

    def count_ge(key):
        cand = jnp.broadcast_to(key_value(key), (8, qb))
        return count(sc_ref, 8, lambda sc: jnp.where(sc >= cand, 1.0, 0.0), F32)

    one_b = jnp.ones((16, qb), BF16)
    zero_b = jnp.zeros((16, qb), BF16)

    def coarse_value(key):
        bits = (key ^ ((key >> 31) & 0x7FFFFFFF)) & (-(1 << 16))
        return jnp.where(key < KEY_MIN_FINITE, -jnp.inf, pltpu.bitcast(bits, F32))

    def count_coarse(hit_of):
        return count(hb_ref, 16, lambda hb: jnp.where(hit_of(hb), one_b, zero_b), BF16)

    def coarse_body(bi, tau):
        key = tau ^ lax.shift_left(jnp.int32(1), 31 - bi)
        cand = jnp.broadcast_to(coarse_value(key), (16, qb)).astype(BF16)
        return jnp.where(count_coarse(lambda hb: hb >= cand) >= n_sel, key, tau)

    coarse = lax.fori_loop(0, 16, coarse_body, jnp.full((1, qb), INT_MIN, I32))
    step = 1 << 16
    centre = jnp.where(coarse < 0, coarse | (step - 1), coarse)
    lo0 = jnp.maximum(centre, INT_MIN + 2 * step) - 2 * step
    hi0 = centre + step
    at_zero = ((count_coarse(lambda hb: hb > zero_b) < n_sel) & (count_coarse(lambda hb: hb >= zero_b) >= n_sel))
    lo0 = jnp.where(at_zero, 0, lo0)
    hi0 = jnp.where(at_zero, 1, hi0)

    def fine_cond(state):
        it, _, _, _, open_rows = state
        return jnp.logical_and(it < 20, open_rows > 0.0)

    def fine_body(state):
        it, lo, hi, c_lo, _ = state
        settled = (c_lo == n_sel) | (hi - lo <= 1)
        open_rows = jnp.max(jnp.where(settled, 0.0, 1.0))
        mid = lo + ((hi - lo) >> 1)
        cnt = count_ge(mid)
        ok = cnt >= n_sel
        return (it + 1, jnp.where(ok, mid, lo), jnp.where(ok, hi, mid), jnp.where(ok, cnt, c_lo), open_rows)

    _, tau, _, _, _ = lax.while_loop(
        fine_cond, fine_body, (jnp.int32(0), lo0, hi0, jnp.full((1, qb), -1.0, F32), jnp.float32(1.0)))
    tau_val = key_value(tau)
    tau8 = jnp.broadcast_to(tau_val, (8, qb))
    need = n_sel - count(sc_ref, 8, lambda sc: jnp.where(sc > tau8, 1.0, 0.0), F32)

    q = q_ref[...]
    qs = [jnp.concatenate([q[:, (g * rep + r) * LANES:(g * rep + r + 1) * LANES] for r in range(rep)], axis=0)
          for g in range(B_KV_HEADS)]
    mx_ref[...] = jnp.full(mx_ref.shape, NEG_BIG, F32)
    acc_ref[...] = jnp.zeros(acc_ref.shape, F32)
    ltri = ltri_ref[...]
    taub = jnp.broadcast_to(tau_val, (ks, qb))

    def att_body(j, eq_seen):
        start = pl.multiple_of(j * ks, ks)
        sct = sc_ref[j]
        eqf = jnp.where(sct == taub, 1.0, 0.0)
        pref = _dot(ltri, eqf.astype(BF16)) + eq_seen
        take = jnp.where(sct > taub, 1.0, jnp.where(pref <= need, eqf, 0.0))
        bias = jnp.where((take > 0.0) & (j * ks + kidx <= qpos), 0.0, 2.0 * NEG_BIG)
        bias4 = jnp.concatenate([bias] * rep, axis=1)
        for g in range(B_KV_HEADS):
            s = _dot_nt(kx_ref[g, pl.ds(start, ks), :], qs[g]) + bias4
            m_old = mx_ref[g]
            m_new = jnp.maximum(m_old, jnp.max(s, axis=0, keepdims=True))
            p = jnp.exp(s - m_new[0:1, :]).astype(BF16)
            alpha = jnp.exp(m_old - m_new)[0:1, :]
            acc_ref[g] = alpha * acc_ref[g] + _dot(vxt_ref[g, :, pl.ds(start, ks)], p)
            mx_ref[g] = m_new
        return eq_seen + jnp.sum(eqf, axis=0, keepdims=True)

    lax.fori_loop(0, ns, att_body, jnp.zeros((1, qb), F32))

    lane = lax.broadcasted_iota(I32, (qb, LANES), 1)
    outs = []
    for g in range(B_KV_HEADS):
        acc = acc_ref[g]
        a = jnp.concatenate([jnp.transpose(acc[:, r * qb:(r + 1) * qb]) for r in range(rep)], axis=0)
        outs.append(a / a[:, B_HD:B_HD + 1])
    for cblk in range(B_WIDTH // LANES):
        g = (2 * cblk) // rep
        r0 = (2 * cblk) % rep
        a = outs[g][r0 * qb:(r0 + 1) * qb]
        b = pltpu.roll(outs[g][(r0 + 1) * qb:(r0 + 2) * qb], B_HD, axis=1)
        z = bz_ref[:, cblk * LANES:(cblk + 1) * LANES]
        o_ref[:, cblk * LANES:(cblk + 1) * LANES] = jnp.where(lane < B_HD, a, b) * _silu(z)


def _dsa(bqx, kx, vxt, iq, ik, small_t, bz, batch, seq, qb, ks):
    nq = seq // qb
    n_sel = min(TOPK_MAX, seq // 4)
    rep = B_HEADS // B_KV_HEADS
    ltri = (jnp.arange(ks)[None, :] <= jnp.arange(ks)[:, None]).astype(BF16)
    row = lambda w: pl.BlockSpec((qb, w), lambda b, i: (b * nq + i, 0))
    per_batch = lambda w: pl.BlockSpec((seq, w), lambda b, i: (b, 0))
    return pl.pallas_call(
        functools.partial(_dsa_kernel, qb=qb, ks=ks, n_sel=n_sel),
        grid=(batch, nq),
        in_specs=[row(B_HEADS * LANES),
                  pl.BlockSpec((B_KV_HEADS, seq, LANES), lambda b, i: (0, b, 0)),
                  pl.BlockSpec((B_KV_HEADS, LANES, seq), lambda b, i: (0, 0, b)),
                  row(IDX_WIDTH), per_batch(IDX_DIM),
                  pl.BlockSpec((1, 16, qb), lambda b, i: (b, 0, i)),
                  row(B_WIDTH), pl.BlockSpec((ks, ks), lambda b, i: (0, 0))],
        out_specs=row(B_WIDTH),
        out_shape=jax.ShapeDtypeStruct((batch * seq, B_WIDTH), F32),
        scratch_shapes=[pltpu.VMEM((seq // ks, ks, qb), F32),
                        pltpu.VMEM((seq // ks, ks, qb), BF16),
                        pltpu.VMEM((B_KV_HEADS, 8, rep * qb), F32),
                        pltpu.VMEM((B_KV_HEADS, LANES, rep * qb), F32)],
        compiler_params=pltpu.CompilerParams(dimension_semantics=("arbitrary", "arbitrary"),
                                             vmem_limit_bytes=VMEM_LIMIT),
        name="dsa",
    )(bqx, kx, vxt, iq, ik, small_t, bz, ltri)


def _out_kernel(x_ref, oa_ref, ob_ref, p_ref, woa_ref, wob_ref, wp_ref, wg_ref, gn_ref, bg_ref, o_ref):
    x1 = (x_ref[...] + _dot(oa_ref[...].astype(BF16), woa_ref[...])
          + _dot(ob_ref[...].astype(BF16), wob_ref[...]))
    hn = x1 * lax.rsqrt(jnp.mean(x1 * x1, axis=-1, keepdims=True) + EPS) * gn_ref[...]
    gate = jax.nn.sigmoid(_dot(hn.astype(BF16), wg_ref[...]) + bg_ref[...])
    o_ref[...] = x1 + _dot(p_ref[...].astype(BF16), wp_ref[...]) * gate


def _output(x2, oa, ob, p2, w_out, w_ple, gate_norm_w, w_gate, b_gate, tm):
    t, d = x2.shape
    woa = w_out[:A_WIDTH].astype(BF16)
    wob = w_out[A_WIDTH:].astype(BF16)
    wp = w_ple.astype(BF16)
    wg = w_gate.astype(BF16)
    gn = gate_norm_w[None, :]
    bg = b_gate[None, :]
    row = lambda w: pl.BlockSpec((tm, w), lambda i: (i, 0))
    full = lambda a: pl.BlockSpec(a.shape, lambda i: (0, 0))
    return pl.pallas_call(
        _out_kernel,
        grid=(t // tm,),
        in_specs=[row(d), row(A_WIDTH), row(B_WIDTH), row(PLE_DIM)] + [full(a) for a in (woa, wob, wp, wg, gn, bg)],
        out_specs=row(d),
        out_shape=jax.ShapeDtypeStruct((t, d), F32),
        compiler_params=pltpu.CompilerParams(dimension_semantics=("arbitrary",), vmem_limit_bytes=VMEM_LIMIT),
        name="out",
    )(x2, oa, ob, p2, woa, wob, wp, wg, gn, bg)


def kernel(x, p, attn_norm_w, w_in, conv_w, a_log, dt_bias, a_out_norm_w, b_q_norm_w, b_k_norm_w, w_out,
           w_ple, ple_gate_norm_w, w_ple_gate, b_ple_gate):
    batch, seq, d = x.shape
    t = batch * seq
    tm = min(512, t)
    bt = min(256, seq)
    qb = min(256, seq)
    ks = min(512, seq)
    x2 = x.reshape(t, d)
    for i in range(w_in.shape[0]):
        qkv, az, bqx, kx, vx, bz, iq, ik, small = _project(x2, attn_norm_w[i], w_in[i], b_q_norm_w[i],
                                                           b_k_norm_w[i], tm)
        small_t = jnp.swapaxes(small.reshape(batch, seq, SMALL_W)[:, :, :16], 1, 2)
        oa = _gdn(qkv, small, small_t, az, conv_w[i], a_log[i], dt_bias[i], a_out_norm_w[i], batch, seq, bt)
        ob = _dsa(bqx, kx, jnp.swapaxes(vx, 1, 2), iq, ik, small_t, bz, batch, seq, qb, ks)
        x2 = _output(x2, oa, ob, p[i].reshape(t, PLE_DIM), w_out[i], w_ple[i], ple_gate_norm_w[i],
                     w_ple_gate[i], b_ple_gate[i], tm)
    return x2.reshape(batch, seq, d)
```

```python
import functools

import jax
import jax.numpy as jnp
from jax import lax
from jax.experimental import pallas as pl
from jax.experimental.pallas import tpu as pltpu

F32 = jnp.float32
BF16 = jnp.bfloat16
I32 = jnp.int32
EPS = 1e-6
HI = lax.Precision.HIGHEST

PLE_DIM = 256
A_HEADS = 4
A_DK = 128
A_DV = 128
A_CONV = 4
A_CHUNK = 64
A_WIDTH = A_HEADS * A_DV
A_QKV = 2 * A_HEADS * A_DK + A_WIDTH
B_HEADS = 8
B_KV_HEADS = 2
B_HD = 64
B_WIDTH = B_HEADS * B_HD
B_KV_WIDTH = B_KV_HEADS * B_HD
IDX_HEADS = 8
IDX_DIM = 128
IDX_WIDTH = IDX_HEADS * IDX_DIM
TOPK_MAX = 256
LANES = 128
SMALL_W = LANES
SM_BETA = 0
SM_DECAY = A_HEADS
SM_IW = 2 * A_HEADS

VMEM_LIMIT = 56 * 1024 * 1024
NEG_BIG = -1e30
INT_MIN = -(2 ** 31)
KEY_MIN_FINITE = INT_MIN + 0x00800000


def _dot(a, b, prec=None):
    return jnp.dot(a, b, preferred_element_type=F32, precision=prec)


def _dot_nt(a, b, prec=None):
    return lax.dot_general(a, b, (((1,), (1,)), ((), ())), preferred_element_type=F32, precision=prec)


def _dot_tn(a, b, prec=None):
    return lax.dot_general(a, b, (((0,), (0,)), ((), ())), preferred_element_type=F32, precision=prec)


def _silu(x):
    return x * jax.nn.sigmoid(x)


def _softplus(x):
    return jnp.maximum(x, 0.0) + jnp.log1p(jnp.exp(-jnp.abs(x)))


def _seg_norm64(xb, gain_row):
    lane = lax.broadcasted_iota(I32, xb.shape, 1)
    lo = lane < B_HD
    sq = xb * xb
    s_lo = jnp.sum(jnp.where(lo, sq, 0.0), axis=-1, keepdims=True)
    s_hi = jnp.sum(jnp.where(lo, 0.0, sq), axis=-1, keepdims=True)
    ms = jnp.where(lo, s_lo, s_hi) * (1.0 / B_HD)
    return xb * lax.rsqrt(ms + EPS) * gain_row


_WB_QKV = 0
_WB_AZ = _WB_QKV + A_QKV
_WB_BQ = _WB_AZ + A_WIDTH
_WB_BK = _WB_BQ + B_WIDTH
_WB_BV = _WB_BK + B_KV_WIDTH
_WB_BZ = _WB_BV + B_KV_WIDTH
_WB_IQ = _WB_BZ + B_WIDTH
_WB_IK = _WB_IQ + IDX_WIDTH
_WB_END = _WB_IK + IDX_DIM


def _proj_kernel(x_ref, nw_ref, wb_ref, wf_ref, qg_ref, kg_ref,
                 qkv_ref, az_ref, bqx_ref, kx_ref, vx_ref, bz_ref, iq_ref, ik_ref, sm_ref):
    x = x_ref[...]
    h = x * lax.rsqrt(jnp.mean(x * x, axis=-1, keepdims=True) + EPS) * nw_ref[...]
    hb = h.astype(BF16)
    step = 512
    for c0 in range(0, A_QKV, step):
        qkv_ref[:, c0:c0 + step] = _dot(hb, wb_ref[:, _WB_QKV + c0:_WB_QKV + c0 + step])
    az_ref[...] = _dot(hb, wb_ref[:, _WB_AZ:_WB_BQ])
    bz_ref[...] = _dot(hb, wb_ref[:, _WB_BZ:_WB_IQ])
    lane = lax.broadcasted_iota(I32, (x.shape[0], LANES), 1)
    lo = lane < B_HD
    bk = _seg_norm64(_dot(hb, wb_ref[:, _WB_BK:_WB_BV]), kg_ref[...])
    bv = _dot(hb, wb_ref[:, _WB_BV:_WB_BZ])
    k_tail = jnp.where(lane == B_HD, 1.0, 0.0)
    for g in range(B_KV_HEADS):
        kg = bk if g == 0 else pltpu.roll(bk, B_HD, axis=1)
        vg = bv if g == 0 else pltpu.roll(bv, B_HD, axis=1)
        kx_ref[g] = jnp.where(lo, kg, k_tail).astype(BF16)
        vx_ref[g] = jnp.where(lo, vg, 1.0).astype(BF16)
    bq = _dot(hb, wb_ref[:, _WB_BQ:_WB_BK])
    scale = B_HD ** -0.5
    for c0 in range(0, B_WIDTH, LANES):
        nb = _seg_norm64(bq[:, c0:c0 + LANES], qg_ref[...]) * scale
        bqx_ref[:, 2 * c0:2 * c0 + LANES] = jnp.where(lo, nb, 0.0).astype(BF16)
        bqx_ref[:, 2 * c0 + LANES:2 * c0 + 2 * LANES] = jnp.where(
            lo, pltpu.roll(nb, B_HD, axis=1), 0.0).astype(BF16)
    for c0 in range(0, IDX_WIDTH, step):
        iq_ref[:, c0:c0 + step] = _dot(hb, wb_ref[:, _WB_IQ + c0:_WB_IQ + c0 + step]).astype(BF16)
    ik_ref[...] = _dot(hb, wb_ref[:, _WB_IK:_WB_END]).astype(BF16)
    sm_ref[...] = _dot(h, wf_ref[...], HI)


def _project(x2, norm_w, w_in, q_gain, k_gain, tm):
    t, d = x2.shape
    sizes = (A_HEADS * A_DK, A_HEADS * A_DK, A_WIDTH, A_WIDTH, A_HEADS, A_HEADS, B_WIDTH, B_KV_WIDTH,
             B_KV_WIDTH, B_WIDTH, IDX_WIDTH, IDX_DIM, IDX_HEADS)
    offs = [0]
    for s in sizes:
        offs.append(offs[-1] + s)
    (a_q, a_k, a_v, a_z, a_b, a_a, b_q, b_k, b_v, b_z, i_q, i_k, i_w) = [
        w_in[:, offs[n]:offs[n + 1]] for n in range(len(sizes))]
    wb = jnp.concatenate([a_q, a_k, a_v, a_z, b_q, b_k, b_v, b_z, i_q, i_k], axis=1).astype(BF16)
    pad = jnp.zeros((d, SMALL_W - 2 * A_HEADS - IDX_HEADS), F32)
    wf = jnp.concatenate([a_b, a_a, i_w, pad], axis=1)
    qg = jnp.tile(q_gain, LANES // B_HD)[None, :]
    kg = jnp.tile(k_gain, LANES // B_HD)[None, :]
    row = lambda w: pl.BlockSpec((tm, w), lambda i: (i, 0))
    grp = pl.BlockSpec((B_KV_HEADS, tm, LANES), lambda i: (0, i, 0))
    full = lambda a: pl.BlockSpec(a.shape, lambda i: (0, 0))
    nw = norm_w[None, :]
    sds = jax.ShapeDtypeStruct
    return pl.pallas_call(
        _proj_kernel,
        grid=(t // tm,),
        in_specs=[row(d), full(nw), full(wb), full(wf), full(qg), full(kg)],
        out_specs=[row(A_QKV), row(A_WIDTH), row(B_HEADS * LANES), grp, grp, row(B_WIDTH), row(IDX_WIDTH),
                   row(IDX_DIM), row(SMALL_W)],
        out_shape=[sds((t, A_QKV), F32), sds((t, A_WIDTH), F32), sds((t, B_HEADS * LANES), BF16),
                   sds((B_KV_HEADS, t, LANES), BF16), sds((B_KV_HEADS, t, LANES), BF16), sds((t, B_WIDTH), F32),
                   sds((t, IDX_WIDTH), BF16), sds((t, IDX_DIM), BF16), sds((t, SMALL_W), F32)],
        compiler_params=pltpu.CompilerParams(dimension_semantics=("arbitrary",), vmem_limit_bytes=VMEM_LIMIT),
        name="proj",
    )(x2, nw, wb, wf, qg, kg)


def _gdn_kernel(qkv_ref, sm_ref, smt_ref, az_ref, cw_ref, alog_r_ref, dtb_r_ref, alog_c_ref, dtb_c_ref,
                onw_ref, o_ref, s_ref, tail_ref, *, bt):
    c = A_CHUNK

    @pl.when(pl.program_id(1) == 0)
    def _():
        s_ref[...] = jnp.zeros(s_ref.shape, F32)
        tail_ref[...] = jnp.zeros(tail_ref.shape, F32)

    xin = qkv_ref[...]
    tail = tail_ref[...]
    cw = cw_ref[...]
    row8 = lax.broadcasted_iota(I32, tail.shape, 0)
    acc = xin * cw[A_CONV - 1:A_CONV, :]
    for d in range(1, A_CONV):
        xr = pltpu.roll(xin, d, axis=0)
        pr = pltpu.roll(tail, d, axis=0)
        head = jnp.where(row8 < d, pr, xr[0:8])
        xs = jnp.concatenate([head, xr[8:]], axis=0)
        acc = acc + xs * cw[A_CONV - 1 - d:A_CONV - d, :]
    tail_ref[...] = xin[bt - 8:bt]
    qkv = _silu(acc)

    sm = sm_ref[...]
    smt = smt_ref[0]
    g_col = -jnp.exp(alog_r_ref[...]) * _softplus(sm + dtb_r_ref[...])
    g_row = -jnp.exp(alog_c_ref[...]) * _softplus(smt + dtb_c_ref[...])
    beta_all = jax.nn.sigmoid(sm)

    r = lax.broadcasted_iota(I32, (bt, bt), 0)
    s = lax.broadcasted_iota(I32, (bt, bt), 1)
    same = (r // c) == (s // c)
    tril = same & (s <= r)
    strict = same & (s < r)
    eye = jnp.where(s == r, 1.0, 0.0).astype(F32)
    gc_col = _dot(jnp.where(tril, 1.0, 0.0).astype(F32), g_col, HI)
    gc_row = _dot(g_row, jnp.where(same & (r <= s), 1.0, 0.0).astype(F32), HI)
    gtot = _dot(jnp.where(same, 1.0, 0.0).astype(F32), g_col, HI)
    egc_all = jnp.exp(gc_col)
    ekd_all = jnp.exp(gtot - gc_col)
    egl_all = jnp.exp(gtot)
    onw = onw_ref[...]

    heads = range(A_HEADS)
    qs, ks_, kbs, decays, amats = [], [], [], [], []
    for h in heads:
        q = qkv[:, h * A_DK:(h + 1) * A_DK]
        k = qkv[:, A_HEADS * A_DK + h * A_DK:A_HEADS * A_DK + (h + 1) * A_DK]
        q = q * lax.rsqrt(jnp.sum(q * q, axis=-1, keepdims=True) + EPS) * (A_DK ** -0.5)
        k = k * lax.rsqrt(jnp.sum(k * k, axis=-1, keepdims=True) + EPS)
        lane = SM_DECAY + h
        gcc = gc_col[:, lane:lane + 1]
        gcr = gc_row[lane:lane + 1, :]
        decay = jnp.where(tril, jnp.exp(jnp.where(tril, gcc - gcr, 0.0)), 0.0)
        kb = k * beta_all[:, SM_BETA + h:SM_BETA + h + 1]
        qs.append(q)
        ks_.append(k)
        kbs.append(kb)
        decays.append(decay)
        amats.append(jnp.where(strict, _dot_nt(kb.astype(BF16), k.astype(BF16)) * decay, 0.0))
    tmats = [eye - a for a in amats]
    apows = list(amats)
    n = 2
    while n < c:
        for h in heads:
            a_bf = apows[h].astype(BF16)
            apows[h] = _dot(a_bf, a_bf)
        for h in heads:
            tmats[h] = tmats[h] + _dot(tmats[h].astype(BF16), apows[h].astype(BF16))
        n *= 2
    values, kcds, atts, qds, kds, egls = [], [], [], [], [], []
    for h in heads:
        lane = SM_DECAY + h
        v = qkv[:, 2 * A_HEADS * A_DK + h * A_DV:2 * A_HEADS * A_DK + (h + 1) * A_DV]
        t_bf = tmats[h].astype(BF16)
        egc = egc_all[:, lane:lane + 1]
        values.append(_dot(t_bf, (v * beta_all[:, SM_BETA + h:SM_BETA + h + 1]).astype(BF16)))
        kcds.append(_dot(t_bf, (kbs[h] * egc).astype(BF16)).astype(BF16))
        atts.append((_dot_nt(qs[h].astype(BF16), ks_[h].astype(BF16)) * decays[h]).astype(BF16))
        qds.append((qs[h] * egc).astype(BF16))
        kds.append((ks_[h] * ekd_all[:, lane:lane + 1]).astype(BF16))
        egls.append(egl_all[:, lane:lane + 1])
    states = [s_ref[h] for h in heads]
    o_inter = [[] for _ in heads]
    v_new = [[] for _ in heads]
    for ci in range(bt // c):
        rows = slice(ci * c, (ci + 1) * c)
        for h in heads:
            s_bf = states[h].astype(BF16)
            vn = (values[h][rows] - _dot(kcds[h][rows], s_bf)).astype(BF16)
            o_inter[h].append(_dot(qds[h][rows], s_bf))
            v_new[h].append(vn)
            gl = egls[h][ci * c:ci * c + 8]
            states[h] = states[h] * jnp.concatenate([gl] * (A_DK // 8), axis=0) + _dot_tn(kds[h][rows], vn)
    for h in heads:
        s_ref[h] = states[h]
        o = jnp.concatenate(o_inter[h], axis=0) + _dot(atts[h], jnp.concatenate(v_new[h], axis=0))
        on = o * lax.rsqrt(jnp.mean(o * o, axis=-1, keepdims=True) + EPS) * onw
        z = az_ref[:, h * A_DV:(h + 1) * A_DV]
        o_ref[:, h * A_DV:(h + 1) * A_DV] = on * _silu(z)


def _gdn(qkv, small, small_t, a_z, conv_w, a_log, dt_bias, out_norm_w, batch, seq, bt):
    nt = seq // bt
    lane_pad = lambda v: jnp.zeros((1, LANES), F32).at[0, SM_DECAY:SM_DECAY + A_HEADS].set(v)
    sub_pad = lambda v: jnp.zeros((8, 1), F32).at[SM_DECAY:SM_DECAY + A_HEADS, 0].set(v)
    cw = jnp.zeros((8, A_QKV), F32).at[:A_CONV].set(conv_w)
    onw = out_norm_w[None, :]
    row = lambda w: pl.BlockSpec((bt, w), lambda b, t: (b * nt + t, 0))
    full = lambda a: pl.BlockSpec(a.shape, lambda b, t: (0, 0))
    consts = (cw, lane_pad(a_log), lane_pad(dt_bias), sub_pad(a_log), sub_pad(dt_bias), onw)
    return pl.pallas_call(
        functools.partial(_gdn_kernel, bt=bt),
        grid=(batch, nt),
        in_specs=[row(A_QKV), row(SMALL_W), pl.BlockSpec((1, 8, bt), lambda b, t: (b, 0, t)), row(A_WIDTH)]
        + [full(a) for a in consts],
        out_specs=row(A_WIDTH),
        out_shape=jax.ShapeDtypeStruct((batch * seq, A_WIDTH), F32),
        scratch_shapes=[pltpu.VMEM((A_HEADS, A_DK, A_DV), F32), pltpu.VMEM((8, A_QKV), F32)],
        compiler_params=pltpu.CompilerParams(dimension_semantics=("arbitrary", "arbitrary"),
                                             vmem_limit_bytes=VMEM_LIMIT),
        name="gdn",
    )(qkv, small, small_t, a_z, *consts)


def _dsa_kernel(q_ref, kx_ref, vxt_ref, iq_ref, ik_ref, smt_ref, bz_ref, ltri_ref, o_ref,
                sc_ref, hb_ref, mx_ref, acc_ref, *, qb, ks, n_sel):
    rep = B_HEADS // B_KV_HEADS
    i = pl.program_id(1)
    ns = (i * qb) // ks + 1
    kidx = lax.broadcasted_iota(I32, (ks, qb), 0)
    qpos = i * qb + lax.broadcasted_iota(I32, (ks, qb), 1)
    kidx_t = lax.broadcasted_iota(I32, (LANES, qb), 0)
    qpos_t = i * qb + lax.broadcasted_iota(I32, (LANES, qb), 1)

    iq = iq_ref[...]
    iqs = jnp.concatenate([iq[:, h * IDX_DIM:(h + 1) * IDX_DIM] for h in range(IDX_HEADS)], axis=0)
    iwt = smt_ref[0][SM_IW:SM_IW + IDX_HEADS, :] * (IDX_HEADS ** -0.5 * IDX_DIM ** -0.5)

    def score_body(j, carry):
        for u in range(ks // LANES):
            start = pl.multiple_of(j * ks + u * LANES, LANES)
            lg = _dot_nt(ik_ref[pl.ds(start, LANES), :], iqs)
            sc = jnp.zeros((LANES, qb), F32)
            for h in range(IDX_HEADS):
                sc = sc + iwt[h:h + 1, :] * jnp.maximum(lg[:, h * qb:(h + 1) * qb], 0.0)
            rows = slice(u * LANES, (u + 1) * LANES)
            sc = jnp.where(start + kidx_t <= qpos_t, sc, -jnp.inf)
            sc_ref[j, rows, :] = sc
            hb_ref[j, rows, :] = sc.astype(BF16)
        return carry

    lax.fori_loop(0, ns, score_body, 0)

    def count(src_ref, chunk, hit_of, dtype):
        def body(j, acc):
            for u in range(ks // chunk):
                acc = acc + hit_of(src_ref[j, u * chunk:(u + 1) * chunk, :])
            return acc

        acc = lax.fori_loop(0, ns, body, jnp.zeros((chunk, qb), dtype))
        return jnp.sum(acc.astype(F32), axis=0, keepdims=True)

    def key_value(key):
        val = pltpu.bitcast(key ^ ((key >> 31) & 0x7FFFFFFF), F32)
        return jnp.where(key < KEY_MIN_FINITE, -jnp.inf, val)

    def count_ge(key):
        cand = jnp.broadcast_to(key_value(key), (8, qb))
        return count(sc_ref, 8, lambda sc: jnp.where(sc >= cand, 1.0, 0.0), F32)

    one_b = jnp.ones((16, qb), BF16)
    zero_b = jnp.zeros((16, qb), BF16)

    def coarse_value(key):
        bits = (key ^ ((key >> 31) & 0x7FFFFFFF)) & (-(1 << 16))
        return jnp.where(key < KEY_MIN_FINITE, -jnp.inf, pltpu.bitcast(bits, F32))

    def count_coarse(hit_of):
        return count(hb_ref, 16, lambda hb: jnp.where(hit_of(hb), one_b, zero_b), BF16)

    def coarse_body(bi, tau):
        key = tau ^ lax.shift_left(jnp.int32(1), 31 - bi)
        cand = jnp.broadcast_to(coarse_value(key), (16, qb)).astype(BF16)
        return jnp.where(count_coarse(lambda hb: hb >= cand) >= n_sel, key, tau)

    coarse = lax.fori_loop(0, 16, coarse_body, jnp.full((1, qb), INT_MIN, I32))
    step = 1 << 16
    centre = jnp.where(coarse < 0, coarse | (step - 1), coarse)
    lo0 = jnp.maximum(centre, INT_MIN + 2 * step) - 2 * step
    hi0 = centre + step
    at_zero = ((count_coarse(lambda hb: hb > zero_b) < n_sel) & (count_coarse(lambda hb: hb >= zero_b) >= n_sel))
    lo0 = jnp.where(at_zero, 0, lo0)
    hi0 = jnp.where(at_zero, 1, hi0)

    def fine_cond(state):
        it, _, _, _, open_rows = state
        return jnp.logical_and(it < 20, open_rows > 0.0)

    def fine_body(state):
        it, lo, hi, c_lo, _ = state
        settled = (c_lo == n_sel) | (hi - lo <= 1)
        open_rows = jnp.max(jnp.where(settled, 0.0, 1.0))
        mid = lo + ((hi - lo) >> 1)
        cnt = count_ge(mid)
        ok = cnt >= n_sel
        return (it + 1, jnp.where(ok, mid, lo), jnp.where(ok, hi, mid), jnp.where(ok, cnt, c_lo), open_rows)

    _, tau, _, _, _ = lax.while_loop(
        fine_cond, fine_body, (jnp.int32(0), lo0, hi0, jnp.full((1, qb), -1.0, F32), jnp.float32(1.0)))
    tau_val = key_value(tau)
    tau8 = jnp.broadcast_to(tau_val, (8, qb))
    need = n_sel - count(sc_ref, 8, lambda sc: jnp.where(sc > tau8, 1.0, 0.0), F32)

    q = q_ref[...]
    qs = [jnp.concatenate([q[:, (g * rep + r) * LANES:(g * rep + r + 1) * LANES] for r in range(rep)], axis=0)
          for g in range(B_KV_HEADS)]
    mx_ref[...] = jnp.full(mx_ref.shape, NEG_BIG, F32)
    acc_ref[...] = jnp.zeros(acc_ref.shape, F32)
    ltri = ltri_ref[...]
    taub = jnp.broadcast_to(tau_val, (ks, qb))

    def att_body(j, eq_seen):
        start = pl.multiple_of(j * ks, ks)
        sct = sc_ref[j]
        eqf = jnp.where(sct == taub, 1.0, 0.0)
        pref = _dot(ltri, eqf.astype(BF16)) + eq_seen
        take = jnp.where(sct > taub, 1.0, jnp.where(pref <= need, eqf, 0.0))
        bias = jnp.where((take > 0.0) & (j * ks + kidx <= qpos), 0.0, 2.0 * NEG_BIG)
        bias4 = jnp.concatenate([bias] * rep, axis=1)
        for g in range(B_KV_HEADS):
            s = _dot_nt(kx_ref[g, pl.ds(start, ks), :], qs[g]) + bias4
            m_old = mx_ref[g]
            m_new = jnp.maximum(m_old, jnp.max(s, axis=0, keepdims=True))
            p = jnp.exp(s - m_new[0:1, :]).astype(BF16)
            alpha = jnp.exp(m_old - m_new)[0:1, :]
            acc_ref[g] = alpha * acc_ref[g] + _dot(vxt_ref[g, :, pl.ds(start, ks)], p)
            mx_ref[g] = m_new
        return eq_seen + jnp.sum(eqf, axis=0, keepdims=True)

    lax.fori_loop(0, ns, att_body, jnp.zeros((1, qb), F32))

    lane = lax.broadcasted_iota(I32, (qb, LANES), 1)
    outs = []
    for g in range(B_KV_HEADS):
        acc = acc_ref[g]
        a = jnp.concatenate([jnp.transpose(acc[:, r * qb:(r + 1) * qb]) for r in range(rep)], axis=0)
        outs.append(a / a[:, B_HD:B_HD + 1])
    for cblk in range(B_WIDTH // LANES):
        g = (2 * cblk) // rep
        r0 = (2 * cblk) % rep
        a = outs[g][r0 * qb:(r0 + 1) * qb]
        b = pltpu.roll(outs[g][(r0 + 1) * qb:(r0 + 2) * qb], B_HD, axis=1)
        z = bz_ref[:, cblk * LANES:(cblk + 1) * LANES]
        o_ref[:, cblk * LANES:(cblk + 1) * LANES] = jnp.where(lane < B_HD, a, b) * _silu(z)


def _dsa(bqx, kx, vxt, iq, ik, small_t, bz, batch, seq, qb, ks):
    nq = seq // qb
    n_sel = min(TOPK_MAX, seq // 4)
    rep = B_HEADS // B_KV_HEADS
    ltri = (jnp.arange(ks)[None, :] <= jnp.arange(ks)[:, None]).astype(BF16)
    row = lambda w: pl.BlockSpec((qb, w), lambda b, i: (b * nq + i, 0))
    per_batch = lambda w: pl.BlockSpec((seq, w), lambda b, i: (b, 0))
    return pl.pallas_call(
        functools.partial(_dsa_kernel, qb=qb, ks=ks, n_sel=n_sel),
        grid=(batch, nq),
        in_specs=[row(B_HEADS * LANES),
                  pl.BlockSpec((B_KV_HEADS, seq, LANES), lambda b, i: (0, b, 0)),
                  pl.BlockSpec((B_KV_HEADS, LANES, seq), lambda b, i: (0, 0, b)),
                  row(IDX_WIDTH), per_batch(IDX_DIM),
                  pl.BlockSpec((1, 16, qb), lambda b, i: (b, 0, i)),
                  row(B_WIDTH), pl.BlockSpec((ks, ks), lambda b, i: (0, 0))],
        out_specs=row(B_WIDTH),
        out_shape=jax.ShapeDtypeStruct((batch * seq, B_WIDTH), F32),
        scratch_shapes=[pltpu.VMEM((seq // ks, ks, qb), F32),
                        pltpu.VMEM((seq // ks, ks, qb), BF16),
                        pltpu.VMEM((B_KV_HEADS, 8, rep * qb), F32),
                        pltpu.VMEM((B_KV_HEADS, LANES, rep * qb), F32)],
        compiler_params=pltpu.CompilerParams(dimension_semantics=("arbitrary", "arbitrary"),
                                             vmem_limit_bytes=VMEM_LIMIT),
        name="dsa",
    )(bqx, kx, vxt, iq, ik, small_t, bz, ltri)


def _out_kernel(x_ref, oa_ref, ob_ref, p_ref, woa_ref, wob_ref, wp_ref, wg_ref, gn_ref, bg_ref, o_ref):
    x1 = (x_ref[...] + _dot(oa_ref[...].astype(BF16), woa_ref[...])
          + _dot(ob_ref[...].astype(BF16), wob_ref[...]))
    hn = x1 * lax.rsqrt(jnp.mean(x1 * x1, axis=-1, keepdims=True) + EPS) * gn_ref[...]
    gate = jax.nn.sigmoid(_dot(hn.astype(BF16), wg_ref[...]) + bg_ref[...])
    o_ref[...] = x1 + _dot(p_ref[...].astype(BF16), wp_ref[...]) * gate


def _output(x2, oa, ob, p2, w_out, w_ple, gate_norm_w, w_gate, b_gate, tm):
    t, d = x2.shape
    woa = w_out[:A_WIDTH].astype(BF16)
    wob = w_out[A_WIDTH:].astype(BF16)
    wp = w_ple.astype(BF16)
    wg = w_gate.astype(BF16)
    gn = gate_norm_w[None, :]
    bg = b_gate[None, :]
    row = lambda w: pl.BlockSpec((tm, w), lambda i: (i, 0))
    full = lambda a: pl.BlockSpec(a.shape, lambda i: (0, 0))
    return pl.pallas_call(
        _out_kernel,
        grid=(t // tm,),
        in_specs=[row(d), row(A_WIDTH), row(B_WIDTH), row(PLE_DIM)] + [full(a) for a in (woa, wob, wp, wg, gn, bg)],
        out_specs=row(d),
        out_shape=jax.ShapeDtypeStruct((t, d), F32),
        compiler_params=pltpu.CompilerParams(dimension_semantics=("arbitrary",), vmem_limit_bytes=VMEM_LIMIT),
        name="out",
    )(x2, oa, ob, p2, woa, wob, wp, wg, gn, bg)


def kernel(x, p, attn_norm_w, w_in, conv_w, a_log, dt_bias, a_out_norm_w, b_q_norm_w, b_k_norm_w, w_out,
           w_ple, ple_gate_norm_w, w_ple_gate, b_ple_gate):
    batch, seq, d = x.shape
    t = batch * seq
    tm = min(512, t)
    bt = min(256, seq)
    qb = min(256, seq)
    ks = min(512, seq)
    x2 = x.reshape(t, d)
    for i in range(w_in.shape[0]):
        qkv, az, bqx, kx, vx, bz, iq, ik, small = _project(x2, attn_norm_w[i], w_in[i], b_q_norm_w[i],
                                                           b_k_norm_w[i], tm)
        small_t = jnp.swapaxes(small.reshape(batch, seq, SMALL_W)[:, :, :16], 1, 2)
        oa = _gdn(qkv, small, small_t, az, conv_w[i], a_log[i], dt_bias[i], a_out_norm_w[i], batch, seq, bt)
        ob = _dsa(bqx, kx, jnp.swapaxes(vx, 1, 2), iq, ik, small_t, bz, batch, seq, qb, ks)
        x2 = _output(x2, oa, ob, p[i].reshape(t, PLE_DIM), w_out[i], w_ple[i], ple_gate_norm_w[i],
                     w_ple_gate[i], b_ple_gate[i], tm)
    return x2.reshape(batch, seq, d)
```

```python
import functools

import jax
import jax.numpy as jnp
from jax import lax
from jax.experimental import pallas as pl
from jax.experimental.pallas import tpu as pltpu

F32 = jnp.float32
BF16 = jnp.bfloat16
I32 = jnp.int32
EPS = 1e-6
HI = lax.Precision.HIGHEST

PLE_DIM = 256
A_HEADS = 4
A_DK = 128
A_DV = 128
A_CONV = 4
A_CHUNK = 64
A_WIDTH = A_HEADS * A_DV
A_QKV = 2 * A_HEADS * A_DK + A_WIDTH
B_HEADS = 8
B_KV_HEADS = 2
B_HD = 64
B_WIDTH = B_HEADS * B_HD
B_KV_WIDTH = B_KV_HEADS * B_HD
IDX_HEADS = 8
IDX_DIM = 128
IDX_WIDTH = IDX_HEADS * IDX_DIM
TOPK_MAX = 256
LANES = 128
SMALL_W = LANES
SM_BETA = 0
SM_DECAY = A_HEADS
SM_IW = 2 * A_HEADS

VMEM_LIMIT = 56 * 1024 * 1024
NEG_BIG = -1e30
LOG2_E = 1.4426950408889634
INT_MIN = -(2 ** 31)
KEY_MIN_FINITE = INT_MIN + 0x00800000


def _dot(a, b, prec=None):
    return jnp.dot(a, b, preferred_element_type=F32, precision=prec)


def _dot_nt(a, b, prec=None):
    return lax.dot_general(a, b, (((1,), (1,)), ((), ())), preferred_element_type=F32, precision=prec)


def _dot_tn(a, b, prec=None):
    return lax.dot_general(a, b, (((0,), (0,)), ((), ())), preferred_element_type=F32, precision=prec)


def _silu(x):
    return x * jax.nn.sigmoid(x)


def _softplus(x):
    return jnp.maximum(x, 0.0) + jnp.log1p(jnp.exp(-jnp.abs(x)))


def _seg_norm64(xb, gain_row):
    lane = lax.broadcasted_iota(I32, xb.shape, 1)
    lo = lane < B_HD
    sq = xb * xb
    s_lo = jnp.sum(jnp.where(lo, sq, 0.0), axis=-1, keepdims=True)
    s_hi = jnp.sum(jnp.where(lo, 0.0, sq), axis=-1, keepdims=True)
    ms = jnp.where(lo, s_lo, s_hi) * (1.0 / B_HD)
    return xb * lax.rsqrt(ms + EPS) * gain_row


_WB_QKV = 0
_WB_AZ = _WB_QKV + A_QKV
_WB_BQ = _WB_AZ + A_WIDTH
_WB_BK = _WB_BQ + B_WIDTH
_WB_BV = _WB_BK + B_KV_WIDTH
_WB_BZ = _WB_BV + B_KV_WIDTH
_WB_IQ = _WB_BZ + B_WIDTH
_WB_IK = _WB_IQ + IDX_WIDTH
_WB_SM = _WB_IK + IDX_DIM
_WB_END = _WB_SM + SMALL_W


def _proj_kernel(x_ref, nw_ref, wb_ref, qg_ref, kg_ref,
                 qkv_ref, az_ref, bqx_ref, kx_ref, vx_ref, bz_ref, iq_ref, ik_ref, sm_ref):
    x = x_ref[...]
    h = x * lax.rsqrt(jnp.mean(x * x, axis=-1, keepdims=True) + EPS) * nw_ref[...]
    hb = h.astype(BF16)
    step = 512
    for c0 in range(0, A_QKV, step):
        qkv_ref[:, c0:c0 + step] = _dot(hb, wb_ref[:, _WB_QKV + c0:_WB_QKV + c0 + step])
    az_ref[...] = _dot(hb, wb_ref[:, _WB_AZ:_WB_BQ])
    bz_ref[...] = _dot(hb, wb_ref[:, _WB_BZ:_WB_IQ])
    lane = lax.broadcasted_iota(I32, (x.shape[0], LANES), 1)
    lo = lane < B_HD
    bk = _seg_norm64(_dot(hb, wb_ref[:, _WB_BK:_WB_BV]), kg_ref[...])
    bv = _dot(hb, wb_ref[:, _WB_BV:_WB_BZ])
    k_tail = jnp.where(lane == B_HD, 1.0, 0.0)
    for g in range(B_KV_HEADS):
        kg = bk if g == 0 else pltpu.roll(bk, B_HD, axis=1)
        vg = bv if g == 0 else pltpu.roll(bv, B_HD, axis=1)
        kx_ref[g] = jnp.where(lo, kg, k_tail).astype(BF16)
        vx_ref[g] = jnp.where(lo, vg, 1.0).astype(BF16)
    bq = _dot(hb, wb_ref[:, _WB_BQ:_WB_BK])
    scale = B_HD ** -0.5 * LOG2_E
    for c0 in range(0, B_WIDTH, LANES):
        nb = _seg_norm64(bq[:, c0:c0 + LANES], qg_ref[...]) * scale
        bqx_ref[:, 2 * c0:2 * c0 + LANES] = jnp.where(lo, nb, 0.0).astype(BF16)
        bqx_ref[:, 2 * c0 + LANES:2 * c0 + 2 * LANES] = jnp.where(
            lo, pltpu.roll(nb, B_HD, axis=1), 0.0).astype(BF16)
    for c0 in range(0, IDX_WIDTH, step):
        iq_ref[:, c0:c0 + step] = _dot(hb, wb_ref[:, _WB_IQ + c0:_WB_IQ + c0 + step]).astype(BF16)
    ik_ref[...] = _dot(hb, wb_ref[:, _WB_IK:_WB_SM]).astype(BF16)
    sm_ref[...] = _dot(hb, wb_ref[:, _WB_SM:_WB_END])


def _project(x2, norm_w, w_in, q_gain, k_gain, tm):
    t, d = x2.shape
    sizes = (A_HEADS * A_DK, A_HEADS * A_DK, A_WIDTH, A_WIDTH, A_HEADS, A_HEADS, B_WIDTH, B_KV_WIDTH,
             B_KV_WIDTH, B_WIDTH, IDX_WIDTH, IDX_DIM, IDX_HEADS)
    offs = [0]
    for s in sizes:
        offs.append(offs[-1] + s)
    (a_q, a_k, a_v, a_z, a_b, a_a, b_q, b_k, b_v, b_z, i_q, i_k, i_w) = [
        w_in[:, offs[n]:offs[n + 1]] for n in range(len(sizes))]
    pad = jnp.zeros((d, SMALL_W - 2 * A_HEADS - IDX_HEADS), F32)
    wb = jnp.concatenate([a_q, a_k, a_v, a_z, b_q, b_k, b_v, b_z, i_q, i_k, a_b, a_a, i_w, pad],
                         axis=1).astype(BF16)
    qg = jnp.tile(q_gain, LANES // B_HD)[None, :]
    kg = jnp.tile(k_gain, LANES // B_HD)[None, :]
    row = lambda w: pl.BlockSpec((tm, w), lambda i: (i, 0))
    grp = pl.BlockSpec((B_KV_HEADS, tm, LANES), lambda i: (0, i, 0))
    full = lambda a: pl.BlockSpec(a.shape, lambda i: (0, 0))
    nw = norm_w[None, :]
    sds = jax.ShapeDtypeStruct
    return pl.pallas_call(
        _proj_kernel,
        grid=(t // tm,),
        in_specs=[row(d), full(nw), full(wb), full(qg), full(kg)],
        out_specs=[row(A_QKV), row(A_WIDTH), row(B_HEADS * LANES), grp, grp, row(B_WIDTH), row(IDX_WIDTH),
                   row(IDX_DIM), row(SMALL_W)],
        out_shape=[sds((t, A_QKV), F32), sds((t, A_WIDTH), F32), sds((t, B_HEADS * LANES), BF16),
                   sds((B_KV_HEADS, t, LANES), BF16), sds((B_KV_HEADS, t, LANES), BF16), sds((t, B_WIDTH), F32),
                   sds((t, IDX_WIDTH), BF16), sds((t, IDX_DIM), BF16), sds((t, SMALL_W), F32)],
        compiler_params=pltpu.CompilerParams(dimension_semantics=("arbitrary",), vmem_limit_bytes=VMEM_LIMIT),
        name="proj",
    )(x2, nw, wb, qg, kg)


def _gdn_kernel(qkv_ref, sm_ref, smt_ref, az_ref, cw_ref, alog_r_ref, dtb_r_ref, alog_c_ref, dtb_c_ref,
                onw_ref, o_ref, s_ref, tail_ref, *, bt):
    c = A_CHUNK

    @pl.when(pl.program_id(1) == 0)
    def _():
        s_ref[...] = jnp.zeros(s_ref.shape, F32)
        tail_ref[...] = jnp.zeros(tail_ref.shape, F32)

    xin = qkv_ref[...]
    tail = tail_ref[...]
    cw = cw_ref[...]
    row8 = lax.broadcasted_iota(I32, tail.shape, 0)
    acc = xin * cw[A_CONV - 1:A_CONV, :]
    for d in range(1, A_CONV):
        xr = pltpu.roll(xin, d, axis=0)
        pr = pltpu.roll(tail, d, axis=0)
        head = jnp.where(row8 < d, pr, xr[0:8])
        xs = jnp.concatenate([head, xr[8:]], axis=0)
        acc = acc + xs * cw[A_CONV - 1 - d:A_CONV - d, :]
    tail_ref[...] = xin[bt - 8:bt]
    qkv = _silu(acc)

    sm = sm_ref[...]
    smt = smt_ref[0]
    g_col = -jnp.exp(alog_r_ref[...]) * _softplus(sm + dtb_r_ref[...])
    g_row = -jnp.exp(alog_c_ref[...]) * _softplus(smt + dtb_c_ref[...])
    beta_all = jax.nn.sigmoid(sm)

    r = lax.broadcasted_iota(I32, (bt, bt), 0)
    s = lax.broadcasted_iota(I32, (bt, bt), 1)
    same = (r // c) == (s // c)
    tril = same & (s <= r)
    strict = same & (s < r)
    eye = jnp.where(s == r, 1.0, 0.0).astype(F32)
    gc_col = _dot(jnp.where(tril, 1.0, 0.0).astype(F32), g_col, HI)
    gc_row = _dot(g_row, jnp.where(same & (r <= s), 1.0, 0.0).astype(F32), HI)
    gtot = _dot(jnp.where(same, 1.0, 0.0).astype(F32), g_col, HI)
    egc_all = jnp.exp(gc_col)
    ekd_all = jnp.exp(gtot - gc_col)
    egl_all = jnp.exp(gtot)
    onw = onw_ref[...]

    heads = range(A_HEADS)
    qs, ks_, kbs, decays, amats = [], [], [], [], []
    for h in heads:
        q = qkv[:, h * A_DK:(h + 1) * A_DK]
        k = qkv[:, A_HEADS * A_DK + h * A_DK:A_HEADS * A_DK + (h + 1) * A_DK]
        q = q * lax.rsqrt(jnp.sum(q * q, axis=-1, keepdims=True) + EPS) * (A_DK ** -0.5)
        k = k * lax.rsqrt(jnp.sum(k * k, axis=-1, keepdims=True) + EPS)
        lane = SM_DECAY + h
        gcc = gc_col[:, lane:lane + 1]
        gcr = gc_row[lane:lane + 1, :]
        decay = jnp.where(tril, jnp.exp(jnp.where(tril, gcc - gcr, 0.0)), 0.0)
        kb = k * beta_all[:, SM_BETA + h:SM_BETA + h + 1]
        qs.append(q)
        ks_.append(k)
        kbs.append(kb)
        decays.append(decay)
        amats.append(jnp.where(strict, _dot_nt(kb.astype(BF16), k.astype(BF16)) * decay, 0.0))
    tmats = [eye - a for a in amats]
    apows = list(amats)
    n = 2
    while n < c:
        for h in heads:
            a_bf = apows[h].astype(BF16)
            apows[h] = _dot(a_bf, a_bf)
        for h in heads:
            tmats[h] = tmats[h] + _dot(tmats[h].astype(BF16), apows[h].astype(BF16))
        n *= 2
    values, kcds, atts, qds, kds, egls = [], [], [], [], [], []
    for h in heads:
        lane = SM_DECAY + h
        v = qkv[:, 2 * A_HEADS * A_DK + h * A_DV:2 * A_HEADS * A_DK + (h + 1) * A_DV]
        t_bf = tmats[h].astype(BF16)
        egc = egc_all[:, lane:lane + 1]
        values.append(_dot(t_bf, (v * beta_all[:, SM_BETA + h:SM_BETA + h + 1]).astype(BF16)))
        kcds.append(_dot(t_bf, (kbs[h] * egc).astype(BF16)).astype(BF16))
        atts.append((_dot_nt(qs[h].astype(BF16), ks_[h].astype(BF16)) * decays[h]).astype(BF16))
        qds.append((qs[h] * egc).astype(BF16))
        kds.append((ks_[h] * ekd_all[:, lane:lane + 1]).astype(BF16))
        egls.append(egl_all[:, lane:lane + 1])
    states = [s_ref[h] for h in heads]
    o_inter = [[] for _ in heads]
    v_new = [[] for _ in heads]
    for ci in range(bt // c):
        rows = slice(ci * c, (ci + 1) * c)
        for h in heads:
            s_bf = states[h].astype(BF16)
            vn = (values[h][rows] - _dot(kcds[h][rows], s_bf)).astype(BF16)
            o_inter[h].append(_dot(qds[h][rows], s_bf))
            v_new[h].append(vn)
            gl = egls[h][ci * c:ci * c + 8]
            states[h] = states[h] * jnp.concatenate([gl] * (A_DK // 8), axis=0) + _dot_tn(kds[h][rows], vn)
    for h in heads:
        s_ref[h] = states[h]
        o = jnp.concatenate(o_inter[h], axis=0) + _dot(atts[h], jnp.concatenate(v_new[h], axis=0))
        on = o * lax.rsqrt(jnp.mean(o * o, axis=-1, keepdims=True) + EPS) * onw
        z = az_ref[:, h * A_DV:(h + 1) * A_DV]
        o_ref[:, h * A_DV:(h + 1) * A_DV] = on * _silu(z)


def _gdn(qkv, small, small_t, a_z, conv_w, a_log, dt_bias, out_norm_w, batch, seq, bt):
    nt = seq // bt
    lane_pad = lambda v: jnp.zeros((1, LANES), F32).at[0, SM_DECAY:SM_DECAY + A_HEADS].set(v)
    sub_pad = lambda v: jnp.zeros((8, 1), F32).at[SM_DECAY:SM_DECAY + A_HEADS, 0].set(v)
    cw = jnp.zeros((8, A_QKV), F32).at[:A_CONV].set(conv_w)
    onw = out_norm_w[None, :]
    row = lambda w: pl.BlockSpec((bt, w), lambda b, t: (b * nt + t, 0))
    full = lambda a: pl.BlockSpec(a.shape, lambda b, t: (0, 0))
    consts = (cw, lane_pad(a_log), lane_pad(dt_bias), sub_pad(a_log), sub_pad(dt_bias), onw)
    return pl.pallas_call(
        functools.partial(_gdn_kernel, bt=bt),
        grid=(batch, nt),
        in_specs=[row(A_QKV), row(SMALL_W), pl.BlockSpec((1, 8, bt), lambda b, t: (b, 0, t)), row(A_WIDTH)]
        + [full(a) for a in consts],
        out_specs=row(A_WIDTH),
        out_shape=jax.ShapeDtypeStruct((batch * seq, A_WIDTH), F32),
        scratch_shapes=[pltpu.VMEM((A_HEADS, A_DK, A_DV), F32), pltpu.VMEM((8, A_QKV), F32)],
        compiler_params=pltpu.CompilerParams(dimension_semantics=("arbitrary", "arbitrary"),
                                             vmem_limit_bytes=VMEM_LIMIT),
        name="gdn",
    )(qkv, small, small_t, a_z, *consts)


def _dsa_kernel(q_ref, kx_ref, vxt_ref, iq_ref, ik_ref, smt_ref, bz_ref, ltri_ref, o_ref,
                sc_ref, hb_ref, mx_ref, acc_ref, s_ref, *, qb, ks, n_sel):
    rep = B_HEADS // B_KV_HEADS
    i = pl.program_id(1)
    ns = (i * qb) // ks + 1
    kidx = lax.broadcasted_iota(I32, (ks, qb), 0)
    qpos = i * qb + lax.broadcasted_iota(I32, (ks, qb), 1)
    kidx_t = lax.broadcasted_iota(I32, (LANES, qb), 0)
    qpos_t = i * qb + lax.broadcasted_iota(I32, (LANES, qb), 1)

    iq = iq_ref[...]
    iqs = jnp.concatenate([iq[:, h * IDX_DIM:(h + 1) * IDX_DIM] for h in range(IDX_HEADS)], axis=0)
    iwt = smt_ref[0][SM_IW:SM_IW + IDX_HEADS, :] * (IDX_HEADS ** -0.5 * IDX_DIM ** -0.5)

    def score_body(j, carry):
        for u in range(ks // LANES):
            start = pl.multiple_of(j * ks + u * LANES, LANES)
            lg = _dot_nt(ik_ref[pl.ds(start, LANES), :], iqs)
            sc = jnp.zeros((LANES, qb), F32)
            for h in range(IDX_HEADS):
                sc = sc + iwt[h:h + 1, :] * jnp.maximum(lg[:, h * qb:(h + 1) * qb], 0.0)
            rows = slice(u * LANES, (u + 1) * LANES)
            sc = jnp.where(start + kidx_t <= qpos_t, sc, -jnp.inf)
            sc_ref[j, rows, :] = sc
            hb_ref[j, rows, :] = sc.astype(BF16)
        return carry

    lax.fori_loop(0, ns, score_body, 0)

    def count(src_ref, chunk, hit_of, dtype):
        def body(j, acc):
            for u in range(ks // chunk):
                acc = acc + hit_of(src_ref[j, u * chunk:(u + 1) * chunk, :])
            return acc

        acc = lax.fori_loop(0, ns, body, jnp.zeros((chunk, qb), dtype))
        return jnp.sum(acc.astype(F32), axis=0, keepdims=True)

    def key_value(key):
        val = pltpu.bitcast(key ^ ((key >> 31) & 0x7FFFFFFF), F32)
        return jnp.where(key < KEY_MIN_FINITE, -jnp.inf, val)

    def count_ge(key):
        cand = jnp.broadcast_to(key_value(key), (8, qb))
        return count(sc_ref, 8, lambda sc: jnp.where(sc >= cand, 1.0, 0.0), F32)

    one_b = jnp.ones((16, qb), BF16)
    zero_b = jnp.zeros((16, qb), BF16)

    def coarse_value(key):
        bits = (key ^ ((key >> 31) & 0x7FFFFFFF)) & (-(1 << 16))
        return jnp.where(key < KEY_MIN_FINITE, -jnp.inf, pltpu.bitcast(bits, F32))

    def count_coarse(hit_of):
        return count(hb_ref, 16, lambda hb: jnp.where(hit_of(hb), one_b, zero_b), BF16)

    def coarse_body(bi, tau):
        key = tau ^ lax.shift_left(jnp.int32(1), 31 - bi)
        cand = jnp.broadcast_to(coarse_value(key), (16, qb)).astype(BF16)
        return jnp.where(count_coarse(lambda hb: hb >= cand) >= n_sel, key, tau)

    coarse = lax.fori_loop(0, 16, coarse_body, jnp.full((1, qb), INT_MIN, I32))
    step = 1 << 16
    centre = jnp.where(coarse < 0, coarse | (step - 1), coarse)
    lo0 = jnp.maximum(centre, INT_MIN + 2 * step) - 2 * step
    hi0 = centre + step
    at_zero = ((count_coarse(lambda hb: hb > zero_b) < n_sel) & (count_coarse(lambda hb: hb >= zero_b) >= n_sel))
    lo0 = jnp.where(at_zero, 0, lo0)
    hi0 = jnp.where(at_zero, 1, hi0)

    def fine_cond(state):
        it, _, _, _, open_rows = state
        return jnp.logical_and(it < 20, open_rows > 0.0)

    def fine_body(state):
        it, lo, hi, c_lo, _ = state
        settled = (c_lo == n_sel) | (hi - lo <= 1)
        open_rows = jnp.max(jnp.where(settled, 0.0, 1.0))
        mid = lo + ((hi - lo) >> 1)
        cnt = count_ge(mid)
        ok = cnt >= n_sel
        return (it + 1, jnp.where(ok, mid, lo), jnp.where(ok, hi, mid), jnp.where(ok, cnt, c_lo), open_rows)

    _, tau, _, _, _ = lax.while_loop(
        fine_cond, fine_body, (jnp.int32(0), lo0, hi0, jnp.full((1, qb), -1.0, F32), jnp.float32(1.0)))
    tau_val = key_value(tau)
    tau8 = jnp.broadcast_to(tau_val, (8, qb))
    need = n_sel - count(sc_ref, 8, lambda sc: jnp.where(sc > tau8, 1.0, 0.0), F32)

    q = q_ref[...]
    qs = [jnp.concatenate([q[:, (g * rep + r) * LANES:(g * rep + r + 1) * LANES] for r in range(rep)], axis=0)
          for g in range(B_KV_HEADS)]
    mx_ref[...] = jnp.full(mx_ref.shape, NEG_BIG, F32)
    acc_ref[...] = jnp.zeros(acc_ref.shape, F32)
    ltri = ltri_ref[...]
    taub = jnp.broadcast_to(tau_val, (ks, qb))

    def att_body(j, eq_seen):
        start = pl.multiple_of(j * ks, ks)
        sct = sc_ref[j]
        eqf = jnp.where(sct == taub, 1.0, 0.0)
        pref = _dot(ltri, eqf.astype(BF16)) + eq_seen
        take = jnp.where(sct > taub, 1.0, jnp.where(pref <= need, eqf, 0.0))
        bias = jnp.where((take > 0.0) & (j * ks + kidx <= qpos), 0.0, 2.0 * NEG_BIG)
        alphas = []
        for g in range(B_KV_HEADS):
            kx = kx_ref[g, pl.ds(start, ks), :]
            for r in range(rep):
                cols = slice(r * qb, (r + 1) * qb)
                m_old = mx_ref[g, :, cols]
                m_new = m_old
                for kc in range(ks // LANES):
                    rows = slice(kc * LANES, (kc + 1) * LANES)
                    s = _dot_nt(kx[rows], qs[g][cols]) + bias[rows]
                    s_ref[g, rows, cols] = s
                    m_new = jnp.maximum(m_new, jnp.max(s, axis=0, keepdims=True))
                mx_ref[g, :, cols] = m_new
                alphas.append(jnp.exp2(m_old - m_new)[0:1, :])
        half = ks // 2
        for g in range(B_KV_HEADS):
            vxt = vxt_ref[g, :, pl.ds(start, ks)]
            for r in range(rep):
                cols = slice(r * qb, (r + 1) * qb)
                m_new = mx_ref[g, 0:1, cols]
                acc = alphas[g * rep + r] * acc_ref[g, :, cols]
                for kk in range(2):
                    rows = slice(kk * half, (kk + 1) * half)
                    p = jnp.exp2(s_ref[g, rows, cols] - m_new).astype(BF16)
                    acc = acc + _dot(vxt[:, rows], p)
                acc_ref[g, :, cols] = acc
        return eq_seen + jnp.sum(eqf, axis=0, keepdims=True)

    lax.fori_loop(0, ns, att_body, jnp.zeros((1, qb), F32))

    lane = lax.broadcasted_iota(I32, (qb, LANES), 1)
    outs = []
    for g in range(B_KV_HEADS):
        acc = acc_ref[g]
        a = jnp.concatenate([jnp.transpose(acc[:, r * qb:(r + 1) * qb]) for r in range(rep)], axis=0)
        outs.append(a / a[:, B_HD:B_HD + 1])
    for cblk in range(B_WIDTH // LANES):
        g = (2 * cblk) // rep
        r0 = (2 * cblk) % rep
        a = outs[g][r0 * qb:(r0 + 1) * qb]
        b = pltpu.roll(outs[g][(r0 + 1) * qb:(r0 + 2) * qb], B_HD, axis=1)
        z = bz_ref[:, cblk * LANES:(cblk + 1) * LANES]
        o_ref[:, cblk * LANES:(cblk + 1) * LANES] = jnp.where(lane < B_HD, a, b) * _silu(z)


def _dsa(bqx, kx, vxt, iq, ik, small_t, bz, batch, seq, qb, ks):
    nq = seq // qb
    n_sel = min(TOPK_MAX, seq // 4)
    rep = B_HEADS // B_KV_HEADS
    ltri = (jnp.arange(ks)[None, :] <= jnp.arange(ks)[:, None]).astype(BF16)
    row = lambda w: pl.BlockSpec((qb, w), lambda b, i: (b * nq + i, 0))
    per_batch = lambda w: pl.BlockSpec((seq, w), lambda b, i: (b, 0))
    return pl.pallas_call(
        functools.partial(_dsa_kernel, qb=qb, ks=ks, n_sel=n_sel),
        grid=(batch, nq),
        in_specs=[row(B_HEADS * LANES),
                  pl.BlockSpec((B_KV_HEADS, seq, LANES), lambda b, i: (0, b, 0)),
                  pl.BlockSpec((B_KV_HEADS, LANES, seq), lambda b, i: (0, 0, b)),
                  row(IDX_WIDTH), per_batch(IDX_DIM),
                  pl.BlockSpec((1, 16, qb), lambda b, i: (b, 0, i)),
                  row(B_WIDTH), pl.BlockSpec((ks, ks), lambda b, i: (0, 0))],
        out_specs=row(B_WIDTH),
        out_shape=jax.ShapeDtypeStruct((batch * seq, B_WIDTH), F32),
        scratch_shapes=[pltpu.VMEM((seq // ks, ks, qb), F32),
                        pltpu.VMEM((seq // ks, ks, qb), BF16),
                        pltpu.VMEM((B_KV_HEADS, 8, rep * qb), F32),
                        pltpu.VMEM((B_KV_HEADS, LANES, rep * qb), F32),
                        pltpu.VMEM((B_KV_HEADS, ks, rep * qb), F32)],
        compiler_params=pltpu.CompilerParams(dimension_semantics=("arbitrary", "arbitrary"),
                                             vmem_limit_bytes=VMEM_LIMIT),
        name="dsa",
    )(bqx, kx, vxt, iq, ik, small_t, bz, ltri)


def _out_kernel(x_ref, oa_ref, ob_ref, p_ref, woa_ref, wob_ref, wp_ref, wg_ref, gn_ref, bg_ref, o_ref):
    x1 = (x_ref[...] + _dot(oa_ref[...].astype(BF16), woa_ref[...])
          + _dot(ob_ref[...].astype(BF16), wob_ref[...]))
    hn = x1 * lax.rsqrt(jnp.mean(x1 * x1, axis=-1, keepdims=True) + EPS) * gn_ref[...]
    gate = jax.nn.sigmoid(_dot(hn.astype(BF16), wg_ref[...]) + bg_ref[...])
    o_ref[...] = x1 + _dot(p_ref[...].astype(BF16), wp_ref[...]) * gate


def _output(x2, oa, ob, p2, w_out, w_ple, gate_norm_w, w_gate, b_gate, tm):
    t, d = x2.shape
    woa = w_out[:A_WIDTH].astype(BF16)
    wob = w_out[A_WIDTH:].astype(BF16)
    wp = w_ple.astype(BF16)
    wg = w_gate.astype(BF16)
    gn = gate_norm_w[None, :]
    bg = b_gate[None, :]
    row = lambda w: pl.BlockSpec((tm, w), lambda i: (i, 0))
    full = lambda a: pl.BlockSpec(a.shape, lambda i: (0, 0))
    return pl.pallas_call(
        _out_kernel,
        grid=(t // tm,),
        in_specs=[row(d), row(A_WIDTH), row(B_WIDTH), row(PLE_DIM)] + [full(a) for a in (woa, wob, wp, wg, gn, bg)],
        out_specs=row(d),
        out_shape=jax.ShapeDtypeStruct((t, d), F32),
        compiler_params=pltpu.CompilerParams(dimension_semantics=("arbitrary",), vmem_limit_bytes=VMEM_LIMIT),
        name="out",
    )(x2, oa, ob, p2, woa, wob, wp, wg, gn, bg)


def kernel(x, p, attn_norm_w, w_in, conv_w, a_log, dt_bias, a_out_norm_w, b_q_norm_w, b_k_norm_w, w_out,
           w_ple, ple_gate_norm_w, w_ple_gate, b_ple_gate):
    batch, seq, d = x.shape
    t = batch * seq
    tm = min(512, t)
    bt = min(256, seq)
    qb = min(256, seq)
    ks = min(512, seq)
    x2 = x.reshape(t, d)
    for i in range(w_in.shape[0]):
        qkv, az, bqx, kx, vx, bz, iq, ik, small = _project(x2, attn_norm_w[i], w_in[i], b_q_norm_w[i],
                                                           b_k_norm_w[i], tm)
        small_t = jnp.swapaxes(small.reshape(batch, seq, SMALL_W)[:, :, :16], 1, 2)
        oa = _gdn(qkv, small, small_t, az, conv_w[i], a_log[i], dt_bias[i], a_out_norm_w[i], batch, seq, bt)
        ob = _dsa(bqx, kx, jnp.swapaxes(vx, 1, 2), iq, ik, small_t, bz, batch, seq, qb, ks)
        x2 = _output(x2, oa, ob, p[i].reshape(t, PLE_DIM), w_out[i], w_ple[i], ple_gate_norm_w[i],
                     w_ple_gate[i], b_ple_gate[i], tm)
    return x2.reshape(batch, seq, d)
```

```python
import functools

import jax
import jax.numpy as jnp
from jax import lax
from jax.experimental import pallas as pl
from jax.experimental.pallas import tpu as pltpu

F32 = jnp.float32
BF16 = jnp.bfloat16
I32 = jnp.int32
EPS = 1e-6
HI = lax.Precision.HIGHEST

PLE_DIM = 256
A_HEADS = 4
A_DK = 128
A_DV = 128
A_CONV = 4
A_CHUNK = 64
A_WIDTH = A_HEADS * A_DV
A_QKV = 2 * A_HEADS * A_DK + A_WIDTH
B_HEADS = 8
B_KV_HEADS = 2
B_HD = 64
B_WIDTH = B_HEADS * B_HD
B_KV_WIDTH = B_KV_HEADS * B_HD
IDX_HEADS = 8
IDX_DIM = 128
IDX_WIDTH = IDX_HEADS * IDX_DIM
TOPK_MAX = 256
LANES = 128
SMALL_W = LANES
SM_BETA = 0
SM_DECAY = A_HEADS
SM_IW = 2 * A_HEADS

VMEM_LIMIT = 56 * 1024 * 1024
NEG_BIG = -1e30
LOG2_E = 1.4426950408889634
INT_MIN = -(2 ** 31)
KEY_MIN_FINITE = INT_MIN + 0x00800000


def _dot(a, b, prec=None):
    return jnp.dot(a, b, preferred_element_type=F32, precision=prec)


def _dot_nt(a, b, prec=None):
    return lax.dot_general(a, b, (((1,), (1,)), ((), ())), preferred_element_type=F32, precision=prec)


def _dot_tn(a, b, prec=None):
    return lax.dot_general(a, b, (((0,), (0,)), ((), ())), preferred_element_type=F32, precision=prec)


def _silu(x):
    return x * jax.nn.sigmoid(x)


def _softplus(x):
    return jnp.maximum(x, 0.0) + jnp.log1p(jnp.exp(-jnp.abs(x)))


def _seg_norm64(xb, gain_row):
    lane = lax.broadcasted_iota(I32, xb.shape, 1)
    lo = lane < B_HD
    sq = xb * xb
    s_lo = jnp.sum(jnp.where(lo, sq, 0.0), axis=-1, keepdims=True)
    s_hi = jnp.sum(jnp.where(lo, 0.0, sq), axis=-1, keepdims=True)
    ms = jnp.where(lo, s_lo, s_hi) * (1.0 / B_HD)
    return xb * lax.rsqrt(ms + EPS) * gain_row


_WB_QKV = 0
_WB_AZ = _WB_QKV + A_QKV
_WB_BQ = _WB_AZ + A_WIDTH
_WB_BK = _WB_BQ + B_WIDTH
_WB_BV = _WB_BK + B_KV_WIDTH
_WB_BZ = _WB_BV + B_KV_WIDTH
_WB_IQ = _WB_BZ + B_WIDTH
_WB_IK = _WB_IQ + IDX_WIDTH
_WB_SM = _WB_IK + IDX_DIM
_WB_END = _WB_SM + SMALL_W


def _proj_kernel(x_ref, nw_ref, wb_ref, qg_ref, kg_ref,
                 qkv_ref, az_ref, bqx_ref, kx_ref, vx_ref, bz_ref, iq_ref, ik_ref, sm_ref):
    x = x_ref[...]
    h = x * lax.rsqrt(jnp.mean(x * x, axis=-1, keepdims=True) + EPS) * nw_ref[...]
    hb = h.astype(BF16)
    step = 512
    for c0 in range(0, A_QKV, step):
        qkv_ref[:, c0:c0 + step] = _dot(hb, wb_ref[:, _WB_QKV + c0:_WB_QKV + c0 + step])
    az_ref[...] = _dot(hb, wb_ref[:, _WB_AZ:_WB_BQ])
    bz_ref[...] = _dot(hb, wb_ref[:, _WB_BZ:_WB_IQ])
    lane = lax.broadcasted_iota(I32, (x.shape[0], LANES), 1)
    lo = lane < B_HD
    bk = _seg_norm64(_dot(hb, wb_ref[:, _WB_BK:_WB_BV]), kg_ref[...])
    bv = _dot(hb, wb_ref[:, _WB_BV:_WB_BZ])
    k_tail = jnp.where(lane == B_HD, 1.0, 0.0)
    for g in range(B_KV_HEADS):
        kg = bk if g == 0 else pltpu.roll(bk, B_HD, axis=1)
        vg = bv if g == 0 else pltpu.roll(bv, B_HD, axis=1)
        kx_ref[g] = jnp.where(lo, kg, k_tail).astype(BF16)
        vx_ref[g] = jnp.where(lo, vg, 1.0).astype(BF16)
    bq = _dot(hb, wb_ref[:, _WB_BQ:_WB_BK])
    scale = B_HD ** -0.5 * LOG2_E
    for c0 in range(0, B_WIDTH, LANES):
        nb = _seg_norm64(bq[:, c0:c0 + LANES], qg_ref[...]) * scale
        bqx_ref[:, 2 * c0:2 * c0 + LANES] = jnp.where(lo, nb, 0.0).astype(BF16)
        bqx_ref[:, 2 * c0 + LANES:2 * c0 + 2 * LANES] = jnp.where(
            lo, pltpu.roll(nb, B_HD, axis=1), 0.0).astype(BF16)
    for c0 in range(0, IDX_WIDTH, step):
        iq_ref[:, c0:c0 + step] = _dot(hb, wb_ref[:, _WB_IQ + c0:_WB_IQ + c0 + step]).astype(BF16)
    ik_ref[...] = _dot(hb, wb_ref[:, _WB_IK:_WB_SM]).astype(BF16)
    sm_ref[...] = _dot(hb, wb_ref[:, _WB_SM:_WB_END])


def _project(x2, norm_w, w_in, q_gain, k_gain, tm):
    t, d = x2.shape
    sizes = (A_HEADS * A_DK, A_HEADS * A_DK, A_WIDTH, A_WIDTH, A_HEADS, A_HEADS, B_WIDTH, B_KV_WIDTH,
             B_KV_WIDTH, B_WIDTH, IDX_WIDTH, IDX_DIM, IDX_HEADS)
    offs = [0]
    for s in sizes:
        offs.append(offs[-1] + s)
    (a_q, a_k, a_v, a_z, a_b, a_a, b_q, b_k, b_v, b_z, i_q, i_k, i_w) = [
        w_in[:, offs[n]:offs[n + 1]] for n in range(len(sizes))]
    pad = jnp.zeros((d, SMALL_W - 2 * A_HEADS - IDX_HEADS), F32)
    wb = jnp.concatenate([a_q, a_k, a_v, a_z, b_q, b_k, b_v, b_z, i_q, i_k, a_b, a_a, i_w, pad],
                         axis=1).astype(BF16)
    qg = jnp.tile(q_gain, LANES // B_HD)[None, :]
    kg = jnp.tile(k_gain, LANES // B_HD)[None, :]
    row = lambda w: pl.BlockSpec((tm, w), lambda i: (i, 0))
    grp = pl.BlockSpec((B_KV_HEADS, tm, LANES), lambda i: (0, i, 0))
    full = lambda a: pl.BlockSpec(a.shape, lambda i: (0, 0))
    nw = norm_w[None, :]
    sds = jax.ShapeDtypeStruct
    return pl.pallas_call(
        _proj_kernel,
        grid=(t // tm,),
        in_specs=[row(d), full(nw), full(wb), full(qg), full(kg)],
        out_specs=[row(A_QKV), row(A_WIDTH), row(B_HEADS * LANES), grp, grp, row(B_WIDTH), row(IDX_WIDTH),
                   row(IDX_DIM), row(SMALL_W)],
        out_shape=[sds((t, A_QKV), F32), sds((t, A_WIDTH), F32), sds((t, B_HEADS * LANES), BF16),
                   sds((B_KV_HEADS, t, LANES), BF16), sds((B_KV_HEADS, t, LANES), BF16), sds((t, B_WIDTH), F32),
                   sds((t, IDX_WIDTH), BF16), sds((t, IDX_DIM), BF16), sds((t, SMALL_W), F32)],
        compiler_params=pltpu.CompilerParams(dimension_semantics=("arbitrary",), vmem_limit_bytes=VMEM_LIMIT),
        name="proj",
    )(x2, nw, wb, qg, kg)


def _gdn_kernel(qkv_ref, sm_ref, smt_ref, az_ref, cw_ref, alog_r_ref, dtb_r_ref, alog_c_ref, dtb_c_ref,
                onw_ref, o_ref, s_ref, tail_ref, *, bt):
    c = A_CHUNK

    @pl.when(pl.program_id(1) == 0)
    def _():
        s_ref[...] = jnp.zeros(s_ref.shape, F32)
        tail_ref[...] = jnp.zeros(tail_ref.shape, F32)

    xin = qkv_ref[...]
    tail = tail_ref[...]
    cw = cw_ref[...]
    row8 = lax.broadcasted_iota(I32, tail.shape, 0)
    acc = xin * cw[A_CONV - 1:A_CONV, :]
    for d in range(1, A_CONV):
        xr = pltpu.roll(xin, d, axis=0)
        pr = pltpu.roll(tail, d, axis=0)
        head = jnp.where(row8 < d, pr, xr[0:8])
        xs = jnp.concatenate([head, xr[8:]], axis=0)
        acc = acc + xs * cw[A_CONV - 1 - d:A_CONV - d, :]
    tail_ref[...] = xin[bt - 8:bt]
    qkv = _silu(acc)

    sm = sm_ref[...]
    smt = smt_ref[0]
    g_col = -jnp.exp(alog_r_ref[...]) * _softplus(sm + dtb_r_ref[...])
    g_row = -jnp.exp(alog_c_ref[...]) * _softplus(smt + dtb_c_ref[...])
    beta_all = jax.nn.sigmoid(sm)

    r = lax.broadcasted_iota(I32, (bt, bt), 0)
    s = lax.broadcasted_iota(I32, (bt, bt), 1)
    same = (r // c) == (s // c)
    tril = same & (s <= r)
    strict = same & (s < r)
    eye = jnp.where(s == r, 1.0, 0.0).astype(F32)
    gc_col = _dot(jnp.where(tril, 1.0, 0.0).astype(F32), g_col, HI)
    gc_row = _dot(g_row, jnp.where(same & (r <= s), 1.0, 0.0).astype(F32), HI)
    gtot = _dot(jnp.where(same, 1.0, 0.0).astype(F32), g_col, HI)
    egc_all = jnp.exp(gc_col)
    ekd_all = jnp.exp(gtot - gc_col)
    egl_all = jnp.exp(gtot)
    onw = onw_ref[...]

    heads = range(A_HEADS)
    qs, ks_, kbs, decays, amats = [], [], [], [], []
    for h in heads:
        q = qkv[:, h * A_DK:(h + 1) * A_DK]
        k = qkv[:, A_HEADS * A_DK + h * A_DK:A_HEADS * A_DK + (h + 1) * A_DK]
        q = q * lax.rsqrt(jnp.sum(q * q, axis=-1, keepdims=True) + EPS) * (A_DK ** -0.5)
        k = k * lax.rsqrt(jnp.sum(k * k, axis=-1, keepdims=True) + EPS)
        lane = SM_DECAY + h
        gcc = gc_col[:, lane:lane + 1]
        gcr = gc_row[lane:lane + 1, :]
        decay = jnp.where(tril, jnp.exp(jnp.where(tril, gcc - gcr, 0.0)), 0.0)
        kb = k * beta_all[:, SM_BETA + h:SM_BETA + h + 1]
        qs.append(q)
        ks_.append(k)
        kbs.append(kb)
        decays.append(decay)
        amats.append(jnp.where(strict, _dot_nt(kb.astype(BF16), k.astype(BF16)) * decay, 0.0))
    tmats = [eye - a for a in amats]
    apows = list(amats)
    n = 2
    while n < c:
        for h in heads:
            a_bf = apows[h].astype(BF16)
            apows[h] = _dot(a_bf, a_bf)
        for h in heads:
            tmats[h] = tmats[h] + _dot(tmats[h].astype(BF16), apows[h].astype(BF16))
        n *= 2
    values, kcds, atts, qds, kds, egls = [], [], [], [], [], []
    for h in heads:
        lane = SM_DECAY + h
        v = qkv[:, 2 * A_HEADS * A_DK + h * A_DV:2 * A_HEADS * A_DK + (h + 1) * A_DV]
        t_bf = tmats[h].astype(BF16)
        egc = egc_all[:, lane:lane + 1]
        values.append(_dot(t_bf, (v * beta_all[:, SM_BETA + h:SM_BETA + h + 1]).astype(BF16)))
        kcds.append(_dot(t_bf, (kbs[h] * egc).astype(BF16)).astype(BF16))
        atts.append((_dot_nt(qs[h].astype(BF16), ks_[h].astype(BF16)) * decays[h]).astype(BF16))
        qds.append((qs[h] * egc).astype(BF16))
        kds.append((ks_[h] * ekd_all[:, lane:lane + 1]).astype(BF16))
        egls.append(egl_all[:, lane:lane + 1])
    states = [s_ref[h] for h in heads]
    o_inter = [[] for _ in heads]
    v_new = [[] for _ in heads]
    for ci in range(bt // c):
        rows = slice(ci * c, (ci + 1) * c)
        for h in heads:
            s_bf = states[h].astype(BF16)
            vn = (values[h][rows] - _dot(kcds[h][rows], s_bf)).astype(BF16)
            o_inter[h].append(_dot(qds[h][rows], s_bf))
            v_new[h].append(vn)
            gl = egls[h][ci * c:ci * c + 8]
            states[h] = states[h] * jnp.concatenate([gl] * (A_DK // 8), axis=0) + _dot_tn(kds[h][rows], vn)
    for h in heads:
        s_ref[h] = states[h]
        o = jnp.concatenate(o_inter[h], axis=0) + _dot(atts[h], jnp.concatenate(v_new[h], axis=0))
        on = o * lax.rsqrt(jnp.mean(o * o, axis=-1, keepdims=True) + EPS) * onw
        z = az_ref[:, h * A_DV:(h + 1) * A_DV]
        o_ref[:, h * A_DV:(h + 1) * A_DV] = on * _silu(z)


def _gdn(qkv, small, small_t, a_z, conv_w, a_log, dt_bias, out_norm_w, batch, seq, bt):
    nt = seq // bt
    lane_pad = lambda v: jnp.zeros((1, LANES), F32).at[0, SM_DECAY:SM_DECAY + A_HEADS].set(v)
    sub_pad = lambda v: jnp.zeros((8, 1), F32).at[SM_DECAY:SM_DECAY + A_HEADS, 0].set(v)
    cw = jnp.zeros((8, A_QKV), F32).at[:A_CONV].set(conv_w)
    onw = out_norm_w[None, :]
    row = lambda w: pl.BlockSpec((bt, w), lambda b, t: (b * nt + t, 0))
    full = lambda a: pl.BlockSpec(a.shape, lambda b, t: (0, 0))
    consts = (cw, lane_pad(a_log), lane_pad(dt_bias), sub_pad(a_log), sub_pad(dt_bias), onw)
    return pl.pallas_call(
        functools.partial(_gdn_kernel, bt=bt),
        grid=(batch, nt),
        in_specs=[row(A_QKV), row(SMALL_W), pl.BlockSpec((1, 8, bt), lambda b, t: (b, 0, t)), row(A_WIDTH)]
        + [full(a) for a in consts],
        out_specs=row(A_WIDTH),
        out_shape=jax.ShapeDtypeStruct((batch * seq, A_WIDTH), F32),
        scratch_shapes=[pltpu.VMEM((A_HEADS, A_DK, A_DV), F32), pltpu.VMEM((8, A_QKV), F32)],
        compiler_params=pltpu.CompilerParams(dimension_semantics=("arbitrary", "arbitrary"),
                                             vmem_limit_bytes=VMEM_LIMIT),
        name="gdn",
    )(qkv, small, small_t, a_z, *consts)


def _dsa_kernel(q_ref, kx_ref, vxt_ref, iq_ref, ik_ref, smt_ref, bz_ref, ltri_ref, o_ref,
                sc_ref, hb_ref, mx_ref, acc_ref, s_ref, *, qb, ks, n_sel):
    rep = B_HEADS // B_KV_HEADS
    i = pl.program_id(1)
    ns = (i * qb) // ks + 1
    kidx = lax.broadcasted_iota(I32, (ks, qb), 0)
    qpos = i * qb + lax.broadcasted_iota(I32, (ks, qb), 1)
    kidx_t = lax.broadcasted_iota(I32, (LANES, qb), 0)
    qpos_t = i * qb + lax.broadcasted_iota(I32, (LANES, qb), 1)

    iq = iq_ref[...]
    iqs = jnp.concatenate([iq[:, h * IDX_DIM:(h + 1) * IDX_DIM] for h in range(IDX_HEADS)], axis=0)
    iwt = smt_ref[0][SM_IW:SM_IW + IDX_HEADS, :] * (IDX_HEADS ** -0.5 * IDX_DIM ** -0.5)

    def score_body(j, carry):
        for u in range(ks // LANES):
            start = pl.multiple_of(j * ks + u * LANES, LANES)
            lg = _dot_nt(ik_ref[pl.ds(start, LANES), :], iqs)
            sc = jnp.zeros((LANES, qb), F32)
            for h in range(IDX_HEADS):
                sc = sc + iwt[h:h + 1, :] * jnp.maximum(lg[:, h * qb:(h + 1) * qb], 0.0)
            rows = slice(u * LANES, (u + 1) * LANES)
            sc = jnp.where(start + kidx_t <= qpos_t, sc, -jnp.inf)
            sc_ref[j, rows, :] = sc
            hb_ref[j, rows, :] = sc.astype(BF16)
        return carry

    lax.fori_loop(0, ns, score_body, 0)

    def count(src_ref, chunk, hit_of, dtype):
        def body(j, acc):
            for u in range(ks // chunk):
                acc = acc + hit_of(src_ref[j, u * chunk:(u + 1) * chunk, :])
            return acc

        acc = lax.fori_loop(0, ns, body, jnp.zeros((chunk, qb), dtype))
        return jnp.sum(acc.astype(F32), axis=0, keepdims=True)

    def key_value(key):
        val = pltpu.bitcast(key ^ ((key >> 31) & 0x7FFFFFFF), F32)
        return jnp.where(key < KEY_MIN_FINITE, -jnp.inf, val)

    def count_ge(key):
        cand = jnp.broadcast_to(key_value(key), (8, qb))
        return count(sc_ref, 8, lambda sc: jnp.where(sc >= cand, 1.0, 0.0), F32)

    one_b = jnp.ones((16, qb), BF16)
    zero_b = jnp.zeros((16, qb), BF16)

    def coarse_value(key):
        bits = (key ^ ((key >> 31) & 0x7FFFFFFF)) & (-(1 << 16))
        return jnp.where(key < KEY_MIN_FINITE, -jnp.inf, pltpu.bitcast(bits, F32))

    def count_coarse(hit_of):
        return count(hb_ref, 16, lambda hb: jnp.where(hit_of(hb), one_b, zero_b), BF16)

    def coarse_body(bi, tau):
        key = tau ^ lax.shift_left(jnp.int32(1), 31 - bi)
        cand = jnp.broadcast_to(coarse_value(key), (16, qb)).astype(BF16)
        return jnp.where(count_coarse(lambda hb: hb >= cand) >= n_sel, key, tau)

    searching = (i + 1) * qb > n_sel
    coarse = lax.fori_loop(0, jnp.where(searching, 16, 0), coarse_body, jnp.full((1, qb), INT_MIN, I32))
    step = 1 << 16
    centre = jnp.where(coarse < 0, coarse | (step - 1), coarse)
    lo0 = jnp.maximum(centre, INT_MIN + step) - (step >> 1)
    hi0 = centre + step
    at_zero = ((count_coarse(lambda hb: hb > zero_b) < n_sel) & (count_coarse(lambda hb: hb >= zero_b) >= n_sel))
    lo0 = jnp.where(at_zero, 0, lo0)
    hi0 = jnp.where(at_zero, 1, hi0)
    max_steps = jnp.where(searching, 20, 0)

    def fine_cond(state):
        it, _, _, _, open_rows = state
        return jnp.logical_and(it < max_steps, open_rows > 0.0)

    def fine_body(state):
        it, lo, hi, c_lo, _ = state
        settled = (c_lo == n_sel) | (hi - lo <= 1)
        open_rows = jnp.max(jnp.where(settled, 0.0, 1.0))
        mid = lo + ((hi - lo) >> 1)
        cnt = count_ge(mid)
        ok = cnt >= n_sel
        return (it + 1, jnp.where(ok, mid, lo), jnp.where(ok, hi, mid), jnp.where(ok, cnt, c_lo), open_rows)

    _, tau, _, _, _ = lax.while_loop(
        fine_cond, fine_body, (jnp.int32(0), lo0, hi0, jnp.full((1, qb), -1.0, F32), jnp.float32(1.0)))
    tau_val = key_value(tau)
    tau8 = jnp.broadcast_to(tau_val, (8, qb))
    need = n_sel - count(sc_ref, 8, lambda sc: jnp.where(sc > tau8, 1.0, 0.0), F32)

    q = q_ref[...]
    qs = [jnp.concatenate([q[:, (g * rep + r) * LANES:(g * rep + r + 1) * LANES] for r in range(rep)], axis=0)
          for g in range(B_KV_HEADS)]
    mx_ref[...] = jnp.full(mx_ref.shape, NEG_BIG, F32)
    acc_ref[...] = jnp.zeros(acc_ref.shape, F32)
    ltri = ltri_ref[...]
    taub = jnp.broadcast_to(tau_val, (ks, qb))

    def att_body(j, eq_seen):
        start = pl.multiple_of(j * ks, ks)
        sct = sc_ref[j]
        eqf = jnp.where(sct == taub, 1.0, 0.0)
        pref = _dot(ltri, eqf.astype(BF16)) + eq_seen
        take = jnp.where(sct > taub, 1.0, jnp.where(pref <= need, eqf, 0.0))
        bias = jnp.where((take > 0.0) & (j * ks + kidx <= qpos), 0.0, 2.0 * NEG_BIG)
        alphas = []
        for g in range(B_KV_HEADS):
            kx = kx_ref[g, pl.ds(start, ks), :]
            for r in range(rep):
                cols = slice(r * qb, (r + 1) * qb)
                m_old = mx_ref[g, :, cols]
                m_new = m_old
                for kc in range(ks // LANES):
                    rows = slice(kc * LANES, (kc + 1) * LANES)
                    s = _dot_nt(kx[rows], qs[g][cols]) + bias[rows]
                    s_ref[g, rows, cols] = s
                    m_new = jnp.maximum(m_new, jnp.max(s, axis=0, keepdims=True))
                mx_ref[g, :, cols] = m_new
                alphas.append(jnp.exp2(m_old - m_new)[0:1, :])
        half = ks // 2
        for g in range(B_KV_HEADS):
            vxt = vxt_ref[g, :, pl.ds(start, ks)]
            for r in range(rep):
                cols = slice(r * qb, (r + 1) * qb)
                m_new = mx_ref[g, 0:1, cols]
                acc = alphas[g * rep + r] * acc_ref[g, :, cols]
                for kk in range(2):
                    rows = slice(kk * half, (kk + 1) * half)
                    p = jnp.exp2(s_ref[g, rows, cols] - m_new).astype(BF16)
                    acc = acc + _dot(vxt[:, rows], p)
                acc_ref[g, :, cols] = acc
        return eq_seen + jnp.sum(eqf, axis=0, keepdims=True)

    lax.fori_loop(0, ns, att_body, jnp.zeros((1, qb), F32))

    lane = lax.broadcasted_iota(I32, (qb, LANES), 1)
    outs = []
    for g in range(B_KV_HEADS):
        acc = acc_ref[g]
        a = jnp.concatenate([jnp.transpose(acc[:, r * qb:(r + 1) * qb]) for r in range(rep)], axis=0)
        outs.append(a / a[:, B_HD:B_HD + 1])
    for cblk in range(B_WIDTH // LANES):
        g = (2 * cblk) // rep
        r0 = (2 * cblk) % rep
        a = outs[g][r0 * qb:(r0 + 1) * qb]
        b = pltpu.roll(outs[g][(r0 + 1) * qb:(r0 + 2) * qb], B_HD, axis=1)
        z = bz_ref[:, cblk * LANES:(cblk + 1) * LANES]
        o_ref[:, cblk * LANES:(cblk + 1) * LANES] = jnp.where(lane < B_HD, a, b) * _silu(z)


def _dsa(bqx, kx, vxt, iq, ik, small_t, bz, batch, seq, qb, ks):
    nq = seq // qb
    n_sel = min(TOPK_MAX, seq // 4)
    rep = B_HEADS // B_KV_HEADS
    ltri = (jnp.arange(ks)[None, :] <= jnp.arange(ks)[:, None]).astype(BF16)
    row = lambda w: pl.BlockSpec((qb, w), lambda b, i: (b * nq + i, 0))
    per_batch = lambda w: pl.BlockSpec((seq, w), lambda b, i: (b, 0))
    return pl.pallas_call(
        functools.partial(_dsa_kernel, qb=qb, ks=ks, n_sel=n_sel),
        grid=(batch, nq),
        in_specs=[row(B_HEADS * LANES),
                  pl.BlockSpec((B_KV_HEADS, seq, LANES), lambda b, i: (0, b, 0)),
                  pl.BlockSpec((B_KV_HEADS, LANES, seq), lambda b, i: (0, 0, b)),
                  row(IDX_WIDTH), per_batch(IDX_DIM),
                  pl.BlockSpec((1, 16, qb), lambda b, i: (b, 0, i)),
                  row(B_WIDTH), pl.BlockSpec((ks, ks), lambda b, i: (0, 0))],
        out_specs=row(B_WIDTH),
        out_shape=jax.ShapeDtypeStruct((batch * seq, B_WIDTH), F32),
        scratch_shapes=[pltpu.VMEM((seq // ks, ks, qb), F32),
                        pltpu.VMEM((seq // ks, ks, qb), BF16),
                        pltpu.VMEM((B_KV_HEADS, 8, rep * qb), F32),
                        pltpu.VMEM((B_KV_HEADS, LANES, rep * qb), F32),
                        pltpu.VMEM((B_KV_HEADS, ks, rep * qb), F32)],
        compiler_params=pltpu.CompilerParams(dimension_semantics=("arbitrary", "arbitrary"),
                                             vmem_limit_bytes=VMEM_LIMIT),
        name="dsa",
    )(bqx, kx, vxt, iq, ik, small_t, bz, ltri)


def _out_kernel(x_ref, oa_ref, ob_ref, p_ref, woa_ref, wob_ref, wp_ref, wg_ref, gn_ref, bg_ref, o_ref):
    x1 = (x_ref[...] + _dot(oa_ref[...].astype(BF16), woa_ref[...])
          + _dot(ob_ref[...].astype(BF16), wob_ref[...]))
    hn = x1 * lax.rsqrt(jnp.mean(x1 * x1, axis=-1, keepdims=True) + EPS) * gn_ref[...]
    gate = jax.nn.sigmoid(_dot(hn.astype(BF16), wg_ref[...]) + bg_ref[...])
    o_ref[...] = x1 + _dot(p_ref[...].astype(BF16), wp_ref[...]) * gate


def _output(x2, oa, ob, p2, w_out, w_ple, gate_norm_w, w_gate, b_gate, tm):
    t, d = x2.shape
    woa = w_out[:A_WIDTH].astype(BF16)
    wob = w_out[A_WIDTH:].astype(BF16)
    wp = w_ple.astype(BF16)
    wg = w_gate.astype(BF16)
    gn = gate_norm_w[None, :]
    bg = b_gate[None, :]
    row = lambda w: pl.BlockSpec((tm, w), lambda i: (i, 0))
    full = lambda a: pl.BlockSpec(a.shape, lambda i: (0, 0))
    return pl.pallas_call(
        _out_kernel,
        grid=(t // tm,),
        in_specs=[row(d), row(A_WIDTH), row(B_WIDTH), row(PLE_DIM)] + [full(a) for a in (woa, wob, wp, wg, gn, bg)],
        out_specs=row(d),
        out_shape=jax.ShapeDtypeStruct((t, d), F32),
        compiler_params=pltpu.CompilerParams(dimension_semantics=("arbitrary",), vmem_limit_bytes=VMEM_LIMIT),
        name="out",
    )(x2, oa, ob, p2, woa, wob, wp, wg, gn, bg)


def kernel(x, p, attn_norm_w, w_in, conv_w, a_log, dt_bias, a_out_norm_w, b_q_norm_w, b_k_norm_w, w_out,
           w_ple, ple_gate_norm_w, w_ple_gate, b_ple_gate):
    batch, seq, d = x.shape
    t = batch * seq
    tm = min(512, t)
    bt = min(256, seq)
    qb = min(256, seq)
    ks = min(512, seq)
    x2 = x.reshape(t, d)
    for i in range(w_in.shape[0]):
        qkv, az, bqx, kx, vx, bz, iq, ik, small = _project(x2, attn_norm_w[i], w_in[i], b_q_norm_w[i],
                                                           b_k_norm_w[i], tm)
        small_t = jnp.swapaxes(small.reshape(batch, seq, SMALL_W)[:, :, :16], 1, 2)
        oa = _gdn(qkv, small, small_t, az, conv_w[i], a_log[i], dt_bias[i], a_out_norm_w[i], batch, seq, bt)
        ob = _dsa(bqx, kx, jnp.swapaxes(vx, 1, 2), iq, ik, small_t, bz, batch, seq, qb, ks)
        x2 = _output(x2, oa, ob, p[i].reshape(t, PLE_DIM), w_out[i], w_ple[i], ple_gate_norm_w[i],
                     w_ple_gate[i], b_ple_gate[i], tm)
    return x2.reshape(batch, seq, d)
```

```python
import functools

import jax
import jax.numpy as jnp
from jax import lax
from jax.experimental import pallas as pl
from jax.experimental.pallas import tpu as pltpu

F32 = jnp.float32
BF16 = jnp.bfloat16
I32 = jnp.int32
EPS = 1e-6
HI = lax.Precision.HIGHEST

PLE_DIM = 256
A_HEADS = 4
A_DK = 128
A_DV = 128
A_CONV = 4
A_CHUNK = 64
A_WIDTH = A_HEADS * A_DV
A_QKV = 2 * A_HEADS * A_DK + A_WIDTH
B_HEADS = 8
B_KV_HEADS = 2
B_HD = 64
B_WIDTH = B_HEADS * B_HD
B_KV_WIDTH = B_KV_HEADS * B_HD
IDX_HEADS = 8
IDX_DIM = 128
IDX_WIDTH = IDX_HEADS * IDX_DIM
TOPK_MAX = 256
LANES = 128
SMALL_W = LANES
SM_BETA = 0
SM_DECAY = A_HEADS
SM_IW = 2 * A_HEADS
V_ROWS = B_HD + 16

VMEM_LIMIT = 56 * 1024 * 1024
NEG_BIG = -1e30
LOG2_E = 1.4426950408889634
INT_MIN = -(2 ** 31)
KEY_MIN_FINITE = INT_MIN + 0x00800000


def _dot(a, b, prec=None):
    return jnp.dot(a, b, preferred_element_type=F32, precision=prec)


def _dot_nt(a, b, prec=None):
    return lax.dot_general(a, b, (((1,), (1,)), ((), ())), preferred_element_type=F32, precision=prec)


def _dot_tn(a, b, prec=None):
    return lax.dot_general(a, b, (((0,), (0,)), ((), ())), preferred_element_type=F32, precision=prec)


def _silu(x):
    return x * jax.nn.sigmoid(x)


def _softplus(x):
    return jnp.maximum(x, 0.0) + jnp.log1p(jnp.exp(-jnp.abs(x)))


def _seg_norm64(xb, gain_row):
    lane = lax.broadcasted_iota(I32, xb.shape, 1)
    lo = lane < B_HD
    sq = xb * xb
    s_lo = jnp.sum(jnp.where(lo, sq, 0.0), axis=-1, keepdims=True)
    s_hi = jnp.sum(jnp.where(lo, 0.0, sq), axis=-1, keepdims=True)
    ms = jnp.where(lo, s_lo, s_hi) * (1.0 / B_HD)
    return xb * lax.rsqrt(ms + EPS) * gain_row


_WB_QKV = 0
_WB_AZ = _WB_QKV + A_QKV
_WB_BQ = _WB_AZ + A_WIDTH
_WB_BK = _WB_BQ + B_WIDTH
_WB_BV = _WB_BK + B_KV_WIDTH
_WB_BZ = _WB_BV + B_KV_WIDTH
_WB_IQ = _WB_BZ + B_WIDTH
_WB_IK = _WB_IQ + IDX_WIDTH
_WB_SM = _WB_IK + IDX_DIM
_WB_END = _WB_SM + SMALL_W


def _proj_kernel(x_ref, nw_ref, wb_ref, qg_ref, kg_ref,
                 qkv_ref, az_ref, bqx_ref, kx_ref, vxt_ref, bz_ref, iq_ref, ik_ref, sm_ref):
    x = x_ref[...]
    h = x * lax.rsqrt(jnp.mean(x * x, axis=-1, keepdims=True) + EPS) * nw_ref[...]
    hb = h.astype(BF16)
    step = 512
    for c0 in range(0, A_QKV, step):
        qkv_ref[:, c0:c0 + step] = _dot(hb, wb_ref[:, _WB_QKV + c0:_WB_QKV + c0 + step])
    az_ref[...] = _dot(hb, wb_ref[:, _WB_AZ:_WB_BQ])
    bz_ref[...] = _dot(hb, wb_ref[:, _WB_BZ:_WB_IQ])
    lane = lax.broadcasted_iota(I32, (x.shape[0], LANES), 1)
    lo = lane < B_HD
    bk = _seg_norm64(_dot(hb, wb_ref[:, _WB_BK:_WB_BV]), kg_ref[...])
    bv = _dot(hb, wb_ref[:, _WB_BV:_WB_BZ])
    k_tail = jnp.where(lane == B_HD, 1.0, 0.0)
    for g in range(B_KV_HEADS):
        kg = bk if g == 0 else pltpu.roll(bk, B_HD, axis=1)
        vg = bv if g == 0 else pltpu.roll(bv, B_HD, axis=1)
        kx_ref[g] = jnp.where(lo, kg, k_tail).astype(BF16)
        vxt_ref[g] = jnp.transpose(jnp.where(lo, vg, 1.0))[0:V_ROWS].astype(BF16)
    bq = _dot(hb, wb_ref[:, _WB_BQ:_WB_BK])
    scale = B_HD ** -0.5 * LOG2_E
    for c0 in range(0, B_WIDTH, LANES):
        nb = _seg_norm64(bq[:, c0:c0 + LANES], qg_ref[...]) * scale
        bqx_ref[:, 2 * c0:2 * c0 + LANES] = jnp.where(lo, nb, 0.0).astype(BF16)
        bqx_ref[:, 2 * c0 + LANES:2 * c0 + 2 * LANES] = jnp.where(
            lo, pltpu.roll(nb, B_HD, axis=1), 0.0).astype(BF16)
    for c0 in range(0, IDX_WIDTH, step):
        iq_ref[:, c0:c0 + step] = _dot(hb, wb_ref[:, _WB_IQ + c0:_WB_IQ + c0 + step]).astype(BF16)
    ik_ref[...] = _dot(hb, wb_ref[:, _WB_IK:_WB_SM]).astype(BF16)
    sm_ref[...] = _dot(hb, wb_ref[:, _WB_SM:_WB_END])


def _project(x2, norm_w, w_in, q_gain, k_gain, tm):
    t, d = x2.shape
    sizes = (A_HEADS * A_DK, A_HEADS * A_DK, A_WIDTH, A_WIDTH, A_HEADS, A_HEADS, B_WIDTH, B_KV_WIDTH,
             B_KV_WIDTH, B_WIDTH, IDX_WIDTH, IDX_DIM, IDX_HEADS)
    offs = [0]
    for s in sizes:
        offs.append(offs[-1] + s)
    (a_q, a_k, a_v, a_z, a_b, a_a, b_q, b_k, b_v, b_z, i_q, i_k, i_w) = [
        w_in[:, offs[n]:offs[n + 1]] for n in range(len(sizes))]
    pad = jnp.zeros((d, SMALL_W - 2 * A_HEADS - IDX_HEADS), F32)
    wb = jnp.concatenate([a_q, a_k, a_v, a_z, b_q, b_k, b_v, b_z, i_q, i_k, a_b, a_a, i_w, pad],
                         axis=1).astype(BF16)
    qg = jnp.tile(q_gain, LANES // B_HD)[None, :]
    kg = jnp.tile(k_gain, LANES // B_HD)[None, :]
    row = lambda w: pl.BlockSpec((tm, w), lambda i: (i, 0))
    grp = pl.BlockSpec((B_KV_HEADS, tm, LANES), lambda i: (0, i, 0))
    full = lambda a: pl.BlockSpec(a.shape, lambda i: (0, 0))
    nw = norm_w[None, :]
    sds = jax.ShapeDtypeStruct
    return pl.pallas_call(
        _proj_kernel,
        grid=(t // tm,),
        in_specs=[row(d), full(nw), full(wb), full(qg), full(kg)],
        out_specs=[row(A_QKV), row(A_WIDTH), row(B_HEADS * LANES), grp,
                   pl.BlockSpec((B_KV_HEADS, V_ROWS, tm), lambda i: (0, 0, i)), row(B_WIDTH), row(IDX_WIDTH),
                   row(IDX_DIM), row(SMALL_W)],
        out_shape=[sds((t, A_QKV), F32), sds((t, A_WIDTH), F32), sds((t, B_HEADS * LANES), BF16),
                   sds((B_KV_HEADS, t, LANES), BF16), sds((B_KV_HEADS, V_ROWS, t), BF16), sds((t, B_WIDTH), F32),
                   sds((t, IDX_WIDTH), BF16), sds((t, IDX_DIM), BF16), sds((t, SMALL_W), F32)],
        compiler_params=pltpu.CompilerParams(dimension_semantics=("arbitrary",), vmem_limit_bytes=VMEM_LIMIT),
        name="proj",
    )(x2, nw, wb, qg, kg)


def _gdn_kernel(qkv_ref, sm_ref, smt_ref, az_ref, cw_ref, alog_r_ref, dtb_r_ref, alog_c_ref, dtb_c_ref,
                onw_ref, o_ref, s_ref, tail_ref, *, bt):
    c = A_CHUNK

    @pl.when(pl.program_id(1) == 0)
    def _():
        s_ref[...] = jnp.zeros(s_ref.shape, F32)
        tail_ref[...] = jnp.zeros(tail_ref.shape, F32)

    xin = qkv_ref[...]
    tail = tail_ref[...]
    cw = cw_ref[...]
    row8 = lax.broadcasted_iota(I32, tail.shape, 0)
    acc = xin * cw[A_CONV - 1:A_CONV, :]
    for d in range(1, A_CONV):
        xr = pltpu.roll(xin, d, axis=0)
        pr = pltpu.roll(tail, d, axis=0)
        head = jnp.where(row8 < d, pr, xr[0:8])
        xs = jnp.concatenate([head, xr[8:]], axis=0)
        acc = acc + xs * cw[A_CONV - 1 - d:A_CONV - d, :]
    tail_ref[...] = xin[bt - 8:bt]
    qkv = _silu(acc)

    sm = sm_ref[...]
    smt = smt_ref[0]
    g_col = -jnp.exp(alog_r_ref[...]) * _softplus(sm + dtb_r_ref[...])
    g_row = -jnp.exp(alog_c_ref[...]) * _softplus(smt + dtb_c_ref[...])
    beta_all = jax.nn.sigmoid(sm)

    r = lax.broadcasted_iota(I32, (bt, bt), 0)
    s = lax.broadcasted_iota(I32, (bt, bt), 1)
    same = (r // c) == (s // c)
    tril = same & (s <= r)
    strict = same & (s < r)
    eye = jnp.where(s == r, 1.0, 0.0).astype(F32)
    gc_col = _dot(jnp.where(tril, 1.0, 0.0).astype(F32), g_col, HI)
    gc_row = _dot(g_row, jnp.where(same & (r <= s), 1.0, 0.0).astype(F32), HI)
    gtot = _dot(jnp.where(same, 1.0, 0.0).astype(F32), g_col, HI)
    egc_all = jnp.exp(gc_col)
    ekd_all = jnp.exp(gtot - gc_col)
    egl_all = jnp.exp(gtot)
    onw = onw_ref[...]

    heads = range(A_HEADS)
    qs, ks_, kbs, decays, amats = [], [], [], [], []
    for h in heads:
        q = qkv[:, h * A_DK:(h + 1) * A_DK]
        k = qkv[:, A_HEADS * A_DK + h * A_DK:A_HEADS * A_DK + (h + 1) * A_DK]
        q = q * lax.rsqrt(jnp.sum(q * q, axis=-1, keepdims=True) + EPS) * (A_DK ** -0.5)
        k = k * lax.rsqrt(jnp.sum(k * k, axis=-1, keepdims=True) + EPS)
        lane = SM_DECAY + h
        gcc = gc_col[:, lane:lane + 1]
        gcr = gc_row[lane:lane + 1, :]
        decay = jnp.where(tril, jnp.exp(jnp.where(tril, gcc - gcr, 0.0)), 0.0)
        kb = k * beta_all[:, SM_BETA + h:SM_BETA + h + 1]
        qs.append(q)
        ks_.append(k)
        kbs.append(kb)
        decays.append(decay)
        amats.append(jnp.where(strict, _dot_nt(kb.astype(BF16), k.astype(BF16)) * decay, 0.0))
    tmats = [eye - a for a in amats]
    apows = list(amats)
    n = 2
    while n < c:
        for h in heads:
            a_bf = apows[h].astype(BF16)
            apows[h] = _dot(a_bf, a_bf)
        for h in heads:
            tmats[h] = tmats[h] + _dot(tmats[h].astype(BF16), apows[h].astype(BF16))
        n *= 2
    values, kcds, atts, qds, kds, egls = [], [], [], [], [], []
    for h in heads:
        lane = SM_DECAY + h
        v = qkv[:, 2 * A_HEADS * A_DK + h * A_DV:2 * A_HEADS * A_DK + (h + 1) * A_DV]
        t_bf = tmats[h].astype(BF16)
        egc = egc_all[:, lane:lane + 1]
        values.append(_dot(t_bf, (v * beta_all[:, SM_BETA + h:SM_BETA + h + 1]).astype(BF16)))
        kcds.append(_dot(t_bf, (kbs[h] * egc).astype(BF16)).astype(BF16))
        atts.append((_dot_nt(qs[h].astype(BF16), ks_[h].astype(BF16)) * decays[h]).astype(BF16))
        qds.append((qs[h] * egc).astype(BF16))
        kds.append((ks_[h] * ekd_all[:, lane:lane + 1]).astype(BF16))
        egls.append(egl_all[:, lane:lane + 1])
    states = [s_ref[h] for h in heads]
    o_inter = [[] for _ in heads]
    v_new = [[] for _ in heads]
    for ci in range(bt // c):
        rows = slice(ci * c, (ci + 1) * c)
        for h in heads:
            s_bf = states[h].astype(BF16)
            vn = (values[h][rows] - _dot(kcds[h][rows], s_bf)).astype(BF16)
            o_inter[h].append(_dot(qds[h][rows], s_bf))
            v_new[h].append(vn)
            gl = egls[h][ci * c:ci * c + 8]
            states[h] = states[h] * jnp.concatenate([gl] * (A_DK // 8), axis=0) + _dot_tn(kds[h][rows], vn)
    for h in heads:
        s_ref[h] = states[h]
        o = jnp.concatenate(o_inter[h], axis=0) + _dot(atts[h], jnp.concatenate(v_new[h], axis=0))
        on = o * lax.rsqrt(jnp.mean(o * o, axis=-1, keepdims=True) + EPS) * onw
        z = az_ref[:, h * A_DV:(h + 1) * A_DV]
        o_ref[:, h * A_DV:(h + 1) * A_DV] = on * _silu(z)


def _gdn(qkv, small, small_t, a_z, conv_w, a_log, dt_bias, out_norm_w, batch, seq, bt):
    nt = seq // bt
    lane_pad = lambda v: jnp.zeros((1, LANES), F32).at[0, SM_DECAY:SM_DECAY + A_HEADS].set(v)
    sub_pad = lambda v: jnp.zeros((8, 1), F32).at[SM_DECAY:SM_DECAY + A_HEADS, 0].set(v)
    cw = jnp.zeros((8, A_QKV), F32).at[:A_CONV].set(conv_w)
    onw = out_norm_w[None, :]
    row = lambda w: pl.BlockSpec((bt, w), lambda b, t: (b * nt + t, 0))
    full = lambda a: pl.BlockSpec(a.shape, lambda b, t: (0, 0))
    consts = (cw, lane_pad(a_log), lane_pad(dt_bias), sub_pad(a_log), sub_pad(dt_bias), onw)
    return pl.pallas_call(
        functools.partial(_gdn_kernel, bt=bt),
        grid=(batch, nt),
        in_specs=[row(A_QKV), row(SMALL_W), pl.BlockSpec((1, 8, bt), lambda b, t: (b, 0, t)), row(A_WIDTH)]
        + [full(a) for a in consts],
        out_specs=row(A_WIDTH),
        out_shape=jax.ShapeDtypeStruct((batch * seq, A_WIDTH), F32),
        scratch_shapes=[pltpu.VMEM((A_HEADS, A_DK, A_DV), F32), pltpu.VMEM((8, A_QKV), F32)],
        compiler_params=pltpu.CompilerParams(dimension_semantics=("arbitrary", "arbitrary"),
                                             vmem_limit_bytes=VMEM_LIMIT),
        name="gdn",
    )(qkv, small, small_t, a_z, *consts)


def _dsa_kernel(q_ref, kx_ref, vxt_ref, iq_ref, ik_ref, smt_ref, bz_ref, ltri_ref, o_ref,
                sc_ref, hb_ref, mx_ref, acc_ref, s_ref, *, qb, ks, n_sel):
    rep = B_HEADS // B_KV_HEADS
    i = pl.program_id(1)
    ns = (i * qb) // ks + 1
    kidx = lax.broadcasted_iota(I32, (ks, qb), 0)
    qpos = i * qb + lax.broadcasted_iota(I32, (ks, qb), 1)
    kidx_t = lax.broadcasted_iota(I32, (LANES, qb), 0)
    qpos_t = i * qb + lax.broadcasted_iota(I32, (LANES, qb), 1)

    iq = iq_ref[...]
    iqs = jnp.concatenate([iq[:, h * IDX_DIM:(h + 1) * IDX_DIM] for h in range(IDX_HEADS)], axis=0)
    iwt = smt_ref[0][SM_IW:SM_IW + IDX_HEADS, :] * (IDX_HEADS ** -0.5 * IDX_DIM ** -0.5)

    def score_body(j, carry):
        for u in range(ks // LANES):
            start = pl.multiple_of(j * ks + u * LANES, LANES)
            lg = _dot_nt(ik_ref[pl.ds(start, LANES), :], iqs)
            sc = jnp.zeros((LANES, qb), F32)
            for h in range(IDX_HEADS):
                sc = sc + iwt[h:h + 1, :] * jnp.maximum(lg[:, h * qb:(h + 1) * qb], 0.0)
            rows = slice(u * LANES, (u + 1) * LANES)
            sc = jnp.where(start + kidx_t <= qpos_t, sc, -jnp.inf)
            sc_ref[j, rows, :] = sc
            hb_ref[j, rows, :] = sc.astype(BF16)
        return carry

    lax.fori_loop(0, ns, score_body, 0)

    def count(src_ref, chunk, hit_of, dtype):
        def body(j, acc):
            for u in range(ks // chunk):
                acc = acc + hit_of(src_ref[j, u * chunk:(u + 1) * chunk, :])
            return acc

        acc = lax.fori_loop(0, ns, body, jnp.zeros((chunk, qb), dtype))
        return jnp.sum(acc.astype(F32), axis=0, keepdims=True)

    def key_value(key):
        val = pltpu.bitcast(key ^ ((key >> 31) & 0x7FFFFFFF), F32)
        return jnp.where(key < KEY_MIN_FINITE, -jnp.inf, val)

    def count_ge(key):
        cand = jnp.broadcast_to(key_value(key), (8, qb))
        return count(sc_ref, 8, lambda sc: jnp.where(sc >= cand, 1.0, 0.0), F32)

    one_b = jnp.ones((16, qb), BF16)
    zero_b = jnp.zeros((16, qb), BF16)

    def coarse_value(key):
        bits = (key ^ ((key >> 31) & 0x7FFFFFFF)) & (-(1 << 16))
        return jnp.where(key < KEY_MIN_FINITE, -jnp.inf, pltpu.bitcast(bits, F32))

    def count_coarse(hit_of):
        return count(hb_ref, 16, lambda hb: jnp.where(hit_of(hb), one_b, zero_b), BF16)

    def coarse_body(bi, tau):
        key = tau ^ lax.shift_left(jnp.int32(1), 31 - bi)
        cand = jnp.broadcast_to(coarse_value(key), (16, qb)).astype(BF16)
        return jnp.where(count_coarse(lambda hb: hb >= cand) >= n_sel, key, tau)

    searching = (i + 1) * qb > n_sel
    coarse = lax.fori_loop(0, jnp.where(searching, 16, 0), coarse_body, jnp.full((1, qb), INT_MIN, I32))
    step = 1 << 16
    centre = jnp.where(coarse < 0, coarse | (step - 1), coarse)
    lo0 = jnp.maximum(centre, INT_MIN + step) - (step >> 1)
    hi0 = centre + step
    at_zero = ((count_coarse(lambda hb: hb > zero_b) < n_sel) & (count_coarse(lambda hb: hb >= zero_b) >= n_sel))
    lo0 = jnp.where(at_zero, 0, lo0)
    hi0 = jnp.where(at_zero, 1, hi0)
    max_steps = jnp.where(searching, 20, 0)

    def fine_cond(state):
        it, _, _, _, open_rows = state
        return jnp.logical_and(it < max_steps, open_rows > 0.0)

    def fine_body(state):
        it, lo, hi, c_lo, _ = state
        settled = (c_lo == n_sel) | (hi - lo <= 1)
        open_rows = jnp.max(jnp.where(settled, 0.0, 1.0))
        mid = lo + ((hi - lo) >> 1)
        cnt = count_ge(mid)
        ok = cnt >= n_sel
        return (it + 1, jnp.where(ok, mid, lo), jnp.where(ok, hi, mid), jnp.where(ok, cnt, c_lo), open_rows)

    _, tau, _, _, _ = lax.while_loop(
        fine_cond, fine_body, (jnp.int32(0), lo0, hi0, jnp.full((1, qb), -1.0, F32), jnp.float32(1.0)))
    tau_val = key_value(tau)
    tau8 = jnp.broadcast_to(tau_val, (8, qb))
    need = n_sel - count(sc_ref, 8, lambda sc: jnp.where(sc > tau8, 1.0, 0.0), F32)

    q = q_ref[...]
    qs = [jnp.concatenate([q[:, (g * rep + r) * LANES:(g * rep + r + 1) * LANES] for r in range(rep)], axis=0)
          for g in range(B_KV_HEADS)]
    mx_ref[...] = jnp.full(mx_ref.shape, NEG_BIG, F32)
    acc_ref[...] = jnp.zeros(acc_ref.shape, F32)
    ltri = ltri_ref[...]
    taub = jnp.broadcast_to(tau_val, (ks, qb))

    def att_body(j, eq_seen):
        start = pl.multiple_of(j * ks, ks)
        sct = sc_ref[j]
        eqf = jnp.where(sct == taub, 1.0, 0.0)
        pref = _dot(ltri, eqf.astype(BF16)) + eq_seen
        take = jnp.where(sct > taub, 1.0, jnp.where(pref <= need, eqf, 0.0))
        bias = jnp.where((take > 0.0) & (j * ks + kidx <= qpos), 0.0, 2.0 * NEG_BIG)
        alphas = []
        for g in range(B_KV_HEADS):
            kx = kx_ref[g, pl.ds(start, ks), :]
            for r in range(rep):
                cols = slice(r * qb, (r + 1) * qb)
                m_old = mx_ref[g, :, cols]
                m_new = m_old
                for kc in range(ks // LANES):
                    rows = slice(kc * LANES, (kc + 1) * LANES)
                    s = _dot_nt(kx[rows], qs[g][cols]) + bias[rows]
                    s_ref[g, rows, cols] = s
                    m_new = jnp.maximum(m_new, jnp.max(s, axis=0, keepdims=True))
                mx_ref[g, :, cols] = m_new
                alphas.append(jnp.exp2(m_old - m_new)[0:1, :])
        half = ks // 2
        for g in range(B_KV_HEADS):
            vxt = vxt_ref[g, :, pl.ds(start, ks)]
            for r in range(rep):
                cols = slice(r * qb, (r + 1) * qb)
                m_new = mx_ref[g, 0:1, cols]
                acc = alphas[g * rep + r] * acc_ref[g, :, cols]
                for kk in range(2):
                    rows = slice(kk * half, (kk + 1) * half)
                    p = jnp.exp2(s_ref[g, rows, cols] - m_new).astype(BF16)
                    acc = acc + _dot(vxt[:, rows], p)
                acc_ref[g, :, cols] = acc
        return eq_seen + jnp.sum(eqf, axis=0, keepdims=True)

    lax.fori_loop(0, ns, att_body, jnp.zeros((1, qb), F32))

    lane = lax.broadcasted_iota(I32, (qb, LANES), 1)
    outs = []
    for g in range(B_KV_HEADS):
        acc = acc_ref[g]
        acc = jnp.concatenate([acc, jnp.zeros((LANES - V_ROWS, rep * qb), F32)], axis=0)
        a = jnp.concatenate([jnp.transpose(acc[:, r * qb:(r + 1) * qb]) for r in range(rep)], axis=0)
        outs.append(a / a[:, B_HD:B_HD + 1])
    for cblk in range(B_WIDTH // LANES):
        g = (2 * cblk) // rep
        r0 = (2 * cblk) % rep
        a = outs[g][r0 * qb:(r0 + 1) * qb]
        b = pltpu.roll(outs[g][(r0 + 1) * qb:(r0 + 2) * qb], B_HD, axis=1)
        z = bz_ref[:, cblk * LANES:(cblk + 1) * LANES]
        o_ref[:, cblk * LANES:(cblk + 1) * LANES] = jnp.where(lane < B_HD, a, b) * _silu(z)


def _dsa(bqx, kx, vxt, iq, ik, small_t, bz, batch, seq, qb, ks):
    nq = seq // qb
    n_sel = min(TOPK_MAX, seq // 4)
    rep = B_HEADS // B_KV_HEADS
    ltri = (jnp.arange(ks)[None, :] <= jnp.arange(ks)[:, None]).astype(BF16)
    row = lambda w: pl.BlockSpec((qb, w), lambda b, i: (b * nq + i, 0))
    per_batch = lambda w: pl.BlockSpec((seq, w), lambda b, i: (b, 0))
    return pl.pallas_call(
        functools.partial(_dsa_kernel, qb=qb, ks=ks, n_sel=n_sel),
        grid=(batch, nq),
        in_specs=[row(B_HEADS * LANES),
                  pl.BlockSpec((B_KV_HEADS, seq, LANES), lambda b, i: (0, b, 0)),
                  pl.BlockSpec((B_KV_HEADS, V_ROWS, seq), lambda b, i: (0, 0, b)),
                  row(IDX_WIDTH), per_batch(IDX_DIM),
                  pl.BlockSpec((1, 16, qb), lambda b, i: (b, 0, i)),
                  row(B_WIDTH), pl.BlockSpec((ks, ks), lambda b, i: (0, 0))],
        out_specs=row(B_WIDTH),
        out_shape=jax.ShapeDtypeStruct((batch * seq, B_WIDTH), F32),
        scratch_shapes=[pltpu.VMEM((seq // ks, ks, qb), F32),
                        pltpu.VMEM((seq // ks, ks, qb), BF16),
                        pltpu.VMEM((B_KV_HEADS, 8, rep * qb), F32),
                        pltpu.VMEM((B_KV_HEADS, V_ROWS, rep * qb), F32),
                        pltpu.VMEM((B_KV_HEADS, ks, rep * qb), F32)],
        compiler_params=pltpu.CompilerParams(dimension_semantics=("arbitrary", "arbitrary"),
                                             vmem_limit_bytes=VMEM_LIMIT),
        name="dsa",
    )(bqx, kx, vxt, iq, ik, small_t, bz, ltri)


def _out_kernel(x_ref, oa_ref, ob_ref, p_ref, wo_ref, wp_ref, wg_ref, gn_ref, bg_ref, o_ref):
    x1 = (x_ref[...] + _dot(oa_ref[...].astype(BF16), wo_ref[0:A_WIDTH, :])
          + _dot(ob_ref[...].astype(BF16), wo_ref[A_WIDTH:A_WIDTH + B_WIDTH, :]))
    hn = x1 * lax.rsqrt(jnp.mean(x1 * x1, axis=-1, keepdims=True) + EPS) * gn_ref[...]
    gate = jax.nn.sigmoid(_dot(hn.astype(BF16), wg_ref[...]) + bg_ref[...])
    o_ref[...] = x1 + _dot(p_ref[...].astype(BF16), wp_ref[...]) * gate


def _output(x2, oa, ob, p2, w_out, w_ple, gate_norm_w, w_gate, b_gate, tm):
    t, d = x2.shape
    wo = w_out.astype(BF16)
    wp = w_ple.astype(BF16)
    wg = w_gate.astype(BF16)
    gn = gate_norm_w[None, :]
    bg = b_gate[None, :]
    row = lambda w: pl.BlockSpec((tm, w), lambda i: (i, 0))
    full = lambda a: pl.BlockSpec(a.shape, lambda i: (0, 0))
    return pl.pallas_call(
        _out_kernel,
        grid=(t // tm,),
        in_specs=[row(d), row(A_WIDTH), row(B_WIDTH), row(PLE_DIM)] + [full(a) for a in (wo, wp, wg, gn, bg)],
        out_specs=row(d),
        out_shape=jax.ShapeDtypeStruct((t, d), F32),
        compiler_params=pltpu.CompilerParams(dimension_semantics=("arbitrary",), vmem_limit_bytes=VMEM_LIMIT),
        name="out",
    )(x2, oa, ob, p2, wo, wp, wg, gn, bg)


def kernel(x, p, attn_norm_w, w_in, conv_w, a_log, dt_bias, a_out_norm_w, b_q_norm_w, b_k_norm_w, w_out,
           w_ple, ple_gate_norm_w, w_ple_gate, b_ple_gate):
    batch, seq, d = x.shape
    t = batch * seq
    tm = min(512, t)
    bt = min(256, seq)
    qb = min(256, seq)
    ks = min(512, seq)
    x2 = x.reshape(t, d)
    for i in range(w_in.shape[0]):
        qkv, az, bqx, kx, vxt, bz, iq, ik, small = _project(x2, attn_norm_w[i], w_in[i], b_q_norm_w[i],
                                                           b_k_norm_w[i], tm)
        small_t = jnp.swapaxes(small.reshape(batch, seq, SMALL_W)[:, :, :16], 1, 2)
        oa = _gdn(qkv, small, small_t, az, conv_w[i], a_log[i], dt_bias[i], a_out_norm_w[i], batch, seq, bt)
        ob = _dsa(bqx, kx, vxt, iq, ik, small_t, bz, batch, seq, qb, ks)
        x2 = _output(x2, oa, ob, p[i].reshape(t, PLE_DIM), w_out[i], w_ple[i], ple_gate_norm_w[i],
                     w_ple_gate[i], b_ple_gate[i], tm)
    return x2.reshape(batch, seq, d)
```

```python
import functools

import jax
import jax.numpy as jnp
from jax import lax
from jax.experimental import pallas as pl
from jax.experimental.pallas import tpu as pltpu

F32 = jnp.float32
BF16 = jnp.bfloat16
I32 = jnp.int32
EPS = 1e-6
HI = lax.Precision.HIGHEST

PLE_DIM = 256
A_HEADS = 4
A_DK = 128
A_DV = 128
A_CONV = 4
A_CHUNK = 64
A_WIDTH = A_HEADS * A_DV
A_QKV = 2 * A_HEADS * A_DK + A_WIDTH
B_HEADS = 8
B_KV_HEADS = 2
B_HD = 64
B_WIDTH = B_HEADS * B_HD
B_KV_WIDTH = B_KV_HEADS * B_HD
IDX_HEADS = 8
IDX_DIM = 128
IDX_WIDTH = IDX_HEADS * IDX_DIM
TOPK_MAX = 256
LANES = 128
SMALL_W = LANES
SM_BETA = 0
SM_DECAY = A_HEADS
SM_IW = 2 * A_HEADS
V_ROWS = B_HD + 16

VMEM_LIMIT = 56 * 1024 * 1024
NEG_BIG = -1e30
LOG2_E = 1.4426950408889634
INT_MIN = -(2 ** 31)
KEY_MIN_FINITE = INT_MIN + 0x00800000


def _dot(a, b, prec=None):
    return jnp.dot(a, b, preferred_element_type=F32, precision=prec)


def _dot_nt(a, b, prec=None):
    return lax.dot_general(a, b, (((1,), (1,)), ((), ())), preferred_element_type=F32, precision=prec)


def _dot_tn(a, b, prec=None):
    return lax.dot_general(a, b, (((0,), (0,)), ((), ())), preferred_element_type=F32, precision=prec)


def _silu(x):
    return x * jax.nn.sigmoid(x)


def _softplus(x):
    return jnp.maximum(x, 0.0) + jnp.log1p(jnp.exp(-jnp.abs(x)))


def _seg_norm64(xb, gain_row):
    lane = lax.broadcasted_iota(I32, xb.shape, 1)
    lo = lane < B_HD
    sq = xb * xb
    s_lo = jnp.sum(jnp.where(lo, sq, 0.0), axis=-1, keepdims=True)
    s_hi = jnp.sum(jnp.where(lo, 0.0, sq), axis=-1, keepdims=True)
    ms = jnp.where(lo, s_lo, s_hi) * (1.0 / B_HD)
    return xb * lax.rsqrt(ms + EPS) * gain_row


_WB_QKV = 0
_WB_AZ = _WB_QKV + A_QKV
_WB_BQ = _WB_AZ + A_WIDTH
_WB_BK = _WB_BQ + B_WIDTH
_WB_BV = _WB_BK + B_KV_WIDTH
_WB_BZ = _WB_BV + B_KV_WIDTH
_WB_IQ = _WB_BZ + B_WIDTH
_WB_IK = _WB_IQ + IDX_WIDTH
_WB_SM = _WB_IK + IDX_DIM
_WB_END = _WB_SM + SMALL_W


def _proj_kernel(x_ref, nw_ref, wb_ref, qg_ref, kg_ref,
                 qkv_ref, az_ref, bqx_ref, kx_ref, vxt_ref, bz_ref, iq_ref, ik_ref, sm_ref):
    x = x_ref[...]
    h = x * lax.rsqrt(jnp.mean(x * x, axis=-1, keepdims=True) + EPS) * nw_ref[...]
    hb = h.astype(BF16)
    step = 512
    for c0 in range(0, A_QKV, step):
        qkv_ref[:, c0:c0 + step] = _dot(hb, wb_ref[:, _WB_QKV + c0:_WB_QKV + c0 + step])
    az_ref[...] = _dot(hb, wb_ref[:, _WB_AZ:_WB_BQ])
    bz_ref[...] = _dot(hb, wb_ref[:, _WB_BZ:_WB_IQ])
    lane = lax.broadcasted_iota(I32, (x.shape[0], LANES), 1)
    lo = lane < B_HD
    bk = _seg_norm64(_dot(hb, wb_ref[:, _WB_BK:_WB_BV]), kg_ref[...])
    bv = _dot(hb, wb_ref[:, _WB_BV:_WB_BZ])
    k_tail = jnp.where(lane == B_HD, 1.0, 0.0)
    for g in range(B_KV_HEADS):
        kg = bk if g == 0 else pltpu.roll(bk, B_HD, axis=1)
        vg = bv if g == 0 else pltpu.roll(bv, B_HD, axis=1)
        kx_ref[g] = jnp.where(lo, kg, k_tail).astype(BF16)
        vxt_ref[g] = jnp.transpose(jnp.where(lo, vg, 1.0))[0:V_ROWS].astype(BF16)
    bq = _dot(hb, wb_ref[:, _WB_BQ:_WB_BK])
    scale = B_HD ** -0.5 * LOG2_E
    for c0 in range(0, B_WIDTH, LANES):
        nb = _seg_norm64(bq[:, c0:c0 + LANES], qg_ref[...]) * scale
        bqx_ref[:, 2 * c0:2 * c0 + LANES] = jnp.where(lo, nb, 0.0).astype(BF16)
        bqx_ref[:, 2 * c0 + LANES:2 * c0 + 2 * LANES] = jnp.where(
            lo, pltpu.roll(nb, B_HD, axis=1), 0.0).astype(BF16)
    for c0 in range(0, IDX_WIDTH, step):
        iq_ref[:, c0:c0 + step] = _dot(hb, wb_ref[:, _WB_IQ + c0:_WB_IQ + c0 + step]).astype(BF16)
    ik_ref[...] = _dot(hb, wb_ref[:, _WB_IK:_WB_SM]).astype(BF16)
    sm_ref[...] = _dot(hb, wb_ref[:, _WB_SM:_WB_END])


def _project(x2, norm_w, w_in, q_gain, k_gain, tm):
    t, d = x2.shape
    sizes = (A_HEADS * A_DK, A_HEADS * A_DK, A_WIDTH, A_WIDTH, A_HEADS, A_HEADS, B_WIDTH, B_KV_WIDTH,
             B_KV_WIDTH, B_WIDTH, IDX_WIDTH, IDX_DIM, IDX_HEADS)
    offs = [0]
    for s in sizes:
        offs.append(offs[-1] + s)
    (a_q, a_k, a_v, a_z, a_b, a_a, b_q, b_k, b_v, b_z, i_q, i_k, i_w) = [
        w_in[:, offs[n]:offs[n + 1]] for n in range(len(sizes))]
    pad = jnp.zeros((d, SMALL_W - 2 * A_HEADS - IDX_HEADS), F32)
    wb = jnp.concatenate([a_q, a_k, a_v, a_z, b_q, b_k, b_v, b_z, i_q, i_k, a_b, a_a, i_w, pad],
                         axis=1).astype(BF16)
    qg = jnp.tile(q_gain, LANES // B_HD)[None, :]
    kg = jnp.tile(k_gain, LANES // B_HD)[None, :]
    row = lambda w: pl.BlockSpec((tm, w), lambda i: (i, 0))
    grp = pl.BlockSpec((B_KV_HEADS, tm, LANES), lambda i: (0, i, 0))
    full = lambda a: pl.BlockSpec(a.shape, lambda i: (0, 0))
    nw = norm_w[None, :]
    sds = jax.ShapeDtypeStruct
    return pl.pallas_call(
        _proj_kernel,
        grid=(t // tm,),
        in_specs=[row(d), full(nw), full(wb), full(qg), full(kg)],
        out_specs=[row(A_QKV), row(A_WIDTH), row(B_HEADS * LANES), grp,
                   pl.BlockSpec((B_KV_HEADS, V_ROWS, tm), lambda i: (0, 0, i)), row(B_WIDTH), row(IDX_WIDTH),
                   row(IDX_DIM), row(SMALL_W)],
        out_shape=[sds((t, A_QKV), F32), sds((t, A_WIDTH), F32), sds((t, B_HEADS * LANES), BF16),
                   sds((B_KV_HEADS, t, LANES), BF16), sds((B_KV_HEADS, V_ROWS, t), BF16), sds((t, B_WIDTH), F32),
                   sds((t, IDX_WIDTH), BF16), sds((t, IDX_DIM), BF16), sds((t, SMALL_W), F32)],
        compiler_params=pltpu.CompilerParams(dimension_semantics=("arbitrary",), vmem_limit_bytes=VMEM_LIMIT),
        name="proj",
    )(x2, nw, wb, qg, kg)


def _gdn_kernel(qkv_ref, sm_ref, smt_ref, az_ref, cw_ref, alog_r_ref, dtb_r_ref, alog_c_ref, dtb_c_ref,
                onw_ref, o_ref, s_ref, tail_ref, *, bt):
    c = A_CHUNK

    @pl.when(pl.program_id(1) == 0)
    def _():
        s_ref[...] = jnp.zeros(s_ref.shape, F32)
        tail_ref[...] = jnp.zeros(tail_ref.shape, F32)

    xin = qkv_ref[...]
    tail = tail_ref[...]
    cw = cw_ref[...]
    row8 = lax.broadcasted_iota(I32, tail.shape, 0)
    acc = xin * cw[A_CONV - 1:A_CONV, :]
    for d in range(1, A_CONV):
        xr = pltpu.roll(xin, d, axis=0)
        pr = pltpu.roll(tail, d, axis=0)
        head = jnp.where(row8 < d, pr, xr[0:8])
        xs = jnp.concatenate([head, xr[8:]], axis=0)
        acc = acc + xs * cw[A_CONV - 1 - d:A_CONV - d, :]
    tail_ref[...] = xin[bt - 8:bt]
    qkv = _silu(acc)

    sm = sm_ref[...]
    smt = smt_ref[0]
    g_col = -jnp.exp(alog_r_ref[...]) * _softplus(sm + dtb_r_ref[...])
    g_row = -jnp.exp(alog_c_ref[...]) * _softplus(smt + dtb_c_ref[...])
    beta_all = jax.nn.sigmoid(sm)

    r = lax.broadcasted_iota(I32, (bt, bt), 0)
    s = lax.broadcasted_iota(I32, (bt, bt), 1)
    same = (r // c) == (s // c)
    tril = same & (s <= r)
    strict = same & (s < r)
    eye = jnp.where(s == r, 1.0, 0.0).astype(F32)
    gc_col = _dot(jnp.where(tril, 1.0, 0.0).astype(F32), g_col, HI)
    gc_row = _dot(g_row, jnp.where(same & (r <= s), 1.0, 0.0).astype(F32), HI)
    gtot = _dot(jnp.where(same, 1.0, 0.0).astype(F32), g_col, HI)
    egc_all = jnp.exp(gc_col)
    ekd_all = jnp.exp(gtot - gc_col)
    egl_all = jnp.exp(gtot)
    onw = onw_ref[...]

    heads = range(A_HEADS)
    qs, ks_, kbs, decays, amats = [], [], [], [], []
    for h in heads:
        q = qkv[:, h * A_DK:(h + 1) * A_DK]
        k = qkv[:, A_HEADS * A_DK + h * A_DK:A_HEADS * A_DK + (h + 1) * A_DK]
        q = q * lax.rsqrt(jnp.sum(q * q, axis=-1, keepdims=True) + EPS) * (A_DK ** -0.5)
        k = k * lax.rsqrt(jnp.sum(k * k, axis=-1, keepdims=True) + EPS)
        lane = SM_DECAY + h
        gcc = gc_col[:, lane:lane + 1]
        gcr = gc_row[lane:lane + 1, :]
        decay = jnp.where(tril, jnp.exp(jnp.where(tril, gcc - gcr, 0.0)), 0.0)
        kb = k * beta_all[:, SM_BETA + h:SM_BETA + h + 1]
        qs.append(q)
        ks_.append(k)
        kbs.append(kb)
        decays.append(decay)
        amats.append(jnp.where(strict, _dot_nt(kb.astype(BF16), k.astype(BF16)) * decay, 0.0))
    tmats = [eye - jnp.where((r // 2) == (s // 2), a, 0.0) for a in amats]
    size = 2
    while size < c:
        off = ((r // (2 * size)) == (s // (2 * size))) & ((r // size) != (s // size))
        t_bfs = [tmats[h].astype(BF16) for h in heads]
        tcs = [_dot(t_bfs[h], jnp.where(off, amats[h], 0.0).astype(BF16)).astype(BF16) for h in heads]
        for h in heads:
            tmats[h] = tmats[h] - _dot(tcs[h], t_bfs[h])
        size *= 2
    values, kcds, atts, qds, kds, egls = [], [], [], [], [], []
    for h in heads:
        lane = SM_DECAY + h
        v = qkv[:, 2 * A_HEADS * A_DK + h * A_DV:2 * A_HEADS * A_DK + (h + 1) * A_DV]
        t_bf = tmats[h].astype(BF16)
        egc = egc_all[:, lane:lane + 1]
        values.append(_dot(t_bf, (v * beta_all[:, SM_BETA + h:SM_BETA + h + 1]).astype(BF16)))
        kcds.append(_dot(t_bf, (kbs[h] * egc).astype(BF16)).astype(BF16))
        atts.append((_dot_nt(qs[h].astype(BF16), ks_[h].astype(BF16)) * decays[h]).astype(BF16))
        qds.append((qs[h] * egc).astype(BF16))
        kds.append((ks_[h] * ekd_all[:, lane:lane + 1]).astype(BF16))
        egls.append(egl_all[:, lane:lane + 1])
    states = [s_ref[h] for h in heads]
    o_inter = [[] for _ in heads]
    v_new = [[] for _ in heads]
    for ci in range(bt // c):
        rows = slice(ci * c, (ci + 1) * c)
        for h in heads:
            s_bf = states[h].astype(BF16)
            vn = (values[h][rows] - _dot(kcds[h][rows], s_bf)).astype(BF16)
            o_inter[h].append(_dot(qds[h][rows], s_bf))
            v_new[h].append(vn)
            gl = egls[h][ci * c:ci * c + 8]
            states[h] = states[h] * jnp.concatenate([gl] * (A_DK // 8), axis=0) + _dot_tn(kds[h][rows], vn)
    for h in heads:
        s_ref[h] = states[h]
        o = jnp.concatenate(o_inter[h], axis=0) + _dot(atts[h], jnp.concatenate(v_new[h], axis=0))
        on = o * lax.rsqrt(jnp.mean(o * o, axis=-1, keepdims=True) + EPS) * onw
        z = az_ref[:, h * A_DV:(h + 1) * A_DV]
        o_ref[:, h * A_DV:(h + 1) * A_DV] = on * _silu(z)


def _gdn(qkv, small, small_t, a_z, conv_w, a_log, dt_bias, out_norm_w, batch, seq, bt):
    nt = seq // bt
    lane_pad = lambda v: jnp.zeros((1, LANES), F32).at[0, SM_DECAY:SM_DECAY + A_HEADS].set(v)
    sub_pad = lambda v: jnp.zeros((8, 1), F32).at[SM_DECAY:SM_DECAY + A_HEADS, 0].set(v)
    cw = jnp.zeros((8, A_QKV), F32).at[:A_CONV].set(conv_w)
    onw = out_norm_w[None, :]
    row = lambda w: pl.BlockSpec((bt, w), lambda b, t: (b * nt + t, 0))
    full = lambda a: pl.BlockSpec(a.shape, lambda b, t: (0, 0))
    consts = (cw, lane_pad(a_log), lane_pad(dt_bias), sub_pad(a_log), sub_pad(dt_bias), onw)
    return pl.pallas_call(
        functools.partial(_gdn_kernel, bt=bt),
        grid=(batch, nt),
        in_specs=[row(A_QKV), row(SMALL_W), pl.BlockSpec((1, 8, bt), lambda b, t: (b, 0, t)), row(A_WIDTH)]
        + [full(a) for a in consts],
        out_specs=row(A_WIDTH),
        out_shape=jax.ShapeDtypeStruct((batch * seq, A_WIDTH), F32),
        scratch_shapes=[pltpu.VMEM((A_HEADS, A_DK, A_DV), F32), pltpu.VMEM((8, A_QKV), F32)],
        compiler_params=pltpu.CompilerParams(dimension_semantics=("arbitrary", "arbitrary"),
                                             vmem_limit_bytes=VMEM_LIMIT),
        name="gdn",
    )(qkv, small, small_t, a_z, *consts)


def _dsa_kernel(q_ref, kx_ref, vxt_ref, iq_ref, ik_ref, smt_ref, bz_ref, ltri_ref, o_ref,
                sc_ref, hb_ref, mx_ref, acc_ref, s_ref, *, qb, ks, n_sel):
    rep = B_HEADS // B_KV_HEADS
    i = pl.program_id(1)
    ns = (i * qb) // ks + 1
    kidx = lax.broadcasted_iota(I32, (ks, qb), 0)
    qpos = i * qb + lax.broadcasted_iota(I32, (ks, qb), 1)
    kidx_t = lax.broadcasted_iota(I32, (LANES, qb), 0)
    qpos_t = i * qb + lax.broadcasted_iota(I32, (LANES, qb), 1)

    iq = iq_ref[...]
    iqs = jnp.concatenate([iq[:, h * IDX_DIM:(h + 1) * IDX_DIM] for h in range(IDX_HEADS)], axis=0)
    iwt = smt_ref[0][SM_IW:SM_IW + IDX_HEADS, :] * (IDX_HEADS ** -0.5 * IDX_DIM ** -0.5)

    def score_body(j, carry):
        for u in range(ks // LANES):
            start = pl.multiple_of(j * ks + u * LANES, LANES)
            lg = _dot_nt(ik_ref[pl.ds(start, LANES), :], iqs)
            sc = jnp.zeros((LANES, qb), F32)
            for h in range(IDX_HEADS):
                sc = sc + iwt[h:h + 1, :] * jnp.maximum(lg[:, h * qb:(h + 1) * qb], 0.0)
            rows = slice(u * LANES, (u + 1) * LANES)
            sc = jnp.where(start + kidx_t <= qpos_t, sc, -jnp.inf)
            sc_ref[j, rows, :] = sc
            hb_ref[j, rows, :] = sc.astype(BF16)
        return carry

    lax.fori_loop(0, ns, score_body, 0)

    def count(src_ref, chunk, hit_of, dtype):
        def body(j, acc):
            for u in range(ks // chunk):
                acc = acc + hit_of(src_ref[j, u * chunk:(u + 1) * chunk, :])
            return acc

        acc = lax.fori_loop(0, ns, body, jnp.zeros((chunk, qb), dtype))
        return jnp.sum(acc.astype(F32), axis=0, keepdims=True)

    def key_value(key):
        val = pltpu.bitcast(key ^ ((key >> 31) & 0x7FFFFFFF), F32)
        return jnp.where(key < KEY_MIN_FINITE, -jnp.inf, val)

    def count_ge(key):
        cand = jnp.broadcast_to(key_value(key), (8, qb))
        return count(sc_ref, 8, lambda sc: jnp.where(sc >= cand, 1.0, 0.0), F32)

    one_b = jnp.ones((16, qb), BF16)
    zero_b = jnp.zeros((16, qb), BF16)

    def coarse_value(key):
        bits = (key ^ ((key >> 31) & 0x7FFFFFFF)) & (-(1 << 16))
        return jnp.where(key < KEY_MIN_FINITE, -jnp.inf, pltpu.bitcast(bits, F32))

    def count_coarse(hit_of):
        return count(hb_ref, 16, lambda hb: jnp.where(hit_of(hb), one_b, zero_b), BF16)

    def coarse_body(bi, tau):
        key = tau ^ lax.shift_left(jnp.int32(1), 31 - bi)
        cand = jnp.broadcast_to(coarse_value(key), (16, qb)).astype(BF16)
        return jnp.where(count_coarse(lambda hb: hb >= cand) >= n_sel, key, tau)

    searching = (i + 1) * qb > n_sel
    coarse = lax.fori_loop(0, jnp.where(searching, 16, 0), coarse_body, jnp.full((1, qb), INT_MIN, I32))
    step = 1 << 16
    centre = jnp.where(coarse < 0, coarse | (step - 1), coarse)
    lo0 = jnp.maximum(centre, INT_MIN + step) - (step >> 1)
    hi0 = centre + step
    at_zero = ((count_coarse(lambda hb: hb > zero_b) < n_sel) & (count_coarse(lambda hb: hb >= zero_b) >= n_sel))
    lo0 = jnp.where(at_zero, 0, lo0)
    hi0 = jnp.where(at_zero, 1, hi0)
    max_steps = jnp.where(searching, 20, 0)

    def fine_cond(state):
        it, _, _, _, open_rows = state
        return jnp.logical_and(it < max_steps, open_rows > 0.0)

    def fine_body(state):
        it, lo, hi, c_lo, _ = state
        settled = (c_lo == n_sel) | (hi - lo <= 1)
        open_rows = jnp.max(jnp.where(settled, 0.0, 1.0))
        mid = lo + ((hi - lo) >> 1)
        cnt = count_ge(mid)
        ok = cnt >= n_sel
        return (it + 1, jnp.where(ok, mid, lo), jnp.where(ok, hi, mid), jnp.where(ok, cnt, c_lo), open_rows)

    _, tau, _, _, _ = lax.while_loop(
        fine_cond, fine_body, (jnp.int32(0), lo0, hi0, jnp.full((1, qb), -1.0, F32), jnp.float32(1.0)))
    tau_val = key_value(tau)
    tau8 = jnp.broadcast_to(tau_val, (8, qb))
    need = n_sel - count(sc_ref, 8, lambda sc: jnp.where(sc > tau8, 1.0, 0.0), F32)

    q = q_ref[...]
    qs = [jnp.concatenate([q[:, (g * rep + r) * LANES:(g * rep + r + 1) * LANES] for r in range(rep)], axis=0)
          for g in range(B_KV_HEADS)]
    mx_ref[...] = jnp.full(mx_ref.shape, NEG_BIG, F32)
    acc_ref[...] = jnp.zeros(acc_ref.shape, F32)
    ltri = ltri_ref[...]
    taub = jnp.broadcast_to(tau_val, (ks, qb))

    def att_body(j, eq_seen):
        start = pl.multiple_of(j * ks, ks)
        sct = sc_ref[j]
        eqf = jnp.where(sct == taub, 1.0, 0.0)
        pref = _dot(ltri, eqf.astype(BF16)) + eq_seen
        take = jnp.where(sct > taub, 1.0, jnp.where(pref <= need, eqf, 0.0))
        bias = jnp.where((take > 0.0) & (j * ks + kidx <= qpos), 0.0, 2.0 * NEG_BIG)
        alphas = []
        for g in range(B_KV_HEADS):
            kx = kx_ref[g, pl.ds(start, ks), :]
            for r in range(rep):
                cols = slice(r * qb, (r + 1) * qb)
                m_old = mx_ref[g, :, cols]
                m_new = m_old
                for kc in range(ks // LANES):
                    rows = slice(kc * LANES, (kc + 1) * LANES)
                    s = _dot_nt(kx[rows], qs[g][cols]) + bias[rows]
                    s_ref[g, rows, cols] = s
                    m_new = jnp.maximum(m_new, jnp.max(s, axis=0, keepdims=True))
                mx_ref[g, :, cols] = m_new
                alphas.append(jnp.exp2(m_old - m_new)[0:1, :])
        half = ks // 2
        for g in range(B_KV_HEADS):
            vxt = vxt_ref[g, :, pl.ds(start, ks)]
            for r in range(rep):
                cols = slice(r * qb, (r + 1) * qb)
                m_new = mx_ref[g, 0:1, cols]
                acc = alphas[g * rep + r] * acc_ref[g, :, cols]
                for kk in range(2):
                    rows = slice(kk * half, (kk + 1) * half)
                    p = jnp.exp2(s_ref[g, rows, cols] - m_new).astype(BF16)
                    acc = acc + _dot(vxt[:, rows], p)
                acc_ref[g, :, cols] = acc
        return eq_seen + jnp.sum(eqf, axis=0, keepdims=True)

    lax.fori_loop(0, ns, att_body, jnp.zeros((1, qb), F32))

    lane = lax.broadcasted_iota(I32, (qb, LANES), 1)
    outs = []
    for g in range(B_KV_HEADS):
        acc = acc_ref[g]
        acc = jnp.concatenate([acc, jnp.zeros((LANES - V_ROWS, rep * qb), F32)], axis=0)
        a = jnp.concatenate([jnp.transpose(acc[:, r * qb:(r + 1) * qb]) for r in range(rep)], axis=0)
        outs.append(a / a[:, B_HD:B_HD + 1])
    for cblk in range(B_WIDTH // LANES):
        g = (2 * cblk) // rep
        r0 = (2 * cblk) % rep
        a = outs[g][r0 * qb:(r0 + 1) * qb]
        b = pltpu.roll(outs[g][(r0 + 1) * qb:(r0 + 2) * qb], B_HD, axis=1)
        z = bz_ref[:, cblk * LANES:(cblk + 1) * LANES]
        o_ref[:, cblk * LANES:(cblk + 1) * LANES] = jnp.where(lane < B_HD, a, b) * _silu(z)


def _dsa(bqx, kx, vxt, iq, ik, small_t, bz, batch, seq, qb, ks):
    nq = seq // qb
    n_sel = min(TOPK_MAX, seq // 4)
    rep = B_HEADS // B_KV_HEADS
    ltri = (jnp.arange(ks)[None, :] <= jnp.arange(ks)[:, None]).astype(BF16)
    row = lambda w: pl.BlockSpec((qb, w), lambda b, i: (b * nq + i, 0))
    per_batch = lambda w: pl.BlockSpec((seq, w), lambda b, i: (b, 0))
    return pl.pallas_call(
        functools.partial(_dsa_kernel, qb=qb, ks=ks, n_sel=n_sel),
        grid=(batch, nq),
        in_specs=[row(B_HEADS * LANES),
                  pl.BlockSpec((B_KV_HEADS, seq, LANES), lambda b, i: (0, b, 0)),
                  pl.BlockSpec((B_KV_HEADS, V_ROWS, seq), lambda b, i: (0, 0, b)),
                  row(IDX_WIDTH), per_batch(IDX_DIM),
                  pl.BlockSpec((1, 16, qb), lambda b, i: (b, 0, i)),
                  row(B_WIDTH), pl.BlockSpec((ks, ks), lambda b, i: (0, 0))],
        out_specs=row(B_WIDTH),
        out_shape=jax.ShapeDtypeStruct((batch * seq, B_WIDTH), F32),
        scratch_shapes=[pltpu.VMEM((seq // ks, ks, qb), F32),
                        pltpu.VMEM((seq // ks, ks, qb), BF16),
                        pltpu.VMEM((B_KV_HEADS, 8, rep * qb), F32),
                        pltpu.VMEM((B_KV_HEADS, V_ROWS, rep * qb), F32),
                        pltpu.VMEM((B_KV_HEADS, ks, rep * qb), F32)],
        compiler_params=pltpu.CompilerParams(dimension_semantics=("arbitrary", "arbitrary"),
                                             vmem_limit_bytes=VMEM_LIMIT),
        name="dsa",
    )(bqx, kx, vxt, iq, ik, small_t, bz, ltri)


def _out_kernel(x_ref, oa_ref, ob_ref, p_ref, wo_ref, wp_ref, wg_ref, gn_ref, bg_ref, o_ref):
    x1 = (x_ref[...] + _dot(oa_ref[...].astype(BF16), wo_ref[0:A_WIDTH, :])
          + _dot(ob_ref[...].astype(BF16), wo_ref[A_WIDTH:A_WIDTH + B_WIDTH, :]))
    hn = x1 * lax.rsqrt(jnp.mean(x1 * x1, axis=-1, keepdims=True) + EPS) * gn_ref[...]
    gate = jax.nn.sigmoid(_dot(hn.astype(BF16), wg_ref[...]) + bg_ref[...])
    o_ref[...] = x1 + _dot(p_ref[...].astype(BF16), wp_ref[...]) * gate


def _output(x2, oa, ob, p2, w_out, w_ple, gate_norm_w, w_gate, b_gate, tm):
    t, d = x2.shape
    wo = w_out.astype(BF16)
    wp = w_ple.astype(BF16)
    wg = w_gate.astype(BF16)
    gn = gate_norm_w[None, :]
    bg = b_gate[None, :]
    row = lambda w: pl.BlockSpec((tm, w), lambda i: (i, 0))
    full = lambda a: pl.BlockSpec(a.shape, lambda i: (0, 0))
    return pl.pallas_call(
        _out_kernel,
        grid=(t // tm,),
        in_specs=[row(d), row(A_WIDTH), row(B_WIDTH), row(PLE_DIM)] + [full(a) for a in (wo, wp, wg, gn, bg)],
        out_specs=row(d),
        out_shape=jax.ShapeDtypeStruct((t, d), F32),
        compiler_params=pltpu.CompilerParams(dimension_semantics=("arbitrary",), vmem_limit_bytes=VMEM_LIMIT),
        name="out",
    )(x2, oa, ob, p2, wo, wp, wg, gn, bg)


def kernel(x, p, attn_norm_w, w_in, conv_w, a_log, dt_bias, a_out_norm_w, b_q_norm_w, b_k_norm_w, w_out,
           w_ple, ple_gate_norm_w, w_ple_gate, b_ple_gate):
    batch, seq, d = x.shape
    t = batch * seq
    tm = min(512, t)
    bt = min(256, seq)
    qb = min(256, seq)
    ks = min(512, seq)
    x2 = x.reshape(t, d)
    for i in range(w_in.shape[0]):
        qkv, az, bqx, kx, vxt, bz, iq, ik, small = _project(x2, attn_norm_w[i], w_in[i], b_q_norm_w[i],
                                                           b_k_norm_w[i], tm)
        small_t = jnp.swapaxes(small.reshape(batch, seq, SMALL_W)[:, :, :16], 1, 2)
        oa = _gdn(qkv, small, small_t, az, conv_w[i], a_log[i], dt_bias[i], a_out_norm_w[i], batch, seq, bt)
        ob = _dsa(bqx, kx, vxt, iq, ik, small_t, bz, batch, seq, qb, ks)
        x2 = _output(x2, oa, ob, p[i].reshape(t, PLE_DIM), w_out[i], w_ple[i], ple_gate_norm_w[i],
                     w_ple_gate[i], b_ple_gate[i], tm)
    return x2.reshape(batch, seq, d)
```

```python
import functools

import jax
import jax.numpy as jnp
from jax import lax
from jax.experimental import pallas as pl
from jax.experimental.pallas import tpu as pltpu

F32 = jnp.float32
BF16 = jnp.bfloat16
I32 = jnp.int32
EPS = 1e-6
HI = lax.Precision.HIGHEST

PLE_DIM = 256
A_HEADS = 4
A_DK = 128
A_DV = 128
A_CONV = 4
A_CHUNK = 64
A_WIDTH = A_HEADS * A_DV
A_QKV = 2 * A_HEADS * A_DK + A_WIDTH
B_HEADS = 8
B_KV_HEADS = 2
B_HD = 64
B_WIDTH = B_HEADS * B_HD
B_KV_WIDTH = B_KV_HEADS * B_HD
IDX_HEADS = 8
IDX_DIM = 128
IDX_WIDTH = IDX_HEADS * IDX_DIM
TOPK_MAX = 256
LANES = 128
SUBLANES = 8
SUBLANES_BF16 = 16
SMALL_W = LANES
SM_BETA = 0
SM_DECAY = A_HEADS
SM_IW = 2 * A_HEADS
V_ROWS = B_HD + 16

VMEM_LIMIT = 56 * 1024 * 1024
NEG_BIG = -1e30
LOG2_E = 1.4426950408889634
INT_MIN = -(2 ** 31)
KEY_MIN_FINITE = INT_MIN + 0x00800000
COARSE_BITS = 16
FINE_MAX_STEPS = 20


def _dot(a, b, prec=None):
    return jnp.dot(a, b, preferred_element_type=F32, precision=prec)


def _dot_nt(a, b, prec=None):
    return lax.dot_general(a, b, (((1,), (1,)), ((), ())), preferred_element_type=F32, precision=prec)


def _dot_tn(a, b, prec=None):
    return lax.dot_general(a, b, (((0,), (0,)), ((), ())), preferred_element_type=F32, precision=prec)


def _silu(x):
    return x * jax.nn.sigmoid(x)


def _softplus(x):
    return jnp.maximum(x, 0.0) + jnp.log1p(jnp.exp(-jnp.abs(x)))


def _seg_norm64(xb, gain_row):
    lane = lax.broadcasted_iota(I32, xb.shape, 1)
    lo = lane < B_HD
    sq = xb * xb
    s_lo = jnp.sum(jnp.where(lo, sq, 0.0), axis=-1, keepdims=True)
    s_hi = jnp.sum(jnp.where(lo, 0.0, sq), axis=-1, keepdims=True)
    ms = jnp.where(lo, s_lo, s_hi) * (1.0 / B_HD)
    return xb * lax.rsqrt(ms + EPS) * gain_row


_WB_QKV = 0
_WB_AZ = _WB_QKV + A_QKV
_WB_BQ = _WB_AZ + A_WIDTH
_WB_BK = _WB_BQ + B_WIDTH
_WB_BV = _WB_BK + B_KV_WIDTH
_WB_BZ = _WB_BV + B_KV_WIDTH
_WB_IQ = _WB_BZ + B_WIDTH
_WB_IK = _WB_IQ + IDX_WIDTH
_WB_SM = _WB_IK + IDX_DIM
_WB_END = _WB_SM + SMALL_W


def _proj_kernel(x_ref, nw_ref, wb_ref, qg_ref, kg_ref,
                 qkv_ref, az_ref, bqx_ref, kx_ref, vxt_ref, bz_ref, iq_ref, ik_ref, sm_ref):
    x = x_ref[...]
    h = x * lax.rsqrt(jnp.mean(x * x, axis=-1, keepdims=True) + EPS) * nw_ref[...]
    hb = h.astype(BF16)
    step = 512
    for c0 in range(0, A_QKV, step):
        qkv_ref[:, c0:c0 + step] = _dot(hb, wb_ref[:, _WB_QKV + c0:_WB_QKV + c0 + step])
    az_ref[...] = _dot(hb, wb_ref[:, _WB_AZ:_WB_BQ])
    bz_ref[...] = _dot(hb, wb_ref[:, _WB_BZ:_WB_IQ])
    lane = lax.broadcasted_iota(I32, (x.shape[0], LANES), 1)
    lo = lane < B_HD
    bk = _seg_norm64(_dot(hb, wb_ref[:, _WB_BK:_WB_BV]), kg_ref[...])
    bv = _dot(hb, wb_ref[:, _WB_BV:_WB_BZ])
    k_tail = jnp.where(lane == B_HD, 1.0, 0.0)
    for g in range(B_KV_HEADS):
        kg = bk if g == 0 else pltpu.roll(bk, B_HD, axis=1)
        vg = bv if g == 0 else pltpu.roll(bv, B_HD, axis=1)
        kx_ref[g] = jnp.where(lo, kg, k_tail).astype(BF16)
        vxt_ref[g] = jnp.transpose(jnp.where(lo, vg, 1.0))[0:V_ROWS].astype(BF16)
    bq = _dot(hb, wb_ref[:, _WB_BQ:_WB_BK])
    scale = B_HD ** -0.5 * LOG2_E
    for c0 in range(0, B_WIDTH, LANES):
        nb = _seg_norm64(bq[:, c0:c0 + LANES], qg_ref[...]) * scale
        bqx_ref[:, 2 * c0:2 * c0 + LANES] = jnp.where(lo, nb, 0.0).astype(BF16)
        bqx_ref[:, 2 * c0 + LANES:2 * c0 + 2 * LANES] = jnp.where(
            lo, pltpu.roll(nb, B_HD, axis=1), 0.0).astype(BF16)
    for c0 in range(0, IDX_WIDTH, step):
        iq_ref[:, c0:c0 + step] = _dot(hb, wb_ref[:, _WB_IQ + c0:_WB_IQ + c0 + step]).astype(BF16)
    ik_ref[...] = _dot(hb, wb_ref[:, _WB_IK:_WB_SM]).astype(BF16)
    sm_ref[...] = _dot(hb, wb_ref[:, _WB_SM:_WB_END])


def _project(x2, norm_w, w_in, q_gain, k_gain, tm):
    t, d = x2.shape
    sizes = (A_HEADS * A_DK, A_HEADS * A_DK, A_WIDTH, A_WIDTH, A_HEADS, A_HEADS, B_WIDTH, B_KV_WIDTH,
             B_KV_WIDTH, B_WIDTH, IDX_WIDTH, IDX_DIM, IDX_HEADS)
    offs = [0]
    for s in sizes:
        offs.append(offs[-1] + s)
    (a_q, a_k, a_v, a_z, a_b, a_a, b_q, b_k, b_v, b_z, i_q, i_k, i_w) = [
        w_in[:, offs[n]:offs[n + 1]] for n in range(len(sizes))]
    pad = jnp.zeros((d, SMALL_W - 2 * A_HEADS - IDX_HEADS), F32)
    wb = jnp.concatenate([a_q, a_k, a_v, a_z, b_q, b_k, b_v, b_z, i_q, i_k, a_b, a_a, i_w, pad],
                         axis=1).astype(BF16)
    qg = jnp.tile(q_gain, LANES // B_HD)[None, :]
    kg = jnp.tile(k_gain, LANES // B_HD)[None, :]
    row = lambda w: pl.BlockSpec((tm, w), lambda i: (i, 0))
    grp = pl.BlockSpec((B_KV_HEADS, tm, LANES), lambda i: (0, i, 0))
    full = lambda a: pl.BlockSpec(a.shape, lambda i: (0, 0))
    nw = norm_w[None, :]
    sds = jax.ShapeDtypeStruct
    return pl.pallas_call(
        _proj_kernel,
        grid=(t // tm,),
        in_specs=[row(d), full(nw), full(wb), full(qg), full(kg)],
        out_specs=[row(A_QKV), row(A_WIDTH), row(B_HEADS * LANES), grp,
                   pl.BlockSpec((B_KV_HEADS, V_ROWS, tm), lambda i: (0, 0, i)), row(B_WIDTH), row(IDX_WIDTH),
                   row(IDX_DIM), row(SMALL_W)],
        out_shape=[sds((t, A_QKV), F32), sds((t, A_WIDTH), F32), sds((t, B_HEADS * LANES), BF16),
                   sds((B_KV_HEADS, t, LANES), BF16), sds((B_KV_HEADS, V_ROWS, t), BF16), sds((t, B_WIDTH), F32),
                   sds((t, IDX_WIDTH), BF16), sds((t, IDX_DIM), BF16), sds((t, SMALL_W), F32)],
        compiler_params=pltpu.CompilerParams(dimension_semantics=("arbitrary",), vmem_limit_bytes=VMEM_LIMIT),
        name="proj",
    )(x2, nw, wb, qg, kg)


def _gdn_kernel(qkv_ref, sm_ref, smt_ref, az_ref, cw_ref, alog_r_ref, dtb_r_ref, alog_c_ref, dtb_c_ref,
                onw_ref, o_ref, s_ref, tail_ref, *, bt):
    c = A_CHUNK

    @pl.when(pl.program_id(1) == 0)
    def _():
        s_ref[...] = jnp.zeros(s_ref.shape, F32)
        tail_ref[...] = jnp.zeros(tail_ref.shape, F32)

    xin = qkv_ref[...]
    tail = tail_ref[...]
    cw = cw_ref[...]
    row8 = lax.broadcasted_iota(I32, tail.shape, 0)
    acc = xin * cw[A_CONV - 1:A_CONV, :]
    for d in range(1, A_CONV):
        xr = pltpu.roll(xin, d, axis=0)
        pr = pltpu.roll(tail, d, axis=0)
        head = jnp.where(row8 < d, pr, xr[0:8])
        xs = jnp.concatenate([head, xr[8:]], axis=0)
        acc = acc + xs * cw[A_CONV - 1 - d:A_CONV - d, :]
    tail_ref[...] = xin[bt - 8:bt]
    qkv = _silu(acc)

    sm = sm_ref[...]
    smt = smt_ref[0]
    g_col = -jnp.exp(alog_r_ref[...]) * _softplus(sm + dtb_r_ref[...])
    g_row = -jnp.exp(alog_c_ref[...]) * _softplus(smt + dtb_c_ref[...])
    beta_all = jax.nn.sigmoid(sm)

    r = lax.broadcasted_iota(I32, (bt, bt), 0)
    s = lax.broadcasted_iota(I32, (bt, bt), 1)
    same = (r // c) == (s // c)
    tril = same & (s <= r)
    strict = same & (s < r)
    eye = jnp.where(s == r, 1.0, 0.0).astype(F32)
    gc_col = _dot(jnp.where(tril, 1.0, 0.0).astype(F32), g_col, HI)
    gc_row = _dot(g_row, jnp.where(same & (r <= s), 1.0, 0.0).astype(F32), HI)
    gtot = _dot(jnp.where(same, 1.0, 0.0).astype(F32), g_col, HI)
    egc_all = jnp.exp(gc_col)
    ekd_all = jnp.exp(gtot - gc_col)
    egl_all = jnp.exp(gtot)
    onw = onw_ref[...]

    heads = range(A_HEADS)
    qs, ks_, kbs, decays, amats = [], [], [], [], []
    for h in heads:
        q = qkv[:, h * A_DK:(h + 1) * A_DK]
        k = qkv[:, A_HEADS * A_DK + h * A_DK:A_HEADS * A_DK + (h + 1) * A_DK]
        q = q * lax.rsqrt(jnp.sum(q * q, axis=-1, keepdims=True) + EPS) * (A_DK ** -0.5)
        k = k * lax.rsqrt(jnp.sum(k * k, axis=-1, keepdims=True) + EPS)
        lane = SM_DECAY + h
        gcc = gc_col[:, lane:lane + 1]
        gcr = gc_row[lane:lane + 1, :]
        decay = jnp.where(tril, jnp.exp(jnp.where(tril, gcc - gcr, 0.0)), 0.0)
        kb = k * beta_all[:, SM_BETA + h:SM_BETA + h + 1]
        qs.append(q)
        ks_.append(k)
        kbs.append(kb)
        decays.append(decay)
        amats.append(jnp.where(strict, _dot_nt(kb.astype(BF16), k.astype(BF16)) * decay, 0.0))
    tmats = [eye - jnp.where((r // 2) == (s // 2), a, 0.0) for a in amats]
    size = 2
    while size < c:
        off = ((r // (2 * size)) == (s // (2 * size))) & ((r // size) != (s // size))
        t_bfs = [tmats[h].astype(BF16) for h in heads]
        tcs = [_dot(t_bfs[h], jnp.where(off, amats[h], 0.0).astype(BF16)).astype(BF16) for h in heads]
        for h in heads:
            tmats[h] = tmats[h] - _dot(tcs[h], t_bfs[h])
        size *= 2
    values, kcds, atts, qds, kds, egls = [], [], [], [], [], []
    for h in heads:
        lane = SM_DECAY + h
        v = qkv[:, 2 * A_HEADS * A_DK + h * A_DV:2 * A_HEADS * A_DK + (h + 1) * A_DV]
        t_bf = tmats[h].astype(BF16)
        egc = egc_all[:, lane:lane + 1]
        values.append(_dot(t_bf, (v * beta_all[:, SM_BETA + h:SM_BETA + h + 1]).astype(BF16)))
        kcds.append(_dot(t_bf, (kbs[h] * egc).astype(BF16)).astype(BF16))
        atts.append((_dot_nt(qs[h].astype(BF16), ks_[h].astype(BF16)) * decays[h]).astype(BF16))
        qds.append((qs[h] * egc).astype(BF16))
        kds.append((ks_[h] * ekd_all[:, lane:lane + 1]).astype(BF16))
        egls.append(egl_all[:, lane:lane + 1])
    states = [s_ref[h] for h in heads]
    o_inter = [[] for _ in heads]
    v_new = [[] for _ in heads]
    for ci in range(bt // c):
        rows = slice(ci * c, (ci + 1) * c)
        for h in heads:
            s_bf = states[h].astype(BF16)
            vn = (values[h][rows] - _dot(kcds[h][rows], s_bf)).astype(BF16)
            o_inter[h].append(_dot(qds[h][rows], s_bf))
            v_new[h].append(vn)
            gl = egls[h][ci * c:ci * c + 8]
            states[h] = states[h] * jnp.concatenate([gl] * (A_DK // 8), axis=0) + _dot_tn(kds[h][rows], vn)
    for h in heads:
        s_ref[h] = states[h]
        o = jnp.concatenate(o_inter[h], axis=0) + _dot(atts[h], jnp.concatenate(v_new[h], axis=0))
        on = o * lax.rsqrt(jnp.mean(o * o, axis=-1, keepdims=True) + EPS) * onw
        z = az_ref[:, h * A_DV:(h + 1) * A_DV]
        o_ref[:, h * A_DV:(h + 1) * A_DV] = on * _silu(z)


def _gdn(qkv, small, small_t, a_z, conv_w, a_log, dt_bias, out_norm_w, batch, seq, bt):
    nt = seq // bt
    lane_pad = lambda v: jnp.zeros((1, LANES), F32).at[0, SM_DECAY:SM_DECAY + A_HEADS].set(v)
    sub_pad = lambda v: jnp.zeros((8, 1), F32).at[SM_DECAY:SM_DECAY + A_HEADS, 0].set(v)
    cw = jnp.zeros((8, A_QKV), F32).at[:A_CONV].set(conv_w)
    onw = out_norm_w[None, :]
    row = lambda w: pl.BlockSpec((bt, w), lambda b, t: (b * nt + t, 0))
    full = lambda a: pl.BlockSpec(a.shape, lambda b, t: (0, 0))
    consts = (cw, lane_pad(a_log), lane_pad(dt_bias), sub_pad(a_log), sub_pad(dt_bias), onw)
    return pl.pallas_call(
        functools.partial(_gdn_kernel, bt=bt),
        grid=(batch, nt),
        in_specs=[row(A_QKV), row(SMALL_W), pl.BlockSpec((1, 8, bt), lambda b, t: (b, 0, t)), row(A_WIDTH)]
        + [full(a) for a in consts],
        out_specs=row(A_WIDTH),
        out_shape=jax.ShapeDtypeStruct((batch * seq, A_WIDTH), F32),
        scratch_shapes=[pltpu.VMEM((A_HEADS, A_DK, A_DV), F32), pltpu.VMEM((8, A_QKV), F32)],
        compiler_params=pltpu.CompilerParams(dimension_semantics=("arbitrary", "arbitrary"),
                                             vmem_limit_bytes=VMEM_LIMIT),
        name="gdn",
    )(qkv, small, small_t, a_z, *consts)


def _dsa_kernel(q_ref, kx_ref, vxt_ref, iq_ref, ik_ref, smt_ref, bz_ref, ltri_ref, o_ref,
                sc_ref, hb_ref, mx_ref, acc_ref, s_ref, *, qb, ks, n_sel):
    rep = B_HEADS // B_KV_HEADS
    i = pl.program_id(1)
    ns = (i * qb) // ks + 1
    kidx = lax.broadcasted_iota(I32, (ks, qb), 0)
    qpos = i * qb + lax.broadcasted_iota(I32, (ks, qb), 1)
    kidx_t = lax.broadcasted_iota(I32, (LANES, qb), 0)
    qpos_t = i * qb + lax.broadcasted_iota(I32, (LANES, qb), 1)

    iq = iq_ref[...]
    iqs = jnp.concatenate([iq[:, h * IDX_DIM:(h + 1) * IDX_DIM] for h in range(IDX_HEADS)], axis=0)
    iwt = smt_ref[0][SM_IW:SM_IW + IDX_HEADS, :] * (IDX_HEADS ** -0.5 * IDX_DIM ** -0.5)

    def score_body(j, carry):
        for u in range(ks // LANES):
            start = pl.multiple_of(j * ks + u * LANES, LANES)
            lg = _dot_nt(ik_ref[pl.ds(start, LANES), :], iqs)
            sc = jnp.zeros((LANES, qb), F32)
            for h in range(IDX_HEADS):
                sc = sc + iwt[h:h + 1, :] * jnp.maximum(lg[:, h * qb:(h + 1) * qb], 0.0)
            rows = slice(u * LANES, (u + 1) * LANES)
            sc = jnp.where(start + kidx_t <= qpos_t, sc, -jnp.inf)
            sc_ref[j, rows, :] = sc
            hb_ref[j, rows, :] = sc.astype(BF16)
        return carry

    lax.fori_loop(0, ns, score_body, 0)

    def count(src_ref, chunk, hit_of, dtype):
        def body(j, acc):
            for u in range(ks // chunk):
                acc = acc + hit_of(src_ref[j, u * chunk:(u + 1) * chunk, :])
            return acc

        acc = lax.fori_loop(0, ns, body, jnp.zeros((chunk, qb), dtype))
        return jnp.sum(acc.astype(F32), axis=0, keepdims=True)

    def key_value(key):
        val = pltpu.bitcast(key ^ ((key >> 31) & 0x7FFFFFFF), F32)
        return jnp.where(key < KEY_MIN_FINITE, -jnp.inf, val)

    def count_ge(key):
        cand = jnp.broadcast_to(key_value(key), (SUBLANES, qb))
        return count(sc_ref, SUBLANES, lambda sc: jnp.where(sc >= cand, 1.0, 0.0), F32)

    one_b = jnp.ones((SUBLANES_BF16, qb), BF16)
    zero_b = jnp.zeros((SUBLANES_BF16, qb), BF16)

    def coarse_value(key):
        bits = (key ^ ((key >> 31) & 0x7FFFFFFF)) & (-(1 << (32 - COARSE_BITS)))
        return jnp.where(key < KEY_MIN_FINITE, -jnp.inf, pltpu.bitcast(bits, F32))

    def count_coarse(hit_of):
        return count(hb_ref, SUBLANES_BF16, lambda hb: jnp.where(hit_of(hb), one_b, zero_b), BF16)

    def coarse_body(bi, tau):
        key = tau ^ lax.shift_left(jnp.int32(1), 31 - bi)
        cand = jnp.broadcast_to(coarse_value(key), (SUBLANES_BF16, qb)).astype(BF16)
        return jnp.where(count_coarse(lambda hb: hb >= cand) >= n_sel, key, tau)

    searching = (i + 1) * qb > n_sel
    coarse = lax.fori_loop(0, jnp.where(searching, COARSE_BITS, 0), coarse_body, jnp.full((1, qb), INT_MIN, I32))
    step = 1 << (32 - COARSE_BITS)
    centre = jnp.where(coarse < 0, coarse | (step - 1), coarse)
    lo0 = jnp.maximum(centre, INT_MIN + step) - (step >> 1)
    hi0 = centre + step
    at_zero = ((count_coarse(lambda hb: hb > zero_b) < n_sel) & (count_coarse(lambda hb: hb >= zero_b) >= n_sel))
    lo0 = jnp.where(at_zero, 0, lo0)
    hi0 = jnp.where(at_zero, 1, hi0)
    max_steps = jnp.where(searching, FINE_MAX_STEPS, 0)

    def fine_cond(state):
        it, _, _, _, open_rows = state
        return jnp.logical_and(it < max_steps, open_rows > 0.0)

    def fine_body(state):
        it, lo, hi, c_lo, _ = state
        settled = (c_lo == n_sel) | (hi - lo <= 1)
        open_rows = jnp.max(jnp.where(settled, 0.0, 1.0))
        mid = lo + ((hi - lo) >> 1)
        cnt = count_ge(mid)
        ok = cnt >= n_sel
        return (it + 1, jnp.where(ok, mid, lo), jnp.where(ok, hi, mid), jnp.where(ok, cnt, c_lo), open_rows)

    _, tau, _, _, _ = lax.while_loop(
        fine_cond, fine_body, (jnp.int32(0), lo0, hi0, jnp.full((1, qb), -1.0, F32), jnp.float32(1.0)))
    tau_val = key_value(tau)
    tau8 = jnp.broadcast_to(tau_val, (SUBLANES, qb))
    need = n_sel - count(sc_ref, SUBLANES, lambda sc: jnp.where(sc > tau8, 1.0, 0.0), F32)

    q = q_ref[...]
    qs = [jnp.concatenate([q[:, (g * rep + r) * LANES:(g * rep + r + 1) * LANES] for r in range(rep)], axis=0)
          for g in range(B_KV_HEADS)]
    mx_ref[...] = jnp.full(mx_ref.shape, NEG_BIG, F32)
    acc_ref[...] = jnp.zeros(acc_ref.shape, F32)
    ltri = ltri_ref[...]
    taub = jnp.broadcast_to(tau_val, (ks, qb))

    def att_body(j, eq_seen):
        start = pl.multiple_of(j * ks, ks)
        sct = sc_ref[j]
        eqf = jnp.where(sct == taub, 1.0, 0.0)
        pref = _dot(ltri, eqf.astype(BF16)) + eq_seen
        take = jnp.where(sct > taub, 1.0, jnp.where(pref <= need, eqf, 0.0))
        bias = jnp.where((take > 0.0) & (j * ks + kidx <= qpos), 0.0, 2.0 * NEG_BIG)
        alphas = []
        for g in range(B_KV_HEADS):
            kx = kx_ref[g, pl.ds(start, ks), :]
            for r in range(rep):
                cols = slice(r * qb, (r + 1) * qb)
                m_old = mx_ref[g, :, cols]
                m_new = m_old
                for kc in range(ks // LANES):
                    rows = slice(kc * LANES, (kc + 1) * LANES)
                    s = _dot_nt(kx[rows], qs[g][cols]) + bias[rows]
                    s_ref[g, rows, cols] = s
                    m_new = jnp.maximum(m_new, jnp.max(s, axis=0, keepdims=True))
                mx_ref[g, :, cols] = m_new
                alphas.append(jnp.exp2(m_old - m_new)[0:1, :])
        half = ks // 2
        for g in range(B_KV_HEADS):
            vxt = vxt_ref[g, :, pl.ds(start, ks)]
            for r in range(rep):
                cols = slice(r * qb, (r + 1) * qb)
                m_new = mx_ref[g, 0:1, cols]
                acc = alphas[g * rep + r] * acc_ref[g, :, cols]
                for kk in range(2):
                    rows = slice(kk * half, (kk + 1) * half)
                    p = jnp.exp2(s_ref[g, rows, cols] - m_new).astype(BF16)
                    acc = acc + _dot(vxt[:, rows], p)
                acc_ref[g, :, cols] = acc
        return eq_seen + jnp.sum(eqf, axis=0, keepdims=True)

    lax.fori_loop(0, ns, att_body, jnp.zeros((1, qb), F32))

    lane = lax.broadcasted_iota(I32, (qb, LANES), 1)
    outs = []
    for g in range(B_KV_HEADS):
        acc = acc_ref[g]
        acc = jnp.concatenate([acc, jnp.zeros((LANES - V_ROWS, rep * qb), F32)], axis=0)
        a = jnp.concatenate([jnp.transpose(acc[:, r * qb:(r + 1) * qb]) for r in range(rep)], axis=0)
        outs.append(a / a[:, B_HD:B_HD + 1])
    for cblk in range(B_WIDTH // LANES):
        g = (2 * cblk) // rep
        r0 = (2 * cblk) % rep
        a = outs[g][r0 * qb:(r0 + 1) * qb]
        b = pltpu.roll(outs[g][(r0 + 1) * qb:(r0 + 2) * qb], B_HD, axis=1)
        z = bz_ref[:, cblk * LANES:(cblk + 1) * LANES]
        o_ref[:, cblk * LANES:(cblk + 1) * LANES] = jnp.where(lane < B_HD, a, b) * _silu(z)


def _dsa(bqx, kx, vxt, iq, ik, small_t, bz, batch, seq, qb, ks):
    nq = seq // qb
    n_sel = min(TOPK_MAX, seq // 4)
    rep = B_HEADS // B_KV_HEADS
    ltri = (jnp.arange(ks)[None, :] <= jnp.arange(ks)[:, None]).astype(BF16)
    row = lambda w: pl.BlockSpec((qb, w), lambda b, i: (b * nq + i, 0))
    per_batch = lambda w: pl.BlockSpec((seq, w), lambda b, i: (b, 0))
    return pl.pallas_call(
        functools.partial(_dsa_kernel, qb=qb, ks=ks, n_sel=n_sel),
        grid=(batch, nq),
        in_specs=[row(B_HEADS * LANES),
                  pl.BlockSpec((B_KV_HEADS, seq, LANES), lambda b, i: (0, b, 0)),
                  pl.BlockSpec((B_KV_HEADS, V_ROWS, seq), lambda b, i: (0, 0, b)),
                  row(IDX_WIDTH), per_batch(IDX_DIM),
                  pl.BlockSpec((1, 16, qb), lambda b, i: (b, 0, i)),
                  row(B_WIDTH), pl.BlockSpec((ks, ks), lambda b, i: (0, 0))],
        out_specs=row(B_WIDTH),
        out_shape=jax.ShapeDtypeStruct((batch * seq, B_WIDTH), F32),
        scratch_shapes=[pltpu.VMEM((seq // ks, ks, qb), F32),
                        pltpu.VMEM((seq // ks, ks, qb), BF16),
                        pltpu.VMEM((B_KV_HEADS, SUBLANES, rep * qb), F32),
                        pltpu.VMEM((B_KV_HEADS, V_ROWS, rep * qb), F32),
                        pltpu.VMEM((B_KV_HEADS, ks, rep * qb), F32)],
        compiler_params=pltpu.CompilerParams(dimension_semantics=("arbitrary", "arbitrary"),
                                             vmem_limit_bytes=VMEM_LIMIT),
        name="dsa",
    )(bqx, kx, vxt, iq, ik, small_t, bz, ltri)


def _out_kernel(x_ref, oa_ref, ob_ref, p_ref, wo_ref, wp_ref, wg_ref, gn_ref, bg_ref, o_ref):
    x1 = (x_ref[...] + _dot(oa_ref[...].astype(BF16), wo_ref[0:A_WIDTH, :])
          + _dot(ob_ref[...].astype(BF16), wo_ref[A_WIDTH:A_WIDTH + B_WIDTH, :]))
    hn = x1 * lax.rsqrt(jnp.mean(x1 * x1, axis=-1, keepdims=True) + EPS) * gn_ref[...]
    gate = jax.nn.sigmoid(_dot(hn.astype(BF16), wg_ref[...]) + bg_ref[...])
    o_ref[...] = x1 + _dot(p_ref[...].astype(BF16), wp_ref[...]) * gate


def _output(x2, oa, ob, p2, w_out, w_ple, gate_norm_w, w_gate, b_gate, tm):
    t, d = x2.shape
    wo = w_out.astype(BF16)
    wp = w_ple.astype(BF16)
    wg = w_gate.astype(BF16)
    gn = gate_norm_w[None, :]
    bg = b_gate[None, :]
    row = lambda w: pl.BlockSpec((tm, w), lambda i: (i, 0))
    full = lambda a: pl.BlockSpec(a.shape, lambda i: (0, 0))
    return pl.pallas_call(
        _out_kernel,
        grid=(t // tm,),
        in_specs=[row(d), row(A_WIDTH), row(B_WIDTH), row(PLE_DIM)] + [full(a) for a in (wo, wp, wg, gn, bg)],
        out_specs=row(d),
        out_shape=jax.ShapeDtypeStruct((t, d), F32),
        compiler_params=pltpu.CompilerParams(dimension_semantics=("arbitrary",), vmem_limit_bytes=VMEM_LIMIT),
        name="out",
    )(x2, oa, ob, p2, wo, wp, wg, gn, bg)


def kernel(x, p, attn_norm_w, w_in, conv_w, a_log, dt_bias, a_out_norm_w, b_q_norm_w, b_k_norm_w, w_out,
           w_ple, ple_gate_norm_w, w_ple_gate, b_ple_gate):
    batch, seq, d = x.shape
    t = batch * seq
    tm = min(512, t)
    bt = min(256, seq)
    qb = min(256, seq)
    ks = min(512, seq)
    x2 = x.reshape(t, d)
    for i in range(w_in.shape[0]):
        qkv, az, bqx, kx, vxt, bz, iq, ik, small = _project(x2, attn_norm_w[i], w_in[i], b_q_norm_w[i],
                                                           b_k_norm_w[i], tm)
        small_t = jnp.swapaxes(small.reshape(batch, seq, SMALL_W)[:, :, :16], 1, 2)
        oa = _gdn(qkv, small, small_t, az, conv_w[i], a_log[i], dt_bias[i], a_out_norm_w[i], batch, seq, bt)
        ob = _dsa(bqx, kx, vxt, iq, ik, small_t, bz, batch, seq, qb, ks)
        x2 = _output(x2, oa, ob, p[i].reshape(t, PLE_DIM), w_out[i], w_ple[i], ple_gate_norm_w[i],
                     w_ple_gate[i], b_ple_gate[i], tm)
    return x2.reshape(batch, seq, d)
```

```python
import functools

import jax
import jax.numpy as jnp
from jax import lax
from jax.experimental import pallas as pl
from jax.experimental.pallas import tpu as pltpu

F32 = jnp.float32
BF16 = jnp.bfloat16
I32 = jnp.int32
EPS = 1e-6
HI = lax.Precision.HIGHEST

PLE_DIM = 256
A_HEADS = 4
A_DK = 128
A_DV = 128
A_CONV = 4
A_CHUNK = 64
A_WIDTH = A_HEADS * A_DV
A_QKV = 2 * A_HEADS * A_DK + A_WIDTH
B_HEADS = 8
B_KV_HEADS = 2
B_HD = 64
B_WIDTH = B_HEADS * B_HD
B_KV_WIDTH = B_KV_HEADS * B_HD
IDX_HEADS = 8
IDX_DIM = 128
IDX_WIDTH = IDX_HEADS * IDX_DIM
TOPK_MAX = 256
LANES = 128
SUBLANES = 8
SUBLANES_BF16 = 16
SMALL_W = LANES
SM_BETA = 0
SM_DECAY = A_HEADS
SM_IW = 2 * A_HEADS
V_ROWS = B_HD + 16

VMEM_LIMIT = 56 * 1024 * 1024
NEG_BIG = -1e30
LOG2_E = 1.4426950408889634
INT_MIN = -(2 ** 31)
KEY_MIN_FINITE = INT_MIN + 0x00800000
COARSE_BITS = 16
FINE_MAX_STEPS = 20


def _dot(a, b, prec=None):
    return jnp.dot(a, b, preferred_element_type=F32, precision=prec)


def _dot_nt(a, b, prec=None):
    return lax.dot_general(a, b, (((1,), (1,)), ((), ())), preferred_element_type=F32, precision=prec)


def _dot_tn(a, b, prec=None):
    return lax.dot_general(a, b, (((0,), (0,)), ((), ())), preferred_element_type=F32, precision=prec)


def _silu(x):
    return x * jax.nn.sigmoid(x)


def _softplus(x):
    return jnp.maximum(x, 0.0) + jnp.log1p(jnp.exp(-jnp.abs(x)))


def _seg_norm64(xb, gain_row):
    lane = lax.broadcasted_iota(I32, xb.shape, 1)
    lo = lane < B_HD
    sq = xb * xb
    s_lo = jnp.sum(jnp.where(lo, sq, 0.0), axis=-1, keepdims=True)
    s_hi = jnp.sum(jnp.where(lo, 0.0, sq), axis=-1, keepdims=True)
    ms = jnp.where(lo, s_lo, s_hi) * (1.0 / B_HD)
    return xb * lax.rsqrt(ms + EPS) * gain_row


_WB_QKV = 0
_WB_AZ = _WB_QKV + A_QKV
_WB_BQ = _WB_AZ + A_WIDTH
_WB_BK = _WB_BQ + B_WIDTH
_WB_BV = _WB_BK + B_KV_WIDTH
_WB_BZ = _WB_BV + B_KV_WIDTH
_WB_IQ = _WB_BZ + B_WIDTH
_WB_IK = _WB_IQ + IDX_WIDTH
_WB_SM = _WB_IK + IDX_DIM
_WB_END = _WB_SM + SMALL_W


def _proj_kernel(x_ref, nw_ref, wb_ref, qg_ref, kg_ref,
                 qkv_ref, az_ref, bqx_ref, kx_ref, vxt_ref, bz_ref, iq_ref, ik_ref, sm_ref):
    x = x_ref[...]
    h = x * lax.rsqrt(jnp.mean(x * x, axis=-1, keepdims=True) + EPS) * nw_ref[...]
    hb = h.astype(BF16)
    step = 512
    for c0 in range(0, A_QKV, step):
        qkv_ref[:, c0:c0 + step] = _dot(hb, wb_ref[:, _WB_QKV + c0:_WB_QKV + c0 + step])
    az_ref[...] = _dot(hb, wb_ref[:, _WB_AZ:_WB_BQ])
    bz_ref[...] = _dot(hb, wb_ref[:, _WB_BZ:_WB_IQ])
    lane = lax.broadcasted_iota(I32, (x.shape[0], LANES), 1)
    lo = lane < B_HD
    bk = _seg_norm64(_dot(hb, wb_ref[:, _WB_BK:_WB_BV]), kg_ref[...])
    bv = _dot(hb, wb_ref[:, _WB_BV:_WB_BZ])
    k_tail = jnp.where(lane == B_HD, 1.0, 0.0)
    for g in range(B_KV_HEADS):
        kg = bk if g == 0 else pltpu.roll(bk, B_HD, axis=1)
        vg = bv if g == 0 else pltpu.roll(bv, B_HD, axis=1)
        kx_ref[g] = jnp.where(lo, kg, k_tail).astype(BF16)
        vxt_ref[g] = jnp.transpose(jnp.where(lo, vg, 1.0))[0:V_ROWS].astype(BF16)
    bq = _dot(hb, wb_ref[:, _WB_BQ:_WB_BK])
    scale = B_HD ** -0.5 * LOG2_E
    for c0 in range(0, B_WIDTH, LANES):
        nb = _seg_norm64(bq[:, c0:c0 + LANES], qg_ref[...]) * scale
        bqx_ref[:, 2 * c0:2 * c0 + LANES] = jnp.where(lo, nb, 0.0).astype(BF16)
        bqx_ref[:, 2 * c0 + LANES:2 * c0 + 2 * LANES] = jnp.where(
            lo, pltpu.roll(nb, B_HD, axis=1), 0.0).astype(BF16)
    for c0 in range(0, IDX_WIDTH, step):
        iq_ref[:, c0:c0 + step] = _dot(hb, wb_ref[:, _WB_IQ + c0:_WB_IQ + c0 + step]).astype(BF16)
    ik_ref[...] = _dot(hb, wb_ref[:, _WB_IK:_WB_SM]).astype(BF16)
    sm_ref[...] = _dot(hb, wb_ref[:, _WB_SM:_WB_END])


def _project(x2, norm_w, w_in, q_gain, k_gain, tm):
    t, d = x2.shape
    sizes = (A_HEADS * A_DK, A_HEADS * A_DK, A_WIDTH, A_WIDTH, A_HEADS, A_HEADS, B_WIDTH, B_KV_WIDTH,
             B_KV_WIDTH, B_WIDTH, IDX_WIDTH, IDX_DIM, IDX_HEADS)
    offs = [0]
    for s in sizes:
        offs.append(offs[-1] + s)
    (a_q, a_k, a_v, a_z, a_b, a_a, b_q, b_k, b_v, b_z, i_q, i_k, i_w) = [
        w_in[:, offs[n]:offs[n + 1]] for n in range(len(sizes))]
    pad = jnp.zeros((d, SMALL_W - 2 * A_HEADS - IDX_HEADS), F32)
    wb = jnp.concatenate([a_q, a_k, a_v, a_z, b_q, b_k, b_v, b_z, i_q, i_k, a_b, a_a, i_w, pad],
                         axis=1).astype(BF16)
    qg = jnp.tile(q_gain, LANES // B_HD)[None, :]
    kg = jnp.tile(k_gain, LANES // B_HD)[None, :]
    row = lambda w: pl.BlockSpec((tm, w), lambda i: (i, 0))
    grp = pl.BlockSpec((B_KV_HEADS, tm, LANES), lambda i: (0, i, 0))
    full = lambda a: pl.BlockSpec(a.shape, lambda i: (0, 0))
    nw = norm_w[None, :]
    sds = jax.ShapeDtypeStruct
    return pl.pallas_call(
        _proj_kernel,
        grid=(t // tm,),
        in_specs=[row(d), full(nw), full(wb), full(qg), full(kg)],
        out_specs=[row(A_QKV), row(A_WIDTH), row(B_HEADS * LANES), grp,
                   pl.BlockSpec((B_KV_HEADS, V_ROWS, tm), lambda i: (0, 0, i)), row(B_WIDTH), row(IDX_WIDTH),
                   row(IDX_DIM), row(SMALL_W)],
        out_shape=[sds((t, A_QKV), F32), sds((t, A_WIDTH), F32), sds((t, B_HEADS * LANES), BF16),
                   sds((B_KV_HEADS, t, LANES), BF16), sds((B_KV_HEADS, V_ROWS, t), BF16), sds((t, B_WIDTH), F32),
                   sds((t, IDX_WIDTH), BF16), sds((t, IDX_DIM), BF16), sds((t, SMALL_W), F32)],
        compiler_params=pltpu.CompilerParams(dimension_semantics=("arbitrary",), vmem_limit_bytes=VMEM_LIMIT),
        name="proj",
    )(x2, nw, wb, qg, kg)


def _gdn_kernel(qkv_ref, sm_ref, smt_ref, az_ref, cw_ref, alog_r_ref, dtb_r_ref, alog_c_ref, dtb_c_ref,
                onw_ref, o_ref, s_ref, tail_ref, *, bt):
    c = A_CHUNK

    @pl.when(pl.program_id(1) == 0)
    def _():
        s_ref[...] = jnp.zeros(s_ref.shape, F32)
        tail_ref[...] = jnp.zeros(tail_ref.shape, F32)

    xin = qkv_ref[...]
    tail = tail_ref[...]
    cw = cw_ref[...]
    row8 = lax.broadcasted_iota(I32, tail.shape, 0)
    acc = xin * cw[A_CONV - 1:A_CONV, :]
    for d in range(1, A_CONV):
        xr = pltpu.roll(xin, d, axis=0)
        pr = pltpu.roll(tail, d, axis=0)
        head = jnp.where(row8 < d, pr, xr[0:8])
        xs = jnp.concatenate([head, xr[8:]], axis=0)
        acc = acc + xs * cw[A_CONV - 1 - d:A_CONV - d, :]
    tail_ref[...] = xin[bt - 8:bt]
    qkv = _silu(acc)

    sm = sm_ref[...]
    smt = smt_ref[0]
    g_col = -jnp.exp(alog_r_ref[...]) * _softplus(sm + dtb_r_ref[...])
    g_row = -jnp.exp(alog_c_ref[...]) * _softplus(smt + dtb_c_ref[...])
    beta_all = jax.nn.sigmoid(sm)

    r = lax.broadcasted_iota(I32, (bt, bt), 0)
    s = lax.broadcasted_iota(I32, (bt, bt), 1)
    same = (r // c) == (s // c)
    tril = same & (s <= r)
    strict = same & (s < r)
    eye = jnp.where(s == r, 1.0, 0.0).astype(F32)
    gc_col = _dot(jnp.where(tril, 1.0, 0.0).astype(F32), g_col, HI)
    gc_row = _dot(g_row, jnp.where(same & (r <= s), 1.0, 0.0).astype(F32), HI)
    gtot = _dot(jnp.where(same, 1.0, 0.0).astype(F32), g_col, HI)
    egc_all = jnp.exp(gc_col)
    ekd_all = jnp.exp(gtot - gc_col)
    egl_all = jnp.exp(gtot)
    onw = onw_ref[...]

    heads = range(A_HEADS)
    qs, ks_, kbs, decays, amats = [], [], [], [], []
    for h in heads:
        q = qkv[:, h * A_DK:(h + 1) * A_DK]
        k = qkv[:, A_HEADS * A_DK + h * A_DK:A_HEADS * A_DK + (h + 1) * A_DK]
        q = q * lax.rsqrt(jnp.sum(q * q, axis=-1, keepdims=True) + EPS) * (A_DK ** -0.5)
        k = k * lax.rsqrt(jnp.sum(k * k, axis=-1, keepdims=True) + EPS)
        lane = SM_DECAY + h
        gcc = gc_col[:, lane:lane + 1]
        gcr = gc_row[lane:lane + 1, :]
        decay = jnp.where(tril, jnp.exp(jnp.where(tril, gcc - gcr, 0.0)), 0.0)
        kb = k * beta_all[:, SM_BETA + h:SM_BETA + h + 1]
        qs.append(q)
        ks_.append(k)
        kbs.append(kb)
        decays.append(decay)
        amats.append(jnp.where(strict, _dot_nt(kb.astype(BF16), k.astype(BF16)) * decay, 0.0))
    tmats = [eye - jnp.where((r // 2) == (s // 2), a, 0.0) for a in amats]
    size = 2
    while size < c:
        off = ((r // (2 * size)) == (s // (2 * size))) & ((r // size) != (s // size))
        t_bfs = [tmats[h].astype(BF16) for h in heads]
        tcs = [_dot(t_bfs[h], jnp.where(off, amats[h], 0.0).astype(BF16)).astype(BF16) for h in heads]
        for h in heads:
            tmats[h] = tmats[h] - _dot(tcs[h], t_bfs[h])
        size *= 2
    values, kcds, atts, qds, kds, egls = [], [], [], [], [], []
    for h in heads:
        lane = SM_DECAY + h
        v = qkv[:, 2 * A_HEADS * A_DK + h * A_DV:2 * A_HEADS * A_DK + (h + 1) * A_DV]
        t_bf = tmats[h].astype(BF16)
        egc = egc_all[:, lane:lane + 1]
        values.append(_dot(t_bf, (v * beta_all[:, SM_BETA + h:SM_BETA + h + 1]).astype(BF16)))
        kcds.append(_dot(t_bf, (kbs[h] * egc).astype(BF16)).astype(BF16))
        atts.append((_dot_nt(qs[h].astype(BF16), ks_[h].astype(BF16)) * decays[h]).astype(BF16))
        qds.append((qs[h] * egc).astype(BF16))
        kds.append((ks_[h] * ekd_all[:, lane:lane + 1]).astype(BF16))
        egls.append(egl_all[:, lane:lane + 1])
    states = [s_ref[h] for h in heads]
    o_inter = [[] for _ in heads]
    v_new = [[] for _ in heads]
    for ci in range(bt // c):
        rows = slice(ci * c, (ci + 1) * c)
        for h in heads:
            s_bf = states[h].astype(BF16)
            vn = (values[h][rows] - _dot(kcds[h][rows], s_bf)).astype(BF16)
            o_inter[h].append(_dot(qds[h][rows], s_bf))
            v_new[h].append(vn)
            gl = egls[h][ci * c:ci * c + 8]
            states[h] = states[h] * jnp.concatenate([gl] * (A_DK // 8), axis=0) + _dot_tn(kds[h][rows], vn)
    for h in heads:
        s_ref[h] = states[h]
        o = jnp.concatenate(o_inter[h], axis=0) + _dot(atts[h], jnp.concatenate(v_new[h], axis=0))
        on = o * lax.rsqrt(jnp.mean(o * o, axis=-1, keepdims=True) + EPS) * onw
        z = az_ref[:, h * A_DV:(h + 1) * A_DV]
        o_ref[:, h * A_DV:(h + 1) * A_DV] = on * _silu(z)


def _gdn(qkv, small, small_t, a_z, conv_w, a_log, dt_bias, out_norm_w, batch, seq, bt):
    nt = seq // bt
    lane_pad = lambda v: jnp.zeros((1, LANES), F32).at[0, SM_DECAY:SM_DECAY + A_HEADS].set(v)
    sub_pad = lambda v: jnp.zeros((8, 1), F32).at[SM_DECAY:SM_DECAY + A_HEADS, 0].set(v)
    cw = jnp.zeros((8, A_QKV), F32).at[:A_CONV].set(conv_w)
    onw = out_norm_w[None, :]
    row = lambda w: pl.BlockSpec((bt, w), lambda b, t: (b * nt + t, 0))
    full = lambda a: pl.BlockSpec(a.shape, lambda b, t: (0, 0))
    consts = (cw, lane_pad(a_log), lane_pad(dt_bias), sub_pad(a_log), sub_pad(dt_bias), onw)
    return pl.pallas_call(
        functools.partial(_gdn_kernel, bt=bt),
        grid=(batch, nt),
        in_specs=[row(A_QKV), row(SMALL_W), pl.BlockSpec((1, 8, bt), lambda b, t: (b, 0, t)), row(A_WIDTH)]
        + [full(a) for a in consts],
        out_specs=row(A_WIDTH),
        out_shape=jax.ShapeDtypeStruct((batch * seq, A_WIDTH), F32),
        scratch_shapes=[pltpu.VMEM((A_HEADS, A_DK, A_DV), F32), pltpu.VMEM((8, A_QKV), F32)],
        compiler_params=pltpu.CompilerParams(dimension_semantics=("arbitrary", "arbitrary"),
                                             vmem_limit_bytes=VMEM_LIMIT),
        name="gdn",
    )(qkv, small, small_t, a_z, *consts)


def _dsa_kernel(q_ref, kx_ref, vxt_ref, iq_ref, ik_ref, smt_ref, bz_ref, ltri_ref, o_ref,
                sc_ref, hb_ref, mx_ref, acc_ref, s_ref, *, qb, ks, n_sel):
    rep = B_HEADS // B_KV_HEADS
    i = pl.program_id(1)
    ns = (i * qb) // ks + 1
    kidx = lax.broadcasted_iota(I32, (ks, qb), 0)
    qpos = i * qb + lax.broadcasted_iota(I32, (ks, qb), 1)
    kidx_t = lax.broadcasted_iota(I32, (LANES, qb), 0)
    qpos_t = i * qb + lax.broadcasted_iota(I32, (LANES, qb), 1)

    iq = iq_ref[...]
    iqs = jnp.concatenate([iq[:, h * IDX_DIM:(h + 1) * IDX_DIM] for h in range(IDX_HEADS)], axis=0)
    iwt = smt_ref[0][SM_IW:SM_IW + IDX_HEADS, :] * (IDX_HEADS ** -0.5 * IDX_DIM ** -0.5)

    def score_body(j, carry):
        for u in range(ks // LANES):
            start = pl.multiple_of(j * ks + u * LANES, LANES)
            lg = _dot_nt(ik_ref[pl.ds(start, LANES), :], iqs)
            sc = jnp.zeros((LANES, qb), F32)
            for h in range(IDX_HEADS):
                sc = sc + iwt[h:h + 1, :] * jnp.maximum(lg[:, h * qb:(h + 1) * qb], 0.0)
            rows = slice(u * LANES, (u + 1) * LANES)
            sc = jnp.where(start + kidx_t <= qpos_t, sc, -jnp.inf)
            sc_ref[j, rows, :] = sc
            hb_ref[j, rows, :] = sc.astype(BF16)
        return carry

    lax.fori_loop(0, ns, score_body, 0)

    def count(src_ref, chunk, hit_of, dtype):
        def body(j, acc):
            for u in range(ks // chunk):
                acc = acc + hit_of(src_ref[j, u * chunk:(u + 1) * chunk, :])
            return acc

        acc = lax.fori_loop(0, ns, body, jnp.zeros((chunk, qb), dtype))
        return jnp.sum(acc.astype(F32), axis=0, keepdims=True)

    def key_value(key):
        val = pltpu.bitcast(key ^ ((key >> 31) & 0x7FFFFFFF), F32)
        return jnp.where(key < KEY_MIN_FINITE, -jnp.inf, val)

    def count_ge(key):
        cand = jnp.broadcast_to(key_value(key), (SUBLANES, qb))
        return count(sc_ref, SUBLANES, lambda sc: jnp.where(sc >= cand, 1.0, 0.0), F32)

    one_b = jnp.ones((SUBLANES_BF16, qb), BF16)
    zero_b = jnp.zeros((SUBLANES_BF16, qb), BF16)

    def coarse_value(key):
        bits = (key ^ ((key >> 31) & 0x7FFFFFFF)) & (-(1 << (32 - COARSE_BITS)))
        return jnp.where(key < KEY_MIN_FINITE, -jnp.inf, pltpu.bitcast(bits, F32))

    def count_coarse(hit_of):
        return count(hb_ref, SUBLANES_BF16, lambda hb: jnp.where(hit_of(hb), one_b, zero_b), BF16)

    def coarse_body(bi, tau):
        key = tau ^ lax.shift_left(jnp.int32(1), 31 - bi)
        cand = jnp.broadcast_to(coarse_value(key), (SUBLANES_BF16, qb)).astype(BF16)
        return jnp.where(count_coarse(lambda hb: hb >= cand) >= n_sel, key, tau)

    searching = (i + 1) * qb > n_sel
    coarse = lax.fori_loop(0, jnp.where(searching, COARSE_BITS, 0), coarse_body, jnp.full((1, qb), INT_MIN, I32))
    step = 1 << (32 - COARSE_BITS)
    centre = jnp.where(coarse < 0, coarse | (step - 1), coarse)
    lo0 = jnp.maximum(centre, INT_MIN + step) - (step >> 1)
    hi0 = centre + step
    at_zero = ((count_coarse(lambda hb: hb > zero_b) < n_sel) & (count_coarse(lambda hb: hb >= zero_b) >= n_sel))
    lo0 = jnp.where(at_zero, 0, lo0)
    hi0 = jnp.where(at_zero, 1, hi0)
    max_steps = jnp.where(searching, FINE_MAX_STEPS, 0)

    def fine_cond(state):
        it, _, _, _, open_rows = state
        return jnp.logical_and(it < max_steps, open_rows > 0.0)

    def fine_body(state):
        it, lo, hi, c_lo, _ = state
        settled = (c_lo == n_sel) | (hi - lo <= 1)
        open_rows = jnp.max(jnp.where(settled, 0.0, 1.0))
        mid = lo + ((hi - lo) >> 1)
        cnt = count_ge(mid)
        ok = cnt >= n_sel
        return (it + 1, jnp.where(ok, mid, lo), jnp.where(ok, hi, mid), jnp.where(ok, cnt, c_lo), open_rows)

    _, tau, _, _, _ = lax.while_loop(
        fine_cond, fine_body, (jnp.int32(0), lo0, hi0, jnp.full((1, qb), -1.0, F32), jnp.float32(1.0)))
    tau_val = key_value(tau)
    tau8 = jnp.broadcast_to(tau_val, (SUBLANES, qb))
    need = n_sel - count(sc_ref, SUBLANES, lambda sc: jnp.where(sc > tau8, 1.0, 0.0), F32)

    q = q_ref[...]
    qs = [jnp.concatenate([q[:, (g * rep + r) * LANES:(g * rep + r + 1) * LANES] for r in range(rep)], axis=0)
          for g in range(B_KV_HEADS)]
    mx_ref[...] = jnp.full(mx_ref.shape, NEG_BIG, F32)
    acc_ref[...] = jnp.zeros(acc_ref.shape, F32)
    ltri = ltri_ref[...]
    taub = jnp.broadcast_to(tau_val, (ks, qb))

    def att_body(j, eq_seen):
        start = pl.multiple_of(j * ks, ks)
        sct = sc_ref[j]
        eqf = jnp.where(sct == taub, 1.0, 0.0)
        pref = _dot(ltri, eqf.astype(BF16)) + eq_seen
        take = jnp.where(sct > taub, 1.0, jnp.where(pref <= need, eqf, 0.0))
        bias = jnp.where((take > 0.0) & (j * ks + kidx <= qpos), 0.0, 2.0 * NEG_BIG)
        alphas = []
        for g in range(B_KV_HEADS):
            kx = kx_ref[g, pl.ds(start, ks), :]
            for r in range(rep):
                cols = slice(r * qb, (r + 1) * qb)
                m_old = mx_ref[g, :, cols]
                m_new = m_old
                for kc in range(ks // LANES):
                    rows = slice(kc * LANES, (kc + 1) * LANES)
                    s = _dot_nt(kx[rows], qs[g][cols]) + bias[rows]
                    s_ref[g, rows, cols] = s
                    m_new = jnp.maximum(m_new, jnp.max(s, axis=0, keepdims=True))
                mx_ref[g, :, cols] = m_new
                alphas.append(jnp.exp2(m_old - m_new)[0:1, :])
        half = ks // 2
        for g in range(B_KV_HEADS):
            vxt = vxt_ref[g, :, pl.ds(start, ks)]
            for r in range(rep):
                cols = slice(r * qb, (r + 1) * qb)
                m_new = mx_ref[g, 0:1, cols]
                acc = alphas[g * rep + r] * acc_ref[g, :, cols]
                for kk in range(2):
                    rows = slice(kk * half, (kk + 1) * half)
                    p = jnp.exp2(s_ref[g, rows, cols] - m_new).astype(BF16)
                    acc = acc + _dot(vxt[:, rows], p)
                acc_ref[g, :, cols] = acc
        return eq_seen + jnp.sum(eqf, axis=0, keepdims=True)

    lax.fori_loop(0, ns, att_body, jnp.zeros((1, qb), F32))

    lane = lax.broadcasted_iota(I32, (qb, LANES), 1)
    outs = []
    for g in range(B_KV_HEADS):
        acc = acc_ref[g]
        acc = jnp.concatenate([acc, jnp.zeros((LANES - V_ROWS, rep * qb), F32)], axis=0)
        a = jnp.concatenate([jnp.transpose(acc[:, r * qb:(r + 1) * qb]) for r in range(rep)], axis=0)
        outs.append(a / a[:, B_HD:B_HD + 1])
    for cblk in range(B_WIDTH // LANES):
        g = (2 * cblk) // rep
        r0 = (2 * cblk) % rep
        a = outs[g][r0 * qb:(r0 + 1) * qb]
        b = pltpu.roll(outs[g][(r0 + 1) * qb:(r0 + 2) * qb], B_HD, axis=1)
        z = bz_ref[:, cblk * LANES:(cblk + 1) * LANES]
        o_ref[:, cblk * LANES:(cblk + 1) * LANES] = jnp.where(lane < B_HD, a, b) * _silu(z)


def _dsa(bqx, kx, vxt, iq, ik, small_t, bz, batch, seq, qb, ks):
    nq = seq // qb
    n_sel = min(TOPK_MAX, seq // 4)
    rep = B_HEADS // B_KV_HEADS
    ltri = (jnp.arange(ks)[None, :] <= jnp.arange(ks)[:, None]).astype(BF16)
    row = lambda w: pl.BlockSpec((qb, w), lambda b, i: (b * nq + i, 0))
    per_batch = lambda w: pl.BlockSpec((seq, w), lambda b, i: (b, 0))
    return pl.pallas_call(
        functools.partial(_dsa_kernel, qb=qb, ks=ks, n_sel=n_sel),
        grid=(batch, nq),
        in_specs=[row(B_HEADS * LANES),
                  pl.BlockSpec((B_KV_HEADS, seq, LANES), lambda b, i: (0, b, 0)),
                  pl.BlockSpec((B_KV_HEADS, V_ROWS, seq), lambda b, i: (0, 0, b)),
                  row(IDX_WIDTH), per_batch(IDX_DIM),
                  pl.BlockSpec((1, 16, qb), lambda b, i: (b, 0, i)),
                  row(B_WIDTH), pl.BlockSpec((ks, ks), lambda b, i: (0, 0))],
        out_specs=row(B_WIDTH),
        out_shape=jax.ShapeDtypeStruct((batch * seq, B_WIDTH), F32),
        scratch_shapes=[pltpu.VMEM((seq // ks, ks, qb), F32),
                        pltpu.VMEM((seq // ks, ks, qb), BF16),
                        pltpu.VMEM((B_KV_HEADS, SUBLANES, rep * qb), F32),
                        pltpu.VMEM((B_KV_HEADS, V_ROWS, rep * qb), F32),
                        pltpu.VMEM((B_KV_HEADS, ks, rep * qb), F32)],
        compiler_params=pltpu.CompilerParams(dimension_semantics=("arbitrary", "arbitrary"),
                                             vmem_limit_bytes=VMEM_LIMIT),
        name="dsa",
    )(bqx, kx, vxt, iq, ik, small_t, bz, ltri)


def _out_kernel(x_ref, oa_ref, ob_ref, p_ref, wo_ref, wp_ref, wg_ref, gn_ref, bg_ref, o_ref):
    x1 = (x_ref[...] + _dot(oa_ref[...].astype(BF16), wo_ref[0:A_WIDTH, :])
          + _dot(ob_ref[...].astype(BF16), wo_ref[A_WIDTH:A_WIDTH + B_WIDTH, :]))
    hn = x1 * lax.rsqrt(jnp.mean(x1 * x1, axis=-1, keepdims=True) + EPS) * gn_ref[...]
    gate = jax.nn.sigmoid(_dot(hn.astype(BF16), wg_ref[...]) + bg_ref[...])
    o_ref[...] = x1 + _dot(p_ref[...].astype(BF16), wp_ref[...]) * gate


def _output(x2, oa, ob, p2, w_out, w_ple, gate_norm_w, w_gate, b_gate, tm):
    t, d = x2.shape
    wo = w_out.astype(BF16)
    wp = w_ple.astype(BF16)
    wg = w_gate.astype(BF16)
    gn = gate_norm_w[None, :]
    bg = b_gate[None, :]
    row = lambda w: pl.BlockSpec((tm, w), lambda i: (i, 0))
    full = lambda a: pl.BlockSpec(a.shape, lambda i: (0, 0))
    return pl.pallas_call(
        _out_kernel,
        grid=(t // tm,),
        in_specs=[row(d), row(A_WIDTH), row(B_WIDTH), row(PLE_DIM)] + [full(a) for a in (wo, wp, wg, gn, bg)],
        out_specs=row(d),
        out_shape=jax.ShapeDtypeStruct((t, d), F32),
        compiler_params=pltpu.CompilerParams(dimension_semantics=("arbitrary",), vmem_limit_bytes=VMEM_LIMIT),
        name="out",
    )(x2, oa, ob, p2, wo, wp, wg, gn, bg)


def kernel(x, p, attn_norm_w, w_in, conv_w, a_log, dt_bias, a_out_norm_w, b_q_norm_w, b_k_norm_w, w_out,
           w_ple, ple_gate_norm_w, w_ple_gate, b_ple_gate):
    batch, seq, d = x.shape
    t = batch * seq
    tm = min(512, t)
    bt = min(256, seq)
    qb = min(512, seq)
    ks = min(512, seq)
    x2 = x.reshape(t, d)
    for i in range(w_in.shape[0]):
        qkv, az, bqx, kx, vxt, bz, iq, ik, small = _project(x2, attn_norm_w[i], w_in[i], b_q_norm_w[i],
                                                           b_k_norm_w[i], tm)
        small_t = jnp.swapaxes(small.reshape(batch, seq, SMALL_W)[:, :, :16], 1, 2)
        oa = _gdn(qkv, small, small_t, az, conv_w[i], a_log[i], dt_bias[i], a_out_norm_w[i], batch, seq, bt)
        ob = _dsa(bqx, kx, vxt, iq, ik, small_t, bz, batch, seq, qb, ks)
        x2 = _output(x2, oa, ob, p[i].reshape(t, PLE_DIM), w_out[i], w_ple[i], ple_gate_norm_w[i],
                     w_ple_gate[i], b_ple_gate[i], tm)
    return x2.reshape(batch, seq, d)
```

```python
import functools

import jax
import jax.numpy as jnp
from jax import lax
from jax.experimental import pallas as pl
from jax.experimental.pallas import tpu as pltpu

F32 = jnp.float32
BF16 = jnp.bfloat16
I32 = jnp.int32
EPS = 1e-6
HI = lax.Precision.HIGHEST

PLE_DIM = 256
A_HEADS = 4
A_DK = 128
A_DV = 128
A_CONV = 4
A_CHUNK = 64
A_WIDTH = A_HEADS * A_DV
A_QKV = 2 * A_HEADS * A_DK + A_WIDTH
B_HEADS = 8
B_KV_HEADS = 2
B_HD = 64
B_WIDTH = B_HEADS * B_HD
B_KV_WIDTH = B_KV_HEADS * B_HD
IDX_HEADS = 8
IDX_DIM = 128
IDX_WIDTH = IDX_HEADS * IDX_DIM
TOPK_MAX = 256
LANES = 128
SUBLANES = 8
SUBLANES_BF16 = 16
SMALL_W = LANES
SM_BETA = 0
SM_DECAY = A_HEADS
SM_IW = 2 * A_HEADS
V_ROWS = B_HD + 16

VMEM_LIMIT = 56 * 1024 * 1024
NEG_BIG = -1e30
LOG2_E = 1.4426950408889634
INT_MIN = -(2 ** 31)
KEY_MIN_FINITE = INT_MIN + 0x00800000
COARSE_BITS = 16
FINE_MAX_STEPS = 20


def _dot(a, b, prec=None):
    return jnp.dot(a, b, preferred_element_type=F32, precision=prec)


def _dot_nt(a, b, prec=None):
    return lax.dot_general(a, b, (((1,), (1,)), ((), ())), preferred_element_type=F32, precision=prec)


def _dot_tn(a, b, prec=None):
    return lax.dot_general(a, b, (((0,), (0,)), ((), ())), preferred_element_type=F32, precision=prec)


def _silu(x):
    return x * jax.nn.sigmoid(x)


def _softplus(x):
    return jnp.maximum(x, 0.0) + jnp.log1p(jnp.exp(-jnp.abs(x)))


def _seg_norm64(xb, gain_row):
    lane = lax.broadcasted_iota(I32, xb.shape, 1)
    lo = lane < B_HD
    sq = xb * xb
    s_lo = jnp.sum(jnp.where(lo, sq, 0.0), axis=-1, keepdims=True)
    s_hi = jnp.sum(jnp.where(lo, 0.0, sq), axis=-1, keepdims=True)
    ms = jnp.where(lo, s_lo, s_hi) * (1.0 / B_HD)
    return xb * lax.rsqrt(ms + EPS) * gain_row


_WB_QKV = 0
_WB_AZ = _WB_QKV + A_QKV
_WB_BQ = _WB_AZ + A_WIDTH
_WB_BK = _WB_BQ + B_WIDTH
_WB_BV = _WB_BK + B_KV_WIDTH
_WB_BZ = _WB_BV + B_KV_WIDTH
_WB_IQ = _WB_BZ + B_WIDTH
_WB_IK = _WB_IQ + IDX_WIDTH
_WB_SM = _WB_IK + IDX_DIM
_WB_END = _WB_SM + SMALL_W


def _proj_kernel(x_ref, nw_ref, wb_ref, qg_ref, kg_ref,
                 qkv_ref, az_ref, bqx_ref, kx_ref, vxt_ref, bz_ref, iq_ref, ik_ref, sm_ref):
    x = x_ref[...]
    h = x * lax.rsqrt(jnp.mean(x * x, axis=-1, keepdims=True) + EPS) * nw_ref[...]
    hb = h.astype(BF16)
    step = 512
    for c0 in range(0, A_QKV, step):
        qkv_ref[:, c0:c0 + step] = _dot(hb, wb_ref[:, _WB_QKV + c0:_WB_QKV + c0 + step])
    az_ref[...] = _dot(hb, wb_ref[:, _WB_AZ:_WB_BQ])
    bz_ref[...] = jnp.transpose(_dot(hb, wb_ref[:, _WB_BZ:_WB_IQ]))
    lane = lax.broadcasted_iota(I32, (x.shape[0], LANES), 1)
    lo = lane < B_HD
    bk = _seg_norm64(_dot(hb, wb_ref[:, _WB_BK:_WB_BV]), kg_ref[...])
    bv = _dot(hb, wb_ref[:, _WB_BV:_WB_BZ])
    k_tail = jnp.where(lane == B_HD, 1.0, 0.0)
    for g in range(B_KV_HEADS):
        kg = bk if g == 0 else pltpu.roll(bk, B_HD, axis=1)
        vg = bv if g == 0 else pltpu.roll(bv, B_HD, axis=1)
        kx_ref[g] = jnp.where(lo, kg, k_tail).astype(BF16)
        vxt_ref[g] = jnp.transpose(jnp.where(lo, vg, 1.0))[0:V_ROWS].astype(BF16)
    bq = _dot(hb, wb_ref[:, _WB_BQ:_WB_BK])
    scale = B_HD ** -0.5 * LOG2_E
    for c0 in range(0, B_WIDTH, LANES):
        nb = _seg_norm64(bq[:, c0:c0 + LANES], qg_ref[...]) * scale
        bqx_ref[:, 2 * c0:2 * c0 + LANES] = jnp.where(lo, nb, 0.0).astype(BF16)
        bqx_ref[:, 2 * c0 + LANES:2 * c0 + 2 * LANES] = jnp.where(
            lo, pltpu.roll(nb, B_HD, axis=1), 0.0).astype(BF16)
    for c0 in range(0, IDX_WIDTH, step):
        iq_ref[:, c0:c0 + step] = _dot(hb, wb_ref[:, _WB_IQ + c0:_WB_IQ + c0 + step]).astype(BF16)
    ik_ref[...] = _dot(hb, wb_ref[:, _WB_IK:_WB_SM]).astype(BF16)
    sm_ref[...] = _dot(hb, wb_ref[:, _WB_SM:_WB_END])


def _project(x2, norm_w, w_in, q_gain, k_gain, tm):
    t, d = x2.shape
    sizes = (A_HEADS * A_DK, A_HEADS * A_DK, A_WIDTH, A_WIDTH, A_HEADS, A_HEADS, B_WIDTH, B_KV_WIDTH,
             B_KV_WIDTH, B_WIDTH, IDX_WIDTH, IDX_DIM, IDX_HEADS)
    offs = [0]
    for s in sizes:
        offs.append(offs[-1] + s)
    (a_q, a_k, a_v, a_z, a_b, a_a, b_q, b_k, b_v, b_z, i_q, i_k, i_w) = [
        w_in[:, offs[n]:offs[n + 1]] for n in range(len(sizes))]
    pad = jnp.zeros((d, SMALL_W - 2 * A_HEADS - IDX_HEADS), F32)
    wb = jnp.concatenate([a_q, a_k, a_v, a_z, b_q, b_k, b_v, b_z, i_q, i_k, a_b, a_a, i_w, pad],
                         axis=1).astype(BF16)
    qg = jnp.tile(q_gain, LANES // B_HD)[None, :]
    kg = jnp.tile(k_gain, LANES // B_HD)[None, :]
    row = lambda w: pl.BlockSpec((tm, w), lambda i: (i, 0))
    grp = pl.BlockSpec((B_KV_HEADS, tm, LANES), lambda i: (0, i, 0))
    full = lambda a: pl.BlockSpec(a.shape, lambda i: (0, 0))
    nw = norm_w[None, :]
    sds = jax.ShapeDtypeStruct
    return pl.pallas_call(
        _proj_kernel,
        grid=(t // tm,),
        in_specs=[row(d), full(nw), full(wb), full(qg), full(kg)],
        out_specs=[row(A_QKV), row(A_WIDTH), row(B_HEADS * LANES), grp,
                   pl.BlockSpec((B_KV_HEADS, V_ROWS, tm), lambda i: (0, 0, i)),
                   pl.BlockSpec((B_WIDTH, tm), lambda i: (0, i)), row(IDX_WIDTH),
                   row(IDX_DIM), row(SMALL_W)],
        out_shape=[sds((t, A_QKV), F32), sds((t, A_WIDTH), F32), sds((t, B_HEADS * LANES), BF16),
                   sds((B_KV_HEADS, t, LANES), BF16), sds((B_KV_HEADS, V_ROWS, t), BF16), sds((B_WIDTH, t), F32),
                   sds((t, IDX_WIDTH), BF16), sds((t, IDX_DIM), BF16), sds((t, SMALL_W), F32)],
        compiler_params=pltpu.CompilerParams(dimension_semantics=("arbitrary",), vmem_limit_bytes=VMEM_LIMIT),
        name="proj",
    )(x2, nw, wb, qg, kg)


def _gdn_kernel(qkv_ref, sm_ref, smt_ref, az_ref, cw_ref, alog_r_ref, dtb_r_ref, alog_c_ref, dtb_c_ref,
                onw_ref, o_ref, s_ref, tail_ref, *, bt):
    c = A_CHUNK

    @pl.when(pl.program_id(1) == 0)
    def _():
        s_ref[...] = jnp.zeros(s_ref.shape, F32)
        tail_ref[...] = jnp.zeros(tail_ref.shape, F32)

    xin = qkv_ref[...]
    tail = tail_ref[...]
    cw = cw_ref[...]
    row8 = lax.broadcasted_iota(I32, tail.shape, 0)
    acc = xin * cw[A_CONV - 1:A_CONV, :]
    for d in range(1, A_CONV):
        xr = pltpu.roll(xin, d, axis=0)
        pr = pltpu.roll(tail, d, axis=0)
        head = jnp.where(row8 < d, pr, xr[0:8])
        xs = jnp.concatenate([head, xr[8:]], axis=0)
        acc = acc + xs * cw[A_CONV - 1 - d:A_CONV - d, :]
    tail_ref[...] = xin[bt - 8:bt]
    qkv = _silu(acc)

    sm = sm_ref[...]
    smt = smt_ref[0]
    g_col = -jnp.exp(alog_r_ref[...]) * _softplus(sm + dtb_r_ref[...])
    g_row = -jnp.exp(alog_c_ref[...]) * _softplus(smt + dtb_c_ref[...])
    beta_all = jax.nn.sigmoid(sm)

    r = lax.broadcasted_iota(I32, (bt, bt), 0)
    s = lax.broadcasted_iota(I32, (bt, bt), 1)
    same = (r // c) == (s // c)
    tril = same & (s <= r)
    strict = same & (s < r)
    eye = jnp.where(s == r, 1.0, 0.0).astype(F32)
    gc_col = _dot(jnp.where(tril, 1.0, 0.0).astype(F32), g_col, HI)
    gc_row = _dot(g_row, jnp.where(same & (r <= s), 1.0, 0.0).astype(F32), HI)
    gtot = _dot(jnp.where(same, 1.0, 0.0).astype(F32), g_col, HI)
    egc_all = jnp.exp(gc_col)
    ekd_all = jnp.exp(gtot - gc_col)
    egl_all = jnp.exp(gtot)
    onw = onw_ref[...]

    heads = range(A_HEADS)
    qs, ks_, kbs, decays, amats = [], [], [], [], []
    for h in heads:
        q = qkv[:, h * A_DK:(h + 1) * A_DK]
        k = qkv[:, A_HEADS * A_DK + h * A_DK:A_HEADS * A_DK + (h + 1) * A_DK]
        q = q * lax.rsqrt(jnp.sum(q * q, axis=-1, keepdims=True) + EPS) * (A_DK ** -0.5)
        k = k * lax.rsqrt(jnp.sum(k * k, axis=-1, keepdims=True) + EPS)
        lane = SM_DECAY + h
        gcc = gc_col[:, lane:lane + 1]
        gcr = gc_row[lane:lane + 1, :]
        decay = jnp.where(tril, jnp.exp(jnp.where(tril, gcc - gcr, 0.0)), 0.0)
        kb = k * beta_all[:, SM_BETA + h:SM_BETA + h + 1]
        qs.append(q)
        ks_.append(k)
        kbs.append(kb)
        decays.append(decay)
        amats.append(jnp.where(strict, _dot_nt(kb.astype(BF16), k.astype(BF16)) * decay, 0.0))
    tmats = [eye - jnp.where((r // 2) == (s // 2), a, 0.0) for a in amats]
    size = 2
    while size < c:
        off = ((r // (2 * size)) == (s // (2 * size))) & ((r // size) != (s // size))
        t_bfs = [tmats[h].astype(BF16) for h in heads]
        tcs = [_dot(t_bfs[h], jnp.where(off, amats[h], 0.0).astype(BF16)).astype(BF16) for h in heads]
        for h in heads:
            tmats[h] = tmats[h] - _dot(tcs[h], t_bfs[h])
        size *= 2
    values, kcds, atts, qds, kds, egls = [], [], [], [], [], []
    for h in heads:
        lane = SM_DECAY + h
        v = qkv[:, 2 * A_HEADS * A_DK + h * A_DV:2 * A_HEADS * A_DK + (h + 1) * A_DV]
        t_bf = tmats[h].astype(BF16)
        egc = egc_all[:, lane:lane + 1]
        values.append(_dot(t_bf, (v * beta_all[:, SM_BETA + h:SM_BETA + h + 1]).astype(BF16)))
        kcds.append(_dot(t_bf, (kbs[h] * egc).astype(BF16)).astype(BF16))
        atts.append((_dot_nt(qs[h].astype(BF16), ks_[h].astype(BF16)) * decays[h]).astype(BF16))
        qds.append((qs[h] * egc).astype(BF16))
        kds.append((ks_[h] * ekd_all[:, lane:lane + 1]).astype(BF16))
        egls.append(egl_all[:, lane:lane + 1])
    states = [s_ref[h] for h in heads]
    o_inter = [[] for _ in heads]
    v_new = [[] for _ in heads]
    for ci in range(bt // c):
        rows = slice(ci * c, (ci + 1) * c)
        for h in heads:
            s_bf = states[h].astype(BF16)
            vn = (values[h][rows] - _dot(kcds[h][rows], s_bf)).astype(BF16)
            o_inter[h].append(_dot(qds[h][rows], s_bf))
            v_new[h].append(vn)
            gl = egls[h][ci * c:ci * c + 8]
            states[h] = states[h] * jnp.concatenate([gl] * (A_DK // 8), axis=0) + _dot_tn(kds[h][rows], vn)
    for h in heads:
        s_ref[h] = states[h]
        o = jnp.concatenate(o_inter[h], axis=0) + _dot(atts[h], jnp.concatenate(v_new[h], axis=0))
        on = o * lax.rsqrt(jnp.mean(o * o, axis=-1, keepdims=True) + EPS) * onw
        z = az_ref[:, h * A_DV:(h + 1) * A_DV]
        o_ref[:, h * A_DV:(h + 1) * A_DV] = on * _silu(z)


def _gdn(qkv, small, small_t, a_z, conv_w, a_log, dt_bias, out_norm_w, batch, seq, bt):
    nt = seq // bt
    lane_pad = lambda v: jnp.zeros((1, LANES), F32).at[0, SM_DECAY:SM_DECAY + A_HEADS].set(v)
    sub_pad = lambda v: jnp.zeros((8, 1), F32).at[SM_DECAY:SM_DECAY + A_HEADS, 0].set(v)
    cw = jnp.zeros((8, A_QKV), F32).at[:A_CONV].set(conv_w)
    onw = out_norm_w[None, :]
    row = lambda w: pl.BlockSpec((bt, w), lambda b, t: (b * nt + t, 0))
    full = lambda a: pl.BlockSpec(a.shape, lambda b, t: (0, 0))
    consts = (cw, lane_pad(a_log), lane_pad(dt_bias), sub_pad(a_log), sub_pad(dt_bias), onw)
    return pl.pallas_call(
        functools.partial(_gdn_kernel, bt=bt),
        grid=(batch, nt),
        in_specs=[row(A_QKV), row(SMALL_W), pl.BlockSpec((1, 8, bt), lambda b, t: (b, 0, t)), row(A_WIDTH)]
        + [full(a) for a in consts],
        out_specs=row(A_WIDTH),
        out_shape=jax.ShapeDtypeStruct((batch * seq, A_WIDTH), F32),
        scratch_shapes=[pltpu.VMEM((A_HEADS, A_DK, A_DV), F32), pltpu.VMEM((8, A_QKV), F32)],
        compiler_params=pltpu.CompilerParams(dimension_semantics=("arbitrary", "arbitrary"),
                                             vmem_limit_bytes=VMEM_LIMIT),
        name="gdn",
    )(qkv, small, small_t, a_z, *consts)


def _dsa_kernel(q_ref, kx_ref, vxt_ref, iq_ref, ik_ref, smt_ref, bz_ref, ltri_ref, o_ref,
                sc_ref, hb_ref, mx_ref, acc_ref, s_ref, *, qb, ks, n_sel):
    rep = B_HEADS // B_KV_HEADS
    i = pl.program_id(1)
    ns = (i * qb) // ks + 1
    kidx = lax.broadcasted_iota(I32, (ks, qb), 0)
    qpos = i * qb + lax.broadcasted_iota(I32, (ks, qb), 1)
    kidx_t = lax.broadcasted_iota(I32, (LANES, qb), 0)
    qpos_t = i * qb + lax.broadcasted_iota(I32, (LANES, qb), 1)

    iq = iq_ref[...]
    iqs = jnp.concatenate([iq[:, h * IDX_DIM:(h + 1) * IDX_DIM] for h in range(IDX_HEADS)], axis=0)
    iwt = smt_ref[0][SM_IW:SM_IW + IDX_HEADS, :] * (IDX_HEADS ** -0.5 * IDX_DIM ** -0.5)

    def score_body(j, carry):
        for u in range(ks // LANES):
            start = pl.multiple_of(j * ks + u * LANES, LANES)
            lg = _dot_nt(ik_ref[pl.ds(start, LANES), :], iqs)
            sc = jnp.zeros((LANES, qb), F32)
            for h in range(IDX_HEADS):
                sc = sc + iwt[h:h + 1, :] * jnp.maximum(lg[:, h * qb:(h + 1) * qb], 0.0)
            rows = slice(u * LANES, (u + 1) * LANES)
            sc = jnp.where(start + kidx_t <= qpos_t, sc, -jnp.inf)
            sc_ref[j, rows, :] = sc
            hb_ref[j, rows, :] = sc.astype(BF16)
        return carry

    lax.fori_loop(0, ns, score_body, 0)

    def count(src_ref, chunk, hit_of, dtype):
        def body(j, acc):
            for u in range(ks // chunk):
                acc = acc + hit_of(src_ref[j, u * chunk:(u + 1) * chunk, :])
            return acc

        acc = lax.fori_loop(0, ns, body, jnp.zeros((chunk, qb), dtype))
        return jnp.sum(acc.astype(F32), axis=0, keepdims=True)

    def key_value(key):
        val = pltpu.bitcast(key ^ ((key >> 31) & 0x7FFFFFFF), F32)
        return jnp.where(key < KEY_MIN_FINITE, -jnp.inf, val)

    def count_ge(key):
        cand = jnp.broadcast_to(key_value(key), (SUBLANES, qb))
        return count(sc_ref, SUBLANES, lambda sc: jnp.where(sc >= cand, 1.0, 0.0), F32)

    one_b = jnp.ones((SUBLANES_BF16, qb), BF16)
    zero_b = jnp.zeros((SUBLANES_BF16, qb), BF16)

    def coarse_value(key):
        bits = (key ^ ((key >> 31) & 0x7FFFFFFF)) & (-(1 << (32 - COARSE_BITS)))
        return jnp.where(key < KEY_MIN_FINITE, -jnp.inf, pltpu.bitcast(bits, F32))

    def count_coarse(hit_of):
        return count(hb_ref, SUBLANES_BF16, lambda hb: jnp.where(hit_of(hb), one_b, zero_b), BF16)

    def coarse_body(bi, tau):
        key = tau ^ lax.shift_left(jnp.int32(1), 31 - bi)
        cand = jnp.broadcast_to(coarse_value(key), (SUBLANES_BF16, qb)).astype(BF16)
        return jnp.where(count_coarse(lambda hb: hb >= cand) >= n_sel, key, tau)

    searching = (i + 1) * qb > n_sel
    coarse = lax.fori_loop(0, jnp.where(searching, COARSE_BITS, 0), coarse_body, jnp.full((1, qb), INT_MIN, I32))
    step = 1 << (32 - COARSE_BITS)
    centre = jnp.where(coarse < 0, coarse | (step - 1), coarse)
    lo0 = jnp.maximum(centre, INT_MIN + step) - (step >> 1)
    hi0 = centre + step
    at_zero = ((count_coarse(lambda hb: hb > zero_b) < n_sel) & (count_coarse(lambda hb: hb >= zero_b) >= n_sel))
    lo0 = jnp.where(at_zero, 0, lo0)
    hi0 = jnp.where(at_zero, 1, hi0)
    max_steps = jnp.where(searching, FINE_MAX_STEPS, 0)

    def fine_cond(state):
        it, _, _, _, open_rows = state
        return jnp.logical_and(it < max_steps, open_rows > 0.0)

    def fine_body(state):
        it, lo, hi, c_lo, _ = state
        settled = (c_lo == n_sel) | (hi - lo <= 1)
        open_rows = jnp.max(jnp.where(settled, 0.0, 1.0))
        mid = lo + ((hi - lo) >> 1)
        cnt = count_ge(mid)
        ok = cnt >= n_sel
        return (it + 1, jnp.where(ok, mid, lo), jnp.where(ok, hi, mid), jnp.where(ok, cnt, c_lo), open_rows)

    _, tau, _, _, _ = lax.while_loop(
        fine_cond, fine_body, (jnp.int32(0), lo0, hi0, jnp.full((1, qb), -1.0, F32), jnp.float32(1.0)))
    tau_val = key_value(tau)
    tau8 = jnp.broadcast_to(tau_val, (SUBLANES, qb))
    need = n_sel - count(sc_ref, SUBLANES, lambda sc: jnp.where(sc > tau8, 1.0, 0.0), F32)

    q = q_ref[...]
    qs = [jnp.concatenate([q[:, (g * rep + r) * LANES:(g * rep + r + 1) * LANES] for r in range(rep)], axis=0)
          for g in range(B_KV_HEADS)]
    mx_ref[...] = jnp.full(mx_ref.shape, NEG_BIG, F32)
    acc_ref[...] = jnp.zeros(acc_ref.shape, F32)
    ltri = ltri_ref[...]
    taub = jnp.broadcast_to(tau_val, (ks, qb))

    def att_body(j, eq_seen):
        start = pl.multiple_of(j * ks, ks)
        sct = sc_ref[j]
        eqf = jnp.where(sct == taub, 1.0, 0.0)
        pref = _dot(ltri, eqf.astype(BF16)) + eq_seen
        take = jnp.where(sct > taub, 1.0, jnp.where(pref <= need, eqf, 0.0))
        bias = jnp.where((take > 0.0) & (j * ks + kidx <= qpos), 0.0, 2.0 * NEG_BIG)
        alphas = []
        for g in range(B_KV_HEADS):
            kx = kx_ref[g, pl.ds(start, ks), :]
            for r in range(rep):
                cols = slice(r * qb, (r + 1) * qb)
                m_old = mx_ref[g, :, cols]
                m_new = m_old
                for kc in range(ks // LANES):
                    rows = slice(kc * LANES, (kc + 1) * LANES)
                    s = _dot_nt(kx[rows], qs[g][cols]) + bias[rows]
                    s_ref[g, rows, cols] = s
                    m_new = jnp.maximum(m_new, jnp.max(s, axis=0, keepdims=True))
                mx_ref[g, :, cols] = m_new
                alphas.append(jnp.exp2(m_old - m_new)[0:1, :])
        half = ks // 2
        for g in range(B_KV_HEADS):
            vxt = vxt_ref[g, :, pl.ds(start, ks)]
            for r in range(rep):
                cols = slice(r * qb, (r + 1) * qb)
                m_new = mx_ref[g, 0:1, cols]
                acc = alphas[g * rep + r] * acc_ref[g, :, cols]
                for kk in range(2):
                    rows = slice(kk * half, (kk + 1) * half)
                    p = jnp.exp2(s_ref[g, rows, cols] - m_new).astype(BF16)
                    acc = acc + _dot(vxt[:, rows], p)
                acc_ref[g, :, cols] = acc
        return eq_seen + jnp.sum(eqf, axis=0, keepdims=True)

    lax.fori_loop(0, ns, att_body, jnp.zeros((1, qb), F32))

    for g in range(B_KV_HEADS):
        for r in range(rep):
            blk = acc_ref[g, :, r * qb:(r + 1) * qb]
            rows = slice((g * rep + r) * B_HD, (g * rep + r + 1) * B_HD)
            o_ref[rows, :] = blk[0:B_HD] / blk[B_HD:B_HD + 1] * _silu(bz_ref[rows, :])


def _dsa(bqx, kx, vxt, iq, ik, small_t, bz, batch, seq, qb, ks):
    nq = seq // qb
    n_sel = min(TOPK_MAX, seq // 4)
    rep = B_HEADS // B_KV_HEADS
    ltri = (jnp.arange(ks)[None, :] <= jnp.arange(ks)[:, None]).astype(BF16)
    row = lambda w: pl.BlockSpec((qb, w), lambda b, i: (b * nq + i, 0))
    per_batch = lambda w: pl.BlockSpec((seq, w), lambda b, i: (b, 0))
    return pl.pallas_call(
        functools.partial(_dsa_kernel, qb=qb, ks=ks, n_sel=n_sel),
        grid=(batch, nq),
        in_specs=[row(B_HEADS * LANES),
                  pl.BlockSpec((B_KV_HEADS, seq, LANES), lambda b, i: (0, b, 0)),
                  pl.BlockSpec((B_KV_HEADS, V_ROWS, seq), lambda b, i: (0, 0, b)),
                  row(IDX_WIDTH), per_batch(IDX_DIM),
                  pl.BlockSpec((1, 16, qb), lambda b, i: (b, 0, i)),
                  pl.BlockSpec((B_WIDTH, qb), lambda b, i: (0, b * nq + i)),
                  pl.BlockSpec((ks, ks), lambda b, i: (0, 0))],
        out_specs=pl.BlockSpec((B_WIDTH, qb), lambda b, i: (0, b * nq + i)),
        out_shape=jax.ShapeDtypeStruct((B_WIDTH, batch * seq), F32),
        scratch_shapes=[pltpu.VMEM((seq // ks, ks, qb), F32),
                        pltpu.VMEM((seq // ks, ks, qb), BF16),
                        pltpu.VMEM((B_KV_HEADS, SUBLANES, rep * qb), F32),
                        pltpu.VMEM((B_KV_HEADS, V_ROWS, rep * qb), F32),
                        pltpu.VMEM((B_KV_HEADS, ks, rep * qb), F32)],
        compiler_params=pltpu.CompilerParams(dimension_semantics=("arbitrary", "arbitrary"),
                                             vmem_limit_bytes=VMEM_LIMIT),
        name="dsa",
    )(bqx, kx, vxt, iq, ik, small_t, bz, ltri)


def _out_kernel(x_ref, oa_ref, ob_ref, p_ref, wo_ref, wp_ref, wg_ref, gn_ref, bg_ref, o_ref):
    x1 = (x_ref[...] + _dot(oa_ref[...].astype(BF16), wo_ref[0:A_WIDTH, :])
          + _dot_tn(ob_ref[...].astype(BF16), wo_ref[A_WIDTH:A_WIDTH + B_WIDTH, :]))
    hn = x1 * lax.rsqrt(jnp.mean(x1 * x1, axis=-1, keepdims=True) + EPS) * gn_ref[...]
    gate = jax.nn.sigmoid(_dot(hn.astype(BF16), wg_ref[...]) + bg_ref[...])
    o_ref[...] = x1 + _dot(p_ref[...].astype(BF16), wp_ref[...]) * gate


def _output(x2, oa, ob, p2, w_out, w_ple, gate_norm_w, w_gate, b_gate, tm):
    t, d = x2.shape
    wo = w_out.astype(BF16)
    wp = w_ple.astype(BF16)
    wg = w_gate.astype(BF16)
    gn = gate_norm_w[None, :]
    bg = b_gate[None, :]
    row = lambda w: pl.BlockSpec((tm, w), lambda i: (i, 0))
    full = lambda a: pl.BlockSpec(a.shape, lambda i: (0, 0))
    return pl.pallas_call(
        _out_kernel,
        grid=(t // tm,),
        in_specs=[row(d), row(A_WIDTH), pl.BlockSpec((B_WIDTH, tm), lambda i: (0, i)), row(PLE_DIM)]
        + [full(a) for a in (wo, wp, wg, gn, bg)],
        out_specs=row(d),
        out_shape=jax.ShapeDtypeStruct((t, d), F32),
        compiler_params=pltpu.CompilerParams(dimension_semantics=("arbitrary",), vmem_limit_bytes=VMEM_LIMIT),
        name="out",
    )(x2, oa, ob, p2, wo, wp, wg, gn, bg)


def kernel(x, p, attn_norm_w, w_in, conv_w, a_log, dt_bias, a_out_norm_w, b_q_norm_w, b_k_norm_w, w_out,
           w_ple, ple_gate_norm_w, w_ple_gate, b_ple_gate):
    batch, seq, d = x.shape
    t = batch * seq
    tm = min(512, t)
    bt = min(256, seq)
    qb = min(512, seq)
    ks = min(512, seq)
    x2 = x.reshape(t, d)
    for i in range(w_in.shape[0]):
        qkv, az, bqx, kx, vxt, bz, iq, ik, small = _project(x2, attn_norm_w[i], w_in[i], b_q_norm_w[i],
                                                           b_k_norm_w[i], tm)
        small_t = jnp.swapaxes(small.reshape(batch, seq, SMALL_W)[:, :, :16], 1, 2)
        oa = _gdn(qkv, small, small_t, az, conv_w[i], a_log[i], dt_bias[i], a_out_norm_w[i], batch, seq, bt)
        ob = _dsa(bqx, kx, vxt, iq, ik, small_t, bz, batch, seq, qb, ks)
        x2 = _output(x2, oa, ob, p[i].reshape(t, PLE_DIM), w_out[i], w_ple[i], ple_gate_norm_w[i],
                     w_ple_gate[i], b_ple_gate[i], tm)
    return x2.reshape(batch, seq, d)
```
